```python
import jax, jax.numpy as jnp
from jax import lax
import numpy as np

D_MODEL = 1024
BATCH = 8
SEQ = 2048
DEPTH = 4
DEC_BATCH = 128
DEC_SEQ = 1
PAST_LEN = 2048
PAGE_SIZE = 128

N_MIXERS = 2
N_LAYERS_A = (DEPTH + 1) // 2
N_LAYERS_B = DEPTH // 2
HEAD_DIM = 64
N_HEADS = D_MODEL // HEAD_DIM
N_PROJ_A = 4
N_SHIFT_A = 6
DECAY_LORA = max(32, int(round(1.8 * D_MODEL ** 0.5 / 32)) * 32)
AAA_LORA = max(32, int(round(1.8 * D_MODEL ** 0.5 / 32)) * 32)
RMS_EPS = 1e-6
LNX_EPS = 64e-5
Q_BLOCK = 128
ATTN_SCALE = HEAD_DIM ** -0.5

kernel_name = "rwkv7_fox_hybrid_step"


def _rmsnorm(x, g):
    xf = x.astype(jnp.float32)
    y = xf * lax.rsqrt(jnp.mean(xf * xf, axis=-1, keepdims=True) + RMS_EPS)
    return (y * g.astype(jnp.float32)).astype(x.dtype)


def _wkv_step(S, inp):
    r, w, k, v, a, b = inp
    sa = jnp.einsum('bhij,bhj->bhi', S, a)
    S = S * w[:, :, None, :] + sa[..., None] * b[:, :, None, :] + v[..., None] * k[:, :, None, :]
    return S, jnp.einsum('bhij,bhj->bhi', S, r)


def _rwkv7_branch(h, h_prev0, S0, mu, w_in, w0, w1, w2, a0, a1, a2, k_k, k_a, r_k, lnx_w, lnx_b):
    B, T, D = h.shape
    f32 = jnp.float32
    h_prev = jnp.concatenate([h_prev0[:, None, :].astype(h.dtype), h[:, :-1, :]], axis=1)
    xx = h_prev - h
    xs = h[None] + xx[None] * mu[:, None, None, :]
    proj = jnp.einsum('sbtd,sde->sbte', xs[:N_PROJ_A], w_in)
    r, k, v, gate = proj[0], proj[1], proj[2], proj[3]
    xw, xa = xs[4], xs[5]
    w_log = (-jax.nn.softplus(-(w0 + jnp.tanh(xw @ w1) @ w2)) - 0.5).astype(f32)
    decay = jnp.exp(-jnp.exp(w_log))
    a = jax.nn.sigmoid((a0 + (xa @ a1) @ a2).astype(f32))
    hd = lambda t: t.reshape(B, T, N_HEADS, HEAD_DIM)
    kf = k.astype(f32)
    kk = hd(kf * k_k.astype(f32))
    kk = kk / jnp.maximum(jnp.sqrt(jnp.sum(kk * kk, axis=-1, keepdims=True)), 1e-12)
    kf = kf * (1.0 + (a - 1.0) * k_a.astype(f32))
    rh, kh, vh, ah = hd(r.astype(f32)), hd(kf), hd(v.astype(f32)), hd(a)
    seq = (rh, hd(decay), kh, vh, -kk, kk * ah)
    seq = tuple(jnp.moveaxis(t, 1, 0) for t in seq)
    S_T, o = lax.scan(_wkv_step, S0.astype(f32), seq)
    o = jnp.moveaxis(o, 0, 1)
    mean = jnp.mean(o, axis=-1, keepdims=True)
    var = jnp.mean(jnp.square(o - mean), axis=-1, keepdims=True)
    o = (o - mean) * lax.rsqrt(var + LNX_EPS)
    o = o * lnx_w.astype(f32).reshape(N_HEADS, HEAD_DIM) + lnx_b.astype(f32).reshape(N_HEADS, HEAD_DIM)
    o = o + jnp.sum(rh * kh * r_k.astype(f32), axis=-1, keepdims=True) * vh
    o = o.reshape(B, T, D) * jax.nn.silu(gate.astype(f32))
    return o.astype(h.dtype), S_T, h[:, -1, :]


def _fox_project(h, w_in, b_f, qn_g, kn_g):
    B, T, D = h.shape
    proj = h @ w_in
    q, k, v, gate, f_logit = jnp.split(proj, [D, 2 * D, 3 * D, 4 * D], axis=-1)
    q = _rmsnorm(q.reshape(B, T, N_HEADS, HEAD_DIM), qn_g)
    k = _rmsnorm(k.reshape(B, T, N_HEADS, HEAD_DIM), kn_g)
    v = v.reshape(B, T, N_HEADS, HEAD_DIM)
    logf = jax.nn.log_sigmoid((f_logit + b_f).astype(jnp.float32))
    return q, k, v, gate, logf


def _fox_attend(q, k, v, cq, ck, qpos, kpos):
    s = jnp.einsum('bqhe,bkhe->bhqk', q, k, preferred_element_type=jnp.float32) * ATTN_SCALE
    s = s + jnp.swapaxes(cq, 1, 2)[:, :, :, None] - jnp.swapaxes(ck, 1, 2)[:, :, None, :]
    s = jnp.where((kpos[None, :] <= qpos[:, None])[None, None], s, -jnp.inf)
    p = jax.nn.softmax(s, axis=-1)
    return jnp.einsum('bhqk,bkhe->bqhe', p.astype(v.dtype), v)


def _fox_blocked(q, k, v, cq, ck, qpos, kpos):
    B, Tq, H, E = q.shape
    if Tq <= Q_BLOCK or Tq % Q_BLOCK:
        return _fox_attend(q, k, v, cq, ck, qpos, kpos)
    nb = Tq // Q_BLOCK
    qs = jnp.moveaxis(q.reshape(B, nb, Q_BLOCK, H, E), 1, 0)
    cqs = jnp.moveaxis(cq.reshape(B, nb, Q_BLOCK, H), 1, 0)
    ps = qpos.reshape(nb, Q_BLOCK)
    o = lax.map(lambda a: _fox_attend(a[0], k, v, a[1], ck, a[2], kpos), (qs, cqs, ps))
    return jnp.moveaxis(o, 0, 1).reshape(B, Tq, H, E)


def setup_inputs(seed: int = 0) -> dict:
    key = jax.random.key(seed)
    ks = iter(jax.random.split(key, 48))
    f32 = jnp.float32
    nrm = lambda shape, s=1.0: s * jax.random.normal(next(ks), shape, f32)
    uni = lambda shape, lo, hi: jax.random.uniform(next(ks), shape, f32, lo, hi)
    D, H, E = D_MODEL, N_HEADS, HEAD_DIM
    n_pages = PAST_LEN // PAGE_SIZE
    n_used = DEC_BATCH * n_pages
    n_pool = n_used + max(1, n_used // 4)
    page_table = jax.random.permutation(next(ks), n_pool)[:n_used].reshape(DEC_BATCH, n_pages).astype(jnp.int32)
    return {
        "x_prompt": nrm((BATCH, SEQ, D)),
        "x_sample": nrm((DEC_BATCH, DEC_SEQ, D)),
        "state_wkv": nrm((N_LAYERS_A, DEC_BATCH, H, E, E), 0.1),
        "state_shift": nrm((N_LAYERS_A, DEC_BATCH, D)),
        "cache_k": nrm((N_LAYERS_B, n_pool, PAGE_SIZE, H, E)),
        "cache_v": nrm((N_LAYERS_B, n_pool, PAGE_SIZE, H, E)),
        "cache_logf": jax.nn.log_sigmoid(3.0 + nrm((N_LAYERS_B, n_pool, PAGE_SIZE, H))),
        "page_table": page_table,
        "norm_a": 1.0 + nrm((N_LAYERS_A, D), 0.1),
        "mu_a": uni((N_LAYERS_A, N_SHIFT_A, D), 0.0, 1.0),
        "w_in_a": nrm((N_LAYERS_A, N_PROJ_A, D, D), D ** -0.5),
        "w0_a": uni((N_LAYERS_A, D), -6.0, -1.0),
        "w1_a": nrm((N_LAYERS_A, D, DECAY_LORA), D ** -0.5),
        "w2_a": nrm((N_LAYERS_A, DECAY_LORA, D), 0.5 * DECAY_LORA ** -0.5),
        "a0_a": nrm((N_LAYERS_A, D), 0.1),
        "a1_a": nrm((N_LAYERS_A, D, AAA_LORA), D ** -0.5),
        "a2_a": nrm((N_LAYERS_A, AAA_LORA, D), 0.5 * AAA_LORA ** -0.5),
        "kk_a": 0.85 + nrm((N_LAYERS_A, D), 0.05),
        "ka_a": 1.0 + nrm((N_LAYERS_A, D), 0.05),
        "rk_a": nrm((N_LAYERS_A, H, E), 0.1),
        "lnx_w_a": 1.0 + nrm((N_LAYERS_A, D), 0.1),
        "lnx_b_a": nrm((N_LAYERS_A, D), 0.01),
        "w_out_a": nrm((N_LAYERS_A, D, D), 0.5 * D ** -0.5),
        "norm_b": 1.0 + nrm((N_LAYERS_B, D), 0.1),
        "w_in_b": nrm((N_LAYERS_B, D, 4 * D + H), D ** -0.5),
        "bf_b": 3.0 + nrm((N_LAYERS_B, H), 0.5),
        "qn_b": 1.0 + nrm((N_LAYERS_B, E), 0.1),
        "kn_b": 1.0 + nrm((N_LAYERS_B, E), 0.1),
        "w_out_b": nrm((N_LAYERS_B, D, D), 0.5 * D ** -0.5),
    }


def reference(x_prompt, x_sample, state_wkv, state_shift, cache_k, cache_v, cache_logf, page_table,
              norm_a, mu_a, w_in_a, w0_a, w1_a, w2_a, a0_a, a1_a, a2_a, kk_a, ka_a, rk_a, lnx_w_a, lnx_b_a, w_out_a,
              norm_b, w_in_b, bf_b, qn_b, kn_b, w_out_b):
    B, T, D = x_prompt.shape
    DB, TS, _ = x_sample.shape
    past = page_table.shape[1] * cache_k.shape[2]
    pos_p = jnp.arange(T)
    pos_s = past + jnp.arange(TS)
    pos_k = jnp.arange(past + TS)
    xp, xs = x_prompt, x_sample
    kp_l, vp_l, fp_l, Sp_l, hp_l = [], [], [], [], []
    ks_l, vs_l, fs_l, Ss_l, hs_l = [], [], [], [], []
    for i in range(DEPTH):
        j = i // N_MIXERS
        if i % N_MIXERS == 0:
            args = (mu_a[j], w_in_a[j], w0_a[j], w1_a[j], w2_a[j], a0_a[j], a1_a[j], a2_a[j],
                    kk_a[j], ka_a[j], rk_a[j], lnx_w_a[j], lnx_b_a[j])
            hp = _rmsnorm(xp, norm_a[j])
            hs = _rmsnorm(xs, norm_a[j])
            op, Sp, lp = _rwkv7_branch(hp, jnp.zeros((B, D), hp.dtype),
                                       jnp.zeros((B, N_HEADS, HEAD_DIM, HEAD_DIM), jnp.float32), *args)
            osm, Ss, ls = _rwkv7_branch(hs, state_shift[j], state_wkv[j], *args)
            xp = xp + op @ w_out_a[j]
            xs = xs + osm @ w_out_a[j]
            Sp_l.append(Sp); hp_l.append(lp); Ss_l.append(Ss); hs_l.append(ls)
        else:
            hp = _rmsnorm(xp, norm_b[j])
            hs = _rmsnorm(xs, norm_b[j])
            qp, kp, vp, gp, lfp = _fox_project(hp, w_in_b[j], bf_b[j], qn_b[j], kn_b[j])
            cp = jnp.cumsum(lfp, axis=1)
            op = _fox_blocked(qp, kp, vp, cp, cp, pos_p, pos_p)
            xp = xp + (op.reshape(B, T, D) * jax.nn.silu(gp.astype(jnp.float32))).astype(xp.dtype) @ w_out_b[j]
            qs, ksn, vsn, gs, lfs = _fox_project(hs, w_in_b[j], bf_b[j], qn_b[j], kn_b[j])
            k_past = cache_k[j][page_table].reshape(DB, past, N_HEADS, HEAD_DIM)
            v_past = cache_v[j][page_table].reshape(DB, past, N_HEADS, HEAD_DIM)
            f_past = cache_logf[j][page_table].reshape(DB, past, N_HEADS).astype(jnp.float32)
            k_all = jnp.concatenate([k_past, ksn.astype(k_past.dtype)], axis=1)
            v_all = jnp.concatenate([v_past, vsn.astype(v_past.dtype)], axis=1)
            c_all = jnp.cumsum(jnp.concatenate([f_past, lfs], axis=1), axis=1)
            osm = _fox_blocked(qs.astype(k_all.dtype), k_all, v_all, c_all[:, past:], c_all, pos_s, pos_k)
            xs = xs + (osm.reshape(DB, TS, D) * jax.nn.silu(gs.astype(jnp.float32))).astype(xs.dtype) @ w_out_b[j]
            kp_l.append(kp); vp_l.append(vp); fp_l.append(lfp)
            ks_l.append(ksn); vs_l.append(vsn); fs_l.append(lfs)
    return (xp, xs,
            jnp.stack(kp_l), jnp.stack(vp_l), jnp.stack(fp_l), jnp.stack(Sp_l), jnp.stack(hp_l),
            jnp.stack(ks_l), jnp.stack(vs_l), jnp.stack(fs_l), jnp.stack(Ss_l), jnp.stack(hs_l))
```

```python
import functools

import jax
import jax.numpy as jnp
from jax import lax
from jax.experimental import pallas as pl
from jax.experimental.pallas import tpu as pltpu

F32 = jnp.float32
BF16 = jnp.bfloat16
HIGHEST = lax.Precision.HIGHEST

LANES = 128
RMS_EPS = 1e-6
LNX_EPS = 64e-5
NEG_BIG = -1e30
WKV_CHUNK = 64
VMEM_LIMIT = 48 * 1024 * 1024


def _nt(x, y, precision=None):
    return lax.dot_general(x, y, (((1,), (1,)), ((), ())), precision=precision,
                           preferred_element_type=F32)


def _tn(x, y, precision=None):
    return lax.dot_general(x, y, (((0,), (0,)), ((), ())), precision=precision,
                           preferred_element_type=F32)


def _nn(x, y, precision=None):
    return jnp.dot(x, y, precision=precision, preferred_element_type=F32)


def _pick_tile(n, candidates):
    for c in candidates:
        if n % c == 0:
            return c
    return n


def _mm_kernel(a_ref, b_ref, o_ref):
    o_ref[...] = _nn(a_ref[...], b_ref[...])


def _mm_res_kernel(a_ref, b_ref, r_ref, o_ref):
    o_ref[...] = r_ref[...] + _nn(a_ref[...], b_ref[...])


def _matmul(a, b, res=None):
    m, k = a.shape
    _, n = b.shape
    tm = _pick_tile(m, (1024, 512, 256, 128))
    tn = _pick_tile(n, (1024, 512, 256, 128))
    in_specs = [pl.BlockSpec((tm, k), lambda i, j: (i, 0)),
                pl.BlockSpec((k, tn), lambda i, j: (0, j))]
    args = [a, b]
    body = _mm_kernel
    if res is not None:
        in_specs.append(pl.BlockSpec((tm, tn), lambda i, j: (i, j)))
        args.append(res)
        body = _mm_res_kernel
    return pl.pallas_call(
        body,
        grid=(m // tm, n // tn),
        in_specs=in_specs,
        out_specs=pl.BlockSpec((tm, tn), lambda i, j: (i, j)),
        out_shape=jax.ShapeDtypeStruct((m, n), F32),
        compiler_params=pltpu.CompilerParams(
            dimension_semantics=("parallel", "parallel"), vmem_limit_bytes=VMEM_LIMIT),
        name="proj_matmul",
    )(*args)


def _wkv_chunk_kernel(r_ref, lw_ref, k_ref, v_ref, a_ref, b_ref, o_ref, zt_ref):
    c = pl.program_id(2)

    @pl.when(c == 0)
    def _():
        zt_ref[...] = jnp.zeros_like(zt_ref)

    L = r_ref.shape[1]
    n = 2 * L
    half = LANES // 2
    r, lw, k, v, a, b = (ref[0] for ref in (r_ref, lw_ref, k_ref, v_ref, a_ref, b_ref))

    ti = lax.broadcasted_iota(jnp.int32, (L, L), 0)
    tj = lax.broadcasted_iota(jnp.int32, (L, L), 1)
    cum = _nn((ti >= tj).astype(F32), lw, HIGHEST)
    p_inc = jnp.exp(cum)
    p_exc = jnp.exp(cum - lw)
    p_inv = jnp.exp(-cum)
    p_last = p_inc[L - 1:L, :]

    lane = lax.broadcasted_iota(jnp.int32, (L, LANES), 1)
    head0 = lane < half

    def stack(x):
        return jnp.concatenate([jnp.where(head0, x, 0.0), jnp.where(head0, 0.0, x)], axis=0)

    rt = stack(r * p_inc).astype(BF16)
    at = stack(a * p_exc).astype(BF16)
    bt = stack(b * p_inv)
    kt = stack(k * p_inv)
    vs = stack(v).astype(BF16)
    bh = (bt * p_last).astype(BF16)
    kh = (kt * p_last).astype(BF16)
    bt = bt.astype(BF16)
    kt = kt.astype(BF16)

    i = lax.broadcasted_iota(jnp.int32, (n, n), 0)
    j = lax.broadcasted_iota(jnp.int32, (n, n), 1)
    strict = i > j
    incl = i >= j
    a_ab = jnp.where(strict, _nt(at, bt), 0.0)
    a_ak = jnp.where(strict, _nt(at, kt), 0.0).astype(BF16)
    a_rb = jnp.where(incl, _nt(rt, bt), 0.0).astype(BF16)
    a_rk = jnp.where(incl, _nt(rt, kt), 0.0).astype(BF16)

    t_inv = jnp.where(i == j, 1.0, 0.0) + jnp.where((i >> 1) == (j >> 1), a_ab, 0.0)
    lvl = 1
    while (2 << lvl) <= L:
        off = ((i >> (lvl + 1)) == (j >> (lvl + 1))) & ((i >> lvl) != (j >> lvl))
        a_off = jnp.where(off, a_ab, 0.0)
        t_inv = t_inv + _nn(_nn(t_inv, a_off, HIGHEST), t_inv, HIGHEST)
        lvl += 1

    zt = zt_ref[0, 0]
    zt_b = zt.astype(BF16)
    rhs = _nt(at, zt_b) + _nn(a_ak, vs)
    u = _nn(t_inv, rhs, HIGHEST)
    u_b = u.astype(BF16)
    o_st = _nt(rt, zt_b) + _nn(a_rb, u_b) + _nn(a_rk, vs)
    o_ref[0] = o_st[:L] + o_st[L:]
    zt_ref[0, 0] = zt * p_last + _tn(u_b, bh) + _tn(vs, kh)


def _wkv_chunked(r, lw, k, v, a, b):
    bsz, t, d = r.shape
    npair = d // LANES
    L = WKV_CHUNK
    spec = pl.BlockSpec((1, L, LANES), lambda bi, p, c: (bi, c, p))
    return pl.pallas_call(
        _wkv_chunk_kernel,
        grid=(bsz, npair, t // L),
        in_specs=[spec] * 6,
        out_specs=[spec, pl.BlockSpec((1, 1, LANES, LANES), lambda bi, p, c: (bi, p, 0, 0))],
        out_shape=[jax.ShapeDtypeStruct((bsz, t, d), F32),
                   jax.ShapeDtypeStruct((bsz, npair, LANES, LANES), F32)],
        compiler_params=pltpu.CompilerParams(
            dimension_semantics=("parallel", "parallel", "arbitrary"), vmem_limit_bytes=VMEM_LIMIT),
        name="wkv_chunked",
    )(r, lw, k, v, a, b)


def _wkv_step_kernel(s_ref, r_ref, w_ref, k_ref, a_ref, b_ref, v_ref, so_ref, o_ref):
    s = s_ref[...]
    sa = jnp.sum(s * a_ref[...], axis=-1, keepdims=True)
    s_new = s * w_ref[...] + sa * b_ref[...] + v_ref[...] * k_ref[...]
    so_ref[...] = s_new
    o_ref[...] = jnp.sum(s_new * r_ref[...], axis=-1, keepdims=True)


def _wkv_step(s0, r, w, k, v, a, b):
    n, h, e, _ = s0.shape
    bb = _pick_tile(n, (8, 4, 2, 1))
    row = lambda x: x.reshape(n, h, 1, e)
    s_spec = pl.BlockSpec((bb, h, e, e), lambda i: (i, 0, 0, 0))
    row_spec = pl.BlockSpec((bb, h, 1, e), lambda i: (i, 0, 0, 0))
    col_spec = pl.BlockSpec((bb, h, e, 1), lambda i: (i, 0, 0, 0))
    s_new, o = pl.pallas_call(
        _wkv_step_kernel,
        grid=(n // bb,),
        in_specs=[s_spec] + [row_spec] * 5 + [col_spec],
        out_specs=[s_spec, col_spec],
        out_shape=[jax.ShapeDtypeStruct((n, h, e, e), F32), jax.ShapeDtypeStruct((n, h, e, 1), F32)],
        compiler_params=pltpu.CompilerParams(
            dimension_semantics=("parallel",), vmem_limit_bytes=VMEM_LIMIT),
        name="wkv_step",
    )(s0, row(r), row(w), row(k), row(a), row(b), v.reshape(n, h, e, 1))
    return o.reshape(n, h, e), s_new


def _fox_flash_kernel(q_ref, k_ref, v_ref, ci_ref, cj_ref, o_ref, m_ref, l_ref, acc_ref, *, scale):
    p = pl.program_id(1)
    qi = pl.program_id(2)
    kj = pl.program_id(3)
    tq = q_ref.shape[1]
    tk = k_ref.shape[1]
    half = LANES // 2

    @pl.when(kj == 0)
    def _():
        m_ref[...] = jnp.full_like(m_ref, NEG_BIG)
        l_ref[...] = jnp.zeros_like(l_ref)
        acc_ref[...] = jnp.zeros_like(acc_ref)

    @pl.when(kj <= qi)
    def _():
        q = q_ref[0] * scale
        kb = k_ref[0].astype(BF16)
        vb = v_ref[0].astype(BF16)
        ci = ci_ref[0, 0]
        lane = lax.broadcasted_iota(jnp.int32, (tq, LANES), 1)
        rowpos = qi * tq + lax.broadcasted_iota(jnp.int32, (tq, tk), 0)
        colpos = kj * tk + lax.broadcasted_iota(jnp.int32, (tq, tk), 1)
        causal = colpos <= rowpos
        for h in range(2):
            mine = (lane < half) if h == 0 else (lane >= half)
            qm = jnp.where(mine, q, 0.0).astype(BF16)
            s = _nt(qm, kb)
            cj = cj_ref[0, pl.ds(2 * p + h, 1), :]
            s = s + (ci[:, h * half:h * half + 1] - cj)
            s = jnp.where(causal, s, NEG_BIG)
            m_prev = m_ref[h]
            m_new = jnp.maximum(m_prev, jnp.max(s, axis=1, keepdims=True))
            alpha = jnp.exp(m_prev - m_new)
            pr = jnp.exp(s - m_new)
            l_ref[h] = alpha * l_ref[h] + jnp.sum(pr, axis=1, keepdims=True)
            acc_ref[h] = alpha * acc_ref[h] + _nn(pr.astype(BF16), vb)
            m_ref[h] = m_new

    @pl.when(kj == qi)
    def _():
        lane = lax.broadcasted_iota(jnp.int32, (tq, LANES), 1)
        o_ref[0] = jnp.where(lane < half, acc_ref[0] / l_ref[0], acc_ref[1] / l_ref[1])


def _fox_flash(q, k, v, c, scale, tile=512):
    bsz, t, d = q.shape
    h = c.shape[-1]
    npair = d // LANES
    e = d // h
    tq = tk = _pick_tile(t, (tile, 256, 128))
    c_splat = jnp.repeat(c, e, axis=-1).reshape(bsz, t, npair, LANES).transpose(0, 2, 1, 3)
    c_t = jnp.swapaxes(c, 1, 2)
    kv_spec = pl.BlockSpec((1, tk, LANES), lambda bi, p, qi, kj: (bi, jnp.minimum(kj, qi), p))
    q_spec = pl.BlockSpec((1, tq, LANES), lambda bi, p, qi, kj: (bi, qi, p))
    return pl.pallas_call(
        functools.partial(_fox_flash_kernel, scale=scale),
        grid=(bsz, npair, t // tq, t // tk),
        in_specs=[q_spec, kv_spec, kv_spec,
                  pl.BlockSpec((1, 1, tq, LANES), lambda bi, p, qi, kj: (bi, p, qi, 0)),
                  pl.BlockSpec((1, h, tk), lambda bi, p, qi, kj: (bi, 0, jnp.minimum(kj, qi)))],
        out_specs=q_spec,
        out_shape=jax.ShapeDtypeStruct((bsz, t, d), F32),
        scratch_shapes=[pltpu.VMEM((2, tq, 1), F32), pltpu.VMEM((2, tq, 1), F32),
                        pltpu.VMEM((2, tq, LANES), F32)],
        compiler_params=pltpu.CompilerParams(
            dimension_semantics=("parallel", "parallel", "parallel", "arbitrary"),
            vmem_limit_bytes=VMEM_LIMIT),
        name="fox_flash",
    )(q, k, v, c_splat, c_t)


def _fox_decode_kernel(pt_ref, q_ref, k_ref, v_ref, bias_ref, kn_ref, vn_ref, o_ref,
                       m_ref, l_ref, acc_ref):
    pg = pl.program_id(1)
    nh = q_ref.shape[1]
    d = q_ref.shape[2]
    e = d // nh

    @pl.when(pg == 0)
    def _():
        m_ref[...] = jnp.full_like(m_ref, NEG_BIG)
        l_ref[...] = jnp.zeros_like(l_ref)
        acc_ref[...] = jnp.zeros_like(acc_ref)

    qbd = q_ref[0]
    s = _nt(qbd.astype(BF16), k_ref[0, 0].astype(BF16)) + bias_ref[0, 0]
    m_prev = m_ref[...]
    m_new = jnp.maximum(m_prev, jnp.max(s, axis=1, keepdims=True))
    alpha = jnp.exp(m_prev - m_new)
    pr = jnp.exp(s - m_new)
    l_ref[...] = alpha * l_ref[...] + jnp.sum(pr, axis=1, keepdims=True)
    acc_ref[...] = alpha * acc_ref[...] + _nn(pr.astype(BF16), v_ref[0, 0].astype(BF16))
    m_ref[...] = m_new

    @pl.when(pg == pl.num_programs(1) - 1)
    def _():
        s_new = jnp.sum(qbd * kn_ref[0], axis=1, keepdims=True)
        m_old = m_ref[...]
        m_fin = jnp.maximum(m_old, s_new)
        a_old = jnp.exp(m_old - m_fin)
        p_new = jnp.exp(s_new - m_fin)
        l_fin = l_ref[...] * a_old + p_new
        full = (acc_ref[...] * a_old + p_new * vn_ref[0]) / l_fin
        row = lax.broadcasted_iota(jnp.int32, (nh, d), 0)
        lane = lax.broadcasted_iota(jnp.int32, (nh, d), 1)
        own = (lane >= row * e) & (lane < (row + 1) * e)
        o_ref[0] = jnp.sum(jnp.where(own, full, 0.0), axis=0, keepdims=True)


def _fox_decode(layer, q, k_new, v_new, cache_k, cache_v, bias_t, page_table, scale):
    n, d = q.shape
    n_pages = page_table.shape[1]
    page = cache_k.shape[2]
    nh = bias_t.shape[2]
    e = d // nh
    lane_head = jnp.arange(d, dtype=jnp.int32) // e
    qbd = jnp.where(lane_head[None, None, :] == jnp.arange(nh, dtype=jnp.int32)[None, :, None],
                    (q * scale)[:, None, :], 0.0)
    kv_spec = pl.BlockSpec((1, 1, page, d), lambda b, pg, pt: (layer, pt[b, pg], 0, 0))
    row_spec = pl.BlockSpec((1, 1, d), lambda b, pg, pt: (b, 0, 0))
    grid_spec = pltpu.PrefetchScalarGridSpec(
        num_scalar_prefetch=1,
        grid=(n, n_pages),
        in_specs=[pl.BlockSpec((1, nh, d), lambda b, pg, pt: (b, 0, 0)),
                  kv_spec, kv_spec,
                  pl.BlockSpec((1, 1, nh, page), lambda b, pg, pt: (b, pg, 0, 0)),
                  row_spec, row_spec],
        out_specs=row_spec,
        scratch_shapes=[pltpu.VMEM((nh, 1), F32), pltpu.VMEM((nh, 1), F32), pltpu.VMEM((nh, d), F32)],
    )
    o = pl.pallas_call(
        _fox_decode_kernel,
        grid_spec=grid_spec,
        out_shape=jax.ShapeDtypeStruct((n, 1, d), F32),
        compiler_params=pltpu.CompilerParams(
            dimension_semantics=("parallel", "arbitrary"), vmem_limit_bytes=VMEM_LIMIT),
        name="fox_decode",
    )(page_table, qbd, cache_k, cache_v, bias_t, k_new.reshape(n, 1, d), v_new.reshape(n, 1, d))
    return o.reshape(n, d)


def _rmsnorm(x, g):
    xf = x.astype(F32)
    y = xf * lax.rsqrt(jnp.mean(xf * xf, axis=-1, keepdims=True) + RMS_EPS)
    return (y * g.astype(F32)).astype(x.dtype)


def _rwkv_layer(x, h_prev0, s0, prm, nh):
    (norm, mu, w_in, w0, w1, w2, a0, a1, a2, k_k, k_a, r_k, lnx_w, lnx_b, w_out) = prm
    bsz, t, d = x.shape
    e = d // nh
    m = bsz * t
    h = _rmsnorm(x, norm)
    h_prev = jnp.concatenate([h_prev0[:, None, :].astype(h.dtype), h[:, :-1, :]], axis=1)
    xx = h_prev - h
    xs = [(h + xx * mu[s]).reshape(m, d).astype(BF16) for s in range(mu.shape[0])]
    r, k, v, gate = (_matmul(xs[s], w_in[s].astype(BF16)) for s in range(4))
    w_lora = _matmul(jnp.tanh(_matmul(xs[4], w1.astype(BF16))).astype(BF16), w2.astype(BF16))
    w_log = -jax.nn.softplus(-(w0 + w_lora)) - 0.5
    lw = -jnp.exp(w_log)
    a_lora = _matmul(_matmul(xs[5], a1.astype(BF16)).astype(BF16), a2.astype(BF16))
    a = jax.nn.sigmoid(a0 + a_lora)
    hd = lambda z: z.reshape(m, nh, e)
    kk = hd(k * k_k)
    kk = kk / jnp.maximum(jnp.sqrt(jnp.sum(kk * kk, axis=-1, keepdims=True)), 1e-12)
    kf = k * (1.0 + (a - 1.0) * k_a)
    neg_kk = (-kk).reshape(m, d)
    kk_a = (kk * hd(a)).reshape(m, d)
    if s0 is None:
        seq = lambda z: z.reshape(bsz, t, d)
        o, zt = _wkv_chunked(seq(r), seq(lw), seq(kf), seq(v), seq(neg_kk), seq(kk_a))
        zt = zt.reshape(bsz, d // LANES, 2, e, 2, e)
        s_fin = jnp.stack([zt[:, :, 0, :, 0, :], zt[:, :, 1, :, 1, :]], axis=2).reshape(bsz, nh, e, e)
        o = o.reshape(m, nh, e)
    else:
        o, s_fin = _wkv_step(s0, hd(r), hd(jnp.exp(lw)), hd(kf), hd(v), hd(neg_kk), hd(kk_a))
    mean = jnp.mean(o, axis=-1, keepdims=True)
    var = jnp.mean(jnp.square(o - mean), axis=-1, keepdims=True)
    o = (o - mean) * lax.rsqrt(var + LNX_EPS)
    o = o * lnx_w.reshape(nh, e) + lnx_b.reshape(nh, e)
    o = o + jnp.sum(hd(r) * hd(kf) * r_k, axis=-1, keepdims=True) * hd(v)
    o = o.reshape(m, d) * jax.nn.silu(gate)
    x_new = _matmul(o.astype(BF16), w_out.astype(BF16), res=x.reshape(m, d)).reshape(bsz, t, d)
    return x_new, s_fin, h[:, -1, :]


def _fox_project(x, norm, w_in, b_f, qn_g, kn_g, nh):
    bsz, t, d = x.shape
    e = d // nh
    m = bsz * t
    hb = _rmsnorm(x, norm).reshape(m, d).astype(BF16)
    proj = _matmul(hb, w_in[:, :4 * d].astype(BF16))
    f_logit = _matmul(hb, w_in[:, 4 * d:].astype(BF16))
    q, k, v, gate = (proj[:, s * d:(s + 1) * d] for s in range(4))
    q = _rmsnorm(q.reshape(m, nh, e), qn_g).reshape(bsz, t, d)
    k = _rmsnorm(k.reshape(m, nh, e), kn_g).reshape(bsz, t, d)
    logf = jax.nn.log_sigmoid(f_logit + b_f).reshape(bsz, t, nh)
    return q, k, v.reshape(bsz, t, d), gate, logf


def _fox_finish(x, o, gate, w_out):
    bsz, t, d = x.shape
    m = bsz * t
    y = (o.reshape(m, d) * jax.nn.silu(gate)).astype(BF16)
    return _matmul(y, w_out.astype(BF16), res=x.reshape(m, d)).reshape(bsz, t, d)


def kernel(x_prompt, x_sample, state_wkv, state_shift, cache_k, cache_v, cache_logf, page_table,
           norm_a, mu_a, w_in_a, w0_a, w1_a, w2_a, a0_a, a1_a, a2_a, kk_a, ka_a, rk_a, lnx_w_a, lnx_b_a, w_out_a,
           norm_b, w_in_b, bf_b, qn_b, kn_b, w_out_b):
    bsz, t, d = x_prompt.shape
    nb, ts, _ = x_sample.shape
    assert ts == 1, "the sample group carries one new token per sequence"
    nh, e = rk_a.shape[1], rk_a.shape[2]
    assert d == nh * e and 2 * e == LANES and t % WKV_CHUNK == 0
    n_layers_a = norm_a.shape[0]
    n_layers_b = norm_b.shape[0]
    depth = n_layers_a + n_layers_b
    n_pool, page = cache_k.shape[1], cache_k.shape[2]
    n_pages = page_table.shape[1]
    past = n_pages * page
    scale = e ** -0.5
    ck4 = cache_k.reshape(n_layers_b, n_pool, page, d)
    cv4 = cache_v.reshape(n_layers_b, n_pool, page, d)

    xp, xs = x_prompt, x_sample
    kp_l, vp_l, fp_l, sp_l, hp_l = [], [], [], [], []
    ks_l, vs_l, fs_l, ss_l, hs_l = [], [], [], [], []
    for i in range(depth):
        j = i // 2
        if i % 2 == 0:
            prm = (norm_a[j], mu_a[j], w_in_a[j], w0_a[j], w1_a[j], w2_a[j], a0_a[j], a1_a[j], a2_a[j],
                   kk_a[j], ka_a[j], rk_a[j], lnx_w_a[j], lnx_b_a[j], w_out_a[j])
            xp, s_p, l_p = _rwkv_layer(xp, jnp.zeros((bsz, d), xp.dtype), None, prm, nh)
            xs, s_s, l_s = _rwkv_layer(xs, state_shift[j], state_wkv[j], prm, nh)
            sp_l.append(s_p); hp_l.append(l_p); ss_l.append(s_s); hs_l.append(l_s)
        else:
            qp, kp, vp, gp, lfp = _fox_project(xp, norm_b[j], w_in_b[j], bf_b[j], qn_b[j], kn_b[j], nh)
            cp = jnp.cumsum(lfp, axis=1)
            op = _fox_flash(qp, kp, vp, cp, scale)
            xp = _fox_finish(xp, op, gp, w_out_b[j])

            qs, ksn, vsn, gs, lfs = _fox_project(xs, norm_b[j], w_in_b[j], bf_b[j], qn_b[j], kn_b[j], nh)
            f_past = cache_logf[j][page_table].reshape(nb, past, nh).astype(F32)
            c_all = jnp.cumsum(jnp.concatenate([f_past, lfs], axis=1), axis=1)
            bias = c_all[:, past:, :] - c_all[:, :past, :]
            bias_t = bias.reshape(nb, n_pages, page, nh).transpose(0, 1, 3, 2)
            osm = _fox_decode(j, qs.reshape(nb, d), ksn.reshape(nb, d), vsn.reshape(nb, d),
                              ck4, cv4, bias_t, page_table, scale)
            xs = _fox_finish(xs, osm.reshape(nb, 1, d), gs, w_out_b[j])
            hd4 = lambda z, n_, t_: z.reshape(n_, t_, nh, e)
            kp_l.append(hd4(kp, bsz, t)); vp_l.append(hd4(vp, bsz, t)); fp_l.append(lfp)
            ks_l.append(hd4(ksn, nb, ts)); vs_l.append(hd4(vsn, nb, ts)); fs_l.append(lfs)
    return (xp, xs,
            jnp.stack(kp_l), jnp.stack(vp_l), jnp.stack(fp_l), jnp.stack(sp_l), jnp.stack(hp_l),
            jnp.stack(ks_l), jnp.stack(vs_l), jnp.stack(fs_l), jnp.stack(ss_l), jnp.stack(hs_l))
```

```python
import functools

import jax
import jax.numpy as jnp
from jax import lax
from jax.experimental import pallas as pl
from jax.experimental.pallas import tpu as pltpu

F32 = jnp.float32
BF16 = jnp.bfloat16
HIGHEST = lax.Precision.HIGHEST

LANES = 128
RMS_EPS = 1e-6
LNX_EPS = 64e-5
NEG_BIG = -1e30
WKV_CHUNK = 64
VMEM_LIMIT = 48 * 1024 * 1024


def _nt(x, y, precision=None):
    return lax.dot_general(x, y, (((1,), (1,)), ((), ())), precision=precision,
                           preferred_element_type=F32)


def _tn(x, y, precision=None):
    return lax.dot_general(x, y, (((0,), (0,)), ((), ())), precision=precision,
                           preferred_element_type=F32)


def _nn(x, y, precision=None):
    return jnp.dot(x, y, precision=precision, preferred_element_type=F32)


def _pick_tile(n, candidates):
    for c in candidates:
        if n % c == 0:
            return c
    return n


def _mm_kernel(a_ref, b_ref, o_ref):
    o_ref[...] = _nn(a_ref[...], b_ref[...])


def _mm_res_kernel(a_ref, b_ref, r_ref, o_ref):
    o_ref[...] = r_ref[...] + _nn(a_ref[...], b_ref[...])


def _matmul(a, b, res=None):
    m, k = a.shape
    _, n = b.shape
    tm = _pick_tile(m, (1024, 512, 256, 128))
    tn = _pick_tile(n, (1024, 512, 256, 128))
    in_specs = [pl.BlockSpec((tm, k), lambda i, j: (i, 0)),
                pl.BlockSpec((k, tn), lambda i, j: (0, j))]
    args = [a, b]
    body = _mm_kernel
    if res is not None:
        in_specs.append(pl.BlockSpec((tm, tn), lambda i, j: (i, j)))
        args.append(res)
        body = _mm_res_kernel
    return pl.pallas_call(
        body,
        grid=(m // tm, n // tn),
        in_specs=in_specs,
        out_specs=pl.BlockSpec((tm, tn), lambda i, j: (i, j)),
        out_shape=jax.ShapeDtypeStruct((m, n), F32),
        compiler_params=pltpu.CompilerParams(
            dimension_semantics=("parallel", "parallel"), vmem_limit_bytes=VMEM_LIMIT),
        name="proj_matmul",
    )(*args)


def _wkv_chunk_kernel(r_ref, lw_ref, k_ref, v_ref, a_ref, b_ref, o_ref, zt_ref):
    c = pl.program_id(1)

    @pl.when(c == 0)
    def _():
        zt_ref[...] = jnp.zeros_like(zt_ref)

    L = r_ref.shape[1]
    n = 2 * L
    half = LANES // 2
    npair = r_ref.shape[2] // LANES

    ti = lax.broadcasted_iota(jnp.int32, (L, L), 0)
    tj = lax.broadcasted_iota(jnp.int32, (L, L), 1)
    tri = (ti >= tj).astype(BF16)
    head0 = lax.broadcasted_iota(jnp.int32, (L, LANES), 1) < half
    i = lax.broadcasted_iota(jnp.int32, (n, n), 0)
    j = lax.broadcasted_iota(jnp.int32, (n, n), 1)
    strict = i > j
    incl = i >= j
    eye = jnp.where(i == j, 1.0, 0.0)
    first = (i >> 1) == (j >> 1)
    levels = []
    lvl = 1
    while (2 << lvl) <= L:
        levels.append(((i >> (lvl + 1)) == (j >> (lvl + 1))) & ((i >> lvl) != (j >> lvl)))
        lvl += 1

    def stack(x):
        return jnp.concatenate([jnp.where(head0, x, 0.0), jnp.where(head0, 0.0, x)], axis=0)

    pairs = range(npair)
    sls = [slice(p * LANES, (p + 1) * LANES) for p in pairs]
    ar, bk, bk_h, vs, p_last = [], [], [], [], []
    for p in pairs:
        r, lw, k, v, a, b = (ref[0, :, sls[p]] for ref in (r_ref, lw_ref, k_ref, v_ref, a_ref, b_ref))
        lw1 = lw.astype(BF16)
        rem = lw - lw1.astype(F32)
        lw2 = rem.astype(BF16)
        lw3 = (rem - lw2.astype(F32)).astype(BF16)
        cum = _nn(tri, lw1) + (_nn(tri, lw2) + _nn(tri, lw3))
        p_inc = jnp.exp(cum)
        p_exc = jnp.exp(cum - lw)
        p_inv = jnp.exp(-cum)
        pl_ = p_inc[L - 1:L, :]
        bk_f = jnp.concatenate([stack(b * p_inv), stack(k * p_inv)], axis=0)
        ar.append(jnp.concatenate([stack(a * p_exc), stack(r * p_inc)], axis=0).astype(BF16))
        bk.append(bk_f.astype(BF16))
        bk_h.append((bk_f * pl_).astype(BF16))
        vs.append(stack(v).astype(BF16))
        p_last.append(pl_)

    g = [_nt(ar[p], bk[p]) for p in pairs]
    a_ab = [jnp.where(strict, g[p][:n, :n], 0.0) for p in pairs]
    ak_rk = [jnp.concatenate([jnp.where(strict, g[p][:n, n:], 0.0),
                              jnp.where(incl, g[p][n:, n:], 0.0)], axis=0).astype(BF16) for p in pairs]
    a_rb = [jnp.where(incl, g[p][n:, :n], 0.0).astype(BF16) for p in pairs]

    t_inv = [eye + jnp.where(first, a_ab[p], 0.0) for p in pairs]
    for off in levels:
        t_b = [t_inv[p].astype(BF16) for p in pairs]
        x = [_nn(t_b[p], jnp.where(off, a_ab[p], 0.0).astype(BF16)).astype(BF16) for p in pairs]
        t_inv = [t_inv[p] + _nn(x[p], t_b[p]) for p in pairs]
    t_b = [t_inv[p].astype(BF16) for p in pairs]

    zt = [zt_ref[0, p] for p in pairs]
    y = [_nt(ar[p], zt[p].astype(BF16)) + _nn(ak_rk[p], vs[p]) for p in pairs]
    u_b = [_nn(t_b[p], y[p][:n].astype(BF16)).astype(BF16) for p in pairs]
    for p in pairs:
        o_st = y[p][n:] + _nn(a_rb[p], u_b[p])
        o_ref[0, :, sls[p]] = o_st[:L] + o_st[L:]
        zt_ref[0, p] = zt[p] * p_last[p] + _tn(jnp.concatenate([u_b[p], vs[p]], axis=0), bk_h[p])


def _wkv_chunked(r, lw, k, v, a, b):
    bsz, t, d = r.shape
    npair = d // LANES
    L = WKV_CHUNK
    spec = pl.BlockSpec((1, L, d), lambda bi, c: (bi, c, 0))
    return pl.pallas_call(
        _wkv_chunk_kernel,
        grid=(bsz, t // L),
        in_specs=[spec] * 6,
        out_specs=[spec, pl.BlockSpec((1, npair, LANES, LANES), lambda bi, c: (bi, 0, 0, 0))],
        out_shape=[jax.ShapeDtypeStruct((bsz, t, d), F32),
                   jax.ShapeDtypeStruct((bsz, npair, LANES, LANES), F32)],
        compiler_params=pltpu.CompilerParams(
            dimension_semantics=("parallel", "arbitrary"), vmem_limit_bytes=VMEM_LIMIT),
        name="wkv_chunked",
    )(r, lw, k, v, a, b)


def _wkv_step_kernel(s_ref, r_ref, w_ref, k_ref, a_ref, b_ref, v_ref, so_ref, o_ref):
    s = s_ref[...]
    sa = jnp.sum(s * a_ref[...], axis=-1, keepdims=True)
    s_new = s * w_ref[...] + sa * b_ref[...] + v_ref[...] * k_ref[...]
    so_ref[...] = s_new
    o_ref[...] = jnp.sum(s_new * r_ref[...], axis=-1, keepdims=True)


def _wkv_step(s0, r, w, k, v, a, b):
    n, h, e, _ = s0.shape
    bb = _pick_tile(n, (8, 4, 2, 1))
    row = lambda x: x.reshape(n, h, 1, e)
    s_spec = pl.BlockSpec((bb, h, e, e), lambda i: (i, 0, 0, 0))
    row_spec = pl.BlockSpec((bb, h, 1, e), lambda i: (i, 0, 0, 0))
    col_spec = pl.BlockSpec((bb, h, e, 1), lambda i: (i, 0, 0, 0))
    s_new, o = pl.pallas_call(
        _wkv_step_kernel,
        grid=(n // bb,),
        in_specs=[s_spec] + [row_spec] * 5 + [col_spec],
        out_specs=[s_spec, col_spec],
        out_shape=[jax.ShapeDtypeStruct((n, h, e, e), F32), jax.ShapeDtypeStruct((n, h, e, 1), F32)],
        compiler_params=pltpu.CompilerParams(
            dimension_semantics=("parallel",), vmem_limit_bytes=VMEM_LIMIT),
        name="wkv_step",
    )(s0, row(r), row(w), row(k), row(a), row(b), v.reshape(n, h, e, 1))
    return o.reshape(n, h, e), s_new


def _fox_flash_kernel(q_ref, k_ref, v_ref, ci_ref, cj_ref, o_ref, m_ref, l_ref, acc_ref, *, scale):
    p = pl.program_id(1)
    qi = pl.program_id(2)
    kj = pl.program_id(3)
    tq = q_ref.shape[1]
    tk = k_ref.shape[1]
    half = LANES // 2

    @pl.when(kj == 0)
    def _():
        m_ref[...] = jnp.full_like(m_ref, NEG_BIG)
        l_ref[...] = jnp.zeros_like(l_ref)
        acc_ref[...] = jnp.zeros_like(acc_ref)

    @pl.when(kj <= qi)
    def _():
        q = q_ref[0] * scale
        kb = k_ref[0].astype(BF16)
        vb = v_ref[0].astype(BF16)
        ci = ci_ref[0, 0]
        lane = lax.broadcasted_iota(jnp.int32, (tq, LANES), 1)
        rowpos = qi * tq + lax.broadcasted_iota(jnp.int32, (tq, tk), 0)
        colpos = kj * tk + lax.broadcasted_iota(jnp.int32, (tq, tk), 1)
        causal = colpos <= rowpos
        for h in range(2):
            mine = (lane < half) if h == 0 else (lane >= half)
            qm = jnp.where(mine, q, 0.0).astype(BF16)
            s = _nt(qm, kb)
            cj = cj_ref[0, pl.ds(2 * p + h, 1), :]
            s = s + (ci[:, h * half:h * half + 1] - cj)
            s = jnp.where(causal, s, NEG_BIG)
            m_prev = m_ref[h]
            m_new = jnp.maximum(m_prev, jnp.max(s, axis=1, keepdims=True))
            alpha = jnp.exp(m_prev - m_new)
            pr = jnp.exp(s - m_new)
            l_ref[h] = alpha * l_ref[h] + jnp.sum(pr, axis=1, keepdims=True)
            acc_ref[h] = alpha * acc_ref[h] + _nn(pr.astype(BF16), vb)
            m_ref[h] = m_new

    @pl.when(kj == qi)
    def _():
        lane = lax.broadcasted_iota(jnp.int32, (tq, LANES), 1)
        o_ref[0] = jnp.where(lane < half, acc_ref[0] / l_ref[0], acc_ref[1] / l_ref[1])


def _fox_flash(q, k, v, c, scale, tile=512):
    bsz, t, d = q.shape
    h = c.shape[-1]
    npair = d // LANES
    e = d // h
    tq = tk = _pick_tile(t, (tile, 256, 128))
    c_splat = jnp.repeat(c, e, axis=-1).reshape(bsz, t, npair, LANES).transpose(0, 2, 1, 3)
    c_t = jnp.swapaxes(c, 1, 2)
    kv_spec = pl.BlockSpec((1, tk, LANES), lambda bi, p, qi, kj: (bi, jnp.minimum(kj, qi), p))
    q_spec = pl.BlockSpec((1, tq, LANES), lambda bi, p, qi, kj: (bi, qi, p))
    return pl.pallas_call(
        functools.partial(_fox_flash_kernel, scale=scale),
        grid=(bsz, npair, t // tq, t // tk),
        in_specs=[q_spec, kv_spec, kv_spec,
                  pl.BlockSpec((1, 1, tq, LANES), lambda bi, p, qi, kj: (bi, p, qi, 0)),
                  pl.BlockSpec((1, h, tk), lambda bi, p, qi, kj: (bi, 0, jnp.minimum(kj, qi)))],
        out_specs=q_spec,
        out_shape=jax.ShapeDtypeStruct((bsz, t, d), F32),
        scratch_shapes=[pltpu.VMEM((2, tq, 1), F32), pltpu.VMEM((2, tq, 1), F32),
                        pltpu.VMEM((2, tq, LANES), F32)],
        compiler_params=pltpu.CompilerParams(
            dimension_semantics=("parallel", "parallel", "parallel", "arbitrary"),
            vmem_limit_bytes=VMEM_LIMIT),
        name="fox_flash",
    )(q, k, v, c_splat, c_t)


def _fox_decode_kernel(pt_ref, q_ref, *refs, npg):
    k_refs = refs[:npg]
    v_refs = refs[npg:2 * npg]
    bias_ref, kn_ref, vn_ref, o_ref, m_ref, l_ref, acc_ref = refs[2 * npg:]
    pg = pl.program_id(1)

    @pl.when(pg == 0)
    def _():
        m_ref[...] = jnp.full_like(m_ref, NEG_BIG)
        l_ref[...] = jnp.zeros_like(l_ref)
        acc_ref[...] = jnp.zeros_like(acc_ref)

    q = q_ref[0]
    s = [jnp.sum(k_refs[t][0, 0] * q, axis=1, keepdims=True) + bias_ref[0, t] for t in range(npg)]
    m_prev = m_ref[...]
    m_new = m_prev
    for t in range(npg):
        m_new = jnp.maximum(m_new, jnp.max(s[t], axis=2, keepdims=True))
    alpha = jnp.exp(m_prev - m_new)
    l_new = alpha * l_ref[...]
    acc = alpha * acc_ref[...]
    for t in range(npg):
        pr = jnp.exp(s[t] - m_new)
        l_new = l_new + jnp.sum(pr, axis=2, keepdims=True)
        acc = acc + pr * v_refs[t][0, 0]
    l_ref[...] = l_new
    acc_ref[...] = acc
    m_ref[...] = m_new

    @pl.when(pg == pl.num_programs(1) - 1)
    def _():
        s_new = jnp.sum(q * kn_ref[0], axis=1, keepdims=True)
        m_fin = jnp.maximum(m_new, s_new)
        a_old = jnp.exp(m_new - m_fin)
        p_new = jnp.exp(s_new - m_fin)
        l_fin = l_new * a_old + p_new
        o_ref[0] = (jnp.sum(acc, axis=2, keepdims=True) * a_old + p_new * vn_ref[0]) / l_fin


def _fox_decode(layer, q, k_new, v_new, cache_kt, cache_vt, bias, page_table):
    n, nh, e = q.shape
    n_pages = page_table.shape[1]
    page = cache_kt.shape[4]
    npg = _pick_tile(n_pages, (4, 2, 1))

    def kv_spec(t):
        return pl.BlockSpec((1, 1, nh, e, page), lambda b, pg, pt: (layer, pt[b, pg * npg + t], 0, 0, 0))

    col = lambda x: x.reshape(n, nh, e, 1)
    col_spec = pl.BlockSpec((1, nh, e, 1), lambda b, pg, pt: (b, 0, 0, 0))
    grid_spec = pltpu.PrefetchScalarGridSpec(
        num_scalar_prefetch=1,
        grid=(n, n_pages // npg),
        in_specs=[col_spec] + [kv_spec(t) for t in range(npg)] * 2
                 + [pl.BlockSpec((1, npg, nh, 1, page), lambda b, pg, pt: (b, pg, 0, 0, 0)),
                    col_spec, col_spec],
        out_specs=col_spec,
        scratch_shapes=[pltpu.VMEM((nh, 1, 1), F32), pltpu.VMEM((nh, 1, 1), F32),
                        pltpu.VMEM((nh, e, page), F32)],
    )
    o = pl.pallas_call(
        functools.partial(_fox_decode_kernel, npg=npg),
        grid_spec=grid_spec,
        out_shape=jax.ShapeDtypeStruct((n, nh, e, 1), F32),
        compiler_params=pltpu.CompilerParams(
            dimension_semantics=("parallel", "arbitrary"), vmem_limit_bytes=VMEM_LIMIT),
        name="fox_decode",
    )(page_table, col(q), *([cache_kt] * npg), *([cache_vt] * npg),
      bias.reshape(n, n_pages, nh, 1, page), col(k_new), col(v_new))
    return o.reshape(n, nh, e)


def _rmsnorm(x, g):
    xf = x.astype(F32)
    y = xf * lax.rsqrt(jnp.mean(xf * xf, axis=-1, keepdims=True) + RMS_EPS)
    return (y * g.astype(F32)).astype(x.dtype)


def _rwkv_layer(x, h_prev0, s0, prm, nh):
    (norm, mu, w_in, w0, w1, w2, a0, a1, a2, k_k, k_a, r_k, lnx_w, lnx_b, w_out) = prm
    bsz, t, d = x.shape
    e = d // nh
    m = bsz * t
    h = _rmsnorm(x, norm)
    h_prev = jnp.concatenate([h_prev0[:, None, :].astype(h.dtype), h[:, :-1, :]], axis=1)
    xx = h_prev - h
    xs = [(h + xx * mu[s]).reshape(m, d).astype(BF16) for s in range(mu.shape[0])]
    r, k, v, gate = (_matmul(xs[s], w_in[s].astype(BF16)) for s in range(4))
    w_lora = _matmul(jnp.tanh(_matmul(xs[4], w1.astype(BF16))).astype(BF16), w2.astype(BF16))
    w_log = -jax.nn.softplus(-(w0 + w_lora)) - 0.5
    lw = -jnp.exp(w_log)
    a_lora = _matmul(_matmul(xs[5], a1.astype(BF16)).astype(BF16), a2.astype(BF16))
    a = jax.nn.sigmoid(a0 + a_lora)
    hd = lambda z: z.reshape(m, nh, e)
    kk = hd(k * k_k)
    kk = kk / jnp.maximum(jnp.sqrt(jnp.sum(kk * kk, axis=-1, keepdims=True)), 1e-12)
    kf = k * (1.0 + (a - 1.0) * k_a)
    neg_kk = (-kk).reshape(m, d)
    kk_a = (kk * hd(a)).reshape(m, d)
    if s0 is None:
        seq = lambda z: z.reshape(bsz, t, d)
        o, zt = _wkv_chunked(seq(r), seq(lw), seq(kf), seq(v), seq(neg_kk), seq(kk_a))
        zt = zt.reshape(bsz, d // LANES, 2, e, 2, e)
        s_fin = jnp.stack([zt[:, :, 0, :, 0, :], zt[:, :, 1, :, 1, :]], axis=2).reshape(bsz, nh, e, e)
        o = o.reshape(m, nh, e)
    else:
        o, s_fin = _wkv_step(s0, hd(r), hd(jnp.exp(lw)), hd(kf), hd(v), hd(neg_kk), hd(kk_a))
    mean = jnp.mean(o, axis=-1, keepdims=True)
    var = jnp.mean(jnp.square(o - mean), axis=-1, keepdims=True)
    o = (o - mean) * lax.rsqrt(var + LNX_EPS)
    o = o * lnx_w.reshape(nh, e) + lnx_b.reshape(nh, e)
    o = o + jnp.sum(hd(r) * hd(kf) * r_k, axis=-1, keepdims=True) * hd(v)
    o = o.reshape(m, d) * jax.nn.silu(gate)
    x_new = _matmul(o.astype(BF16), w_out.astype(BF16), res=x.reshape(m, d)).reshape(bsz, t, d)
    return x_new, s_fin, h[:, -1, :]


def _fox_project(x, norm, w_in, b_f, qn_g, kn_g, nh):
    bsz, t, d = x.shape
    e = d // nh
    m = bsz * t
    hb = _rmsnorm(x, norm).reshape(m, d).astype(BF16)
    proj = _matmul(hb, w_in[:, :4 * d].astype(BF16))
    f_logit = _matmul(hb, w_in[:, 4 * d:].astype(BF16))
    q, k, v, gate = (proj[:, s * d:(s + 1) * d] for s in range(4))
    q = _rmsnorm(q.reshape(m, nh, e), qn_g).reshape(bsz, t, d)
    k = _rmsnorm(k.reshape(m, nh, e), kn_g).reshape(bsz, t, d)
    logf = jax.nn.log_sigmoid(f_logit + b_f).reshape(bsz, t, nh)
    return q, k, v.reshape(bsz, t, d), gate, logf


def _fox_finish(x, o, gate, w_out):
    bsz, t, d = x.shape
    m = bsz * t
    y = (o.reshape(m, d) * jax.nn.silu(gate)).astype(BF16)
    return _matmul(y, w_out.astype(BF16), res=x.reshape(m, d)).reshape(bsz, t, d)


def kernel(x_prompt, x_sample, state_wkv, state_shift, cache_k, cache_v, cache_logf, page_table,
           norm_a, mu_a, w_in_a, w0_a, w1_a, w2_a, a0_a, a1_a, a2_a, kk_a, ka_a, rk_a, lnx_w_a, lnx_b_a, w_out_a,
           norm_b, w_in_b, bf_b, qn_b, kn_b, w_out_b):
    bsz, t, d = x_prompt.shape
    nb, ts, _ = x_sample.shape
    assert ts == 1, "the sample group carries one new token per sequence"
    nh, e = rk_a.shape[1], rk_a.shape[2]
    assert d == nh * e and 2 * e == LANES and t % WKV_CHUNK == 0
    n_layers_a = norm_a.shape[0]
    n_layers_b = norm_b.shape[0]
    depth = n_layers_a + n_layers_b
    n_pool, page = cache_k.shape[1], cache_k.shape[2]
    n_pages = page_table.shape[1]
    past = n_pages * page
    scale = e ** -0.5
    ckt = jnp.transpose(cache_k, (0, 1, 3, 4, 2))
    cvt = jnp.transpose(cache_v, (0, 1, 3, 4, 2))

    xp, xs = x_prompt, x_sample
    kp_l, vp_l, fp_l, sp_l, hp_l = [], [], [], [], []
    ks_l, vs_l, fs_l, ss_l, hs_l = [], [], [], [], []
    for i in range(depth):
        j = i // 2
        if i % 2 == 0:
            prm = (norm_a[j], mu_a[j], w_in_a[j], w0_a[j], w1_a[j], w2_a[j], a0_a[j], a1_a[j], a2_a[j],
                   kk_a[j], ka_a[j], rk_a[j], lnx_w_a[j], lnx_b_a[j], w_out_a[j])
            xp, s_p, l_p = _rwkv_layer(xp, jnp.zeros((bsz, d), xp.dtype), None, prm, nh)
            xs, s_s, l_s = _rwkv_layer(xs, state_shift[j], state_wkv[j], prm, nh)
            sp_l.append(s_p); hp_l.append(l_p); ss_l.append(s_s); hs_l.append(l_s)
        else:
            qp, kp, vp, gp, lfp = _fox_project(xp, norm_b[j], w_in_b[j], bf_b[j], qn_b[j], kn_b[j], nh)
            cp = jnp.cumsum(lfp, axis=1)
            op = _fox_flash(qp, kp, vp, cp, scale)
            xp = _fox_finish(xp, op, gp, w_out_b[j])

            qs, ksn, vsn, gs, lfs = _fox_project(xs, norm_b[j], w_in_b[j], bf_b[j], qn_b[j], kn_b[j], nh)
            f_past = cache_logf[j][page_table].reshape(nb, past, nh).astype(F32)
            c_all = jnp.cumsum(jnp.concatenate([f_past, lfs], axis=1), axis=1)
            bias = c_all[:, past:, :] - c_all[:, :past, :]
            bias_t = bias.reshape(nb, n_pages, page, nh).transpose(0, 1, 3, 2)
            osm = _fox_decode(j, (qs * scale).reshape(nb, nh, e), ksn.reshape(nb, nh, e),
                              vsn.reshape(nb, nh, e), ckt, cvt, bias_t, page_table)
            xs = _fox_finish(xs, osm.reshape(nb, 1, d), gs, w_out_b[j])
            hd4 = lambda z, n_, t_: z.reshape(n_, t_, nh, e)
            kp_l.append(hd4(kp, bsz, t)); vp_l.append(hd4(vp, bsz, t)); fp_l.append(lfp)
            ks_l.append(hd4(ksn, nb, ts)); vs_l.append(hd4(vsn, nb, ts)); fs_l.append(lfs)
    return (xp, xs,
            jnp.stack(kp_l), jnp.stack(vp_l), jnp.stack(fp_l), jnp.stack(sp_l), jnp.stack(hp_l),
            jnp.stack(ks_l), jnp.stack(vs_l), jnp.stack(fs_l), jnp.stack(ss_l), jnp.stack(hs_l))
```

```python
import functools

import jax
import jax.numpy as jnp
from jax import lax
from jax.experimental import pallas as pl
from jax.experimental.pallas import tpu as pltpu

F32 = jnp.float32
BF16 = jnp.bfloat16
HIGHEST = lax.Precision.HIGHEST

LANES = 128
RMS_EPS = 1e-6
LNX_EPS = 64e-5
NEG_BIG = -1e30
LOG2E = 1.4426950408889634
WKV_CHUNK = 64
VMEM_LIMIT = 48 * 1024 * 1024


def _nt(x, y, precision=None):
    return lax.dot_general(x, y, (((1,), (1,)), ((), ())), precision=precision,
                           preferred_element_type=F32)


def _tn(x, y, precision=None):
    return lax.dot_general(x, y, (((0,), (0,)), ((), ())), precision=precision,
                           preferred_element_type=F32)


def _nn(x, y, precision=None):
    return jnp.dot(x, y, precision=precision, preferred_element_type=F32)


def _pick_tile(n, candidates):
    for c in candidates:
        if n % c == 0:
            return c
    return n


def _wkv_chunk_kernel(r_ref, lw_ref, k_ref, v_ref, a_ref, b_ref, o_ref, zt_ref):
    c = pl.program_id(1)

    @pl.when(c == 0)
    def _():
        zt_ref[...] = jnp.zeros_like(zt_ref)

    L = r_ref.shape[1]
    n = 2 * L
    half = LANES // 2
    npair = r_ref.shape[2] // LANES

    ti = lax.broadcasted_iota(jnp.int32, (L, L), 0)
    tj = lax.broadcasted_iota(jnp.int32, (L, L), 1)
    tri = (ti >= tj).astype(BF16)
    head0 = lax.broadcasted_iota(jnp.int32, (L, LANES), 1) < half
    i = lax.broadcasted_iota(jnp.int32, (n, n), 0)
    j = lax.broadcasted_iota(jnp.int32, (n, n), 1)
    strict = i > j
    incl = i >= j
    eye = jnp.where(i == j, 1.0, 0.0)
    first = (i >> 1) == (j >> 1)
    levels = []
    lvl = 1
    while (2 << lvl) <= L:
        levels.append(((i >> (lvl + 1)) == (j >> (lvl + 1))) & ((i >> lvl) != (j >> lvl)))
        lvl += 1

    def stack(x):
        return jnp.concatenate([jnp.where(head0, x, 0.0), jnp.where(head0, 0.0, x)], axis=0)

    pairs = range(npair)
    sls = [slice(p * LANES, (p + 1) * LANES) for p in pairs]
    ar, bk, bk_h, vs, p_last = [], [], [], [], []
    for p in pairs:
        r, lw, k, v, a, b = (ref[0, :, sls[p]] for ref in (r_ref, lw_ref, k_ref, v_ref, a_ref, b_ref))
        lw1 = lw.astype(BF16)
        rem = lw - lw1.astype(F32)
        lw2 = rem.astype(BF16)
        lw3 = (rem - lw2.astype(F32)).astype(BF16)
        cum = _nn(tri, lw1) + (_nn(tri, lw2) + _nn(tri, lw3))
        p_inc = jnp.exp(cum)
        p_exc = jnp.exp(cum - lw)
        p_inv = jnp.exp(-cum)
        pl_ = p_inc[L - 1:L, :]
        bk_f = jnp.concatenate([stack(b * p_inv), stack(k * p_inv)], axis=0)
        ar.append(jnp.concatenate([stack(a * p_exc), stack(r * p_inc)], axis=0).astype(BF16))
        bk.append(bk_f.astype(BF16))
        bk_h.append((bk_f * pl_).astype(BF16))
        vs.append(stack(v).astype(BF16))
        p_last.append(pl_)

    g = [_nt(ar[p], bk[p]) for p in pairs]
    a_ab = [jnp.where(strict, g[p][:n, :n], 0.0) for p in pairs]
    ak_rk = [jnp.concatenate([jnp.where(strict, g[p][:n, n:], 0.0),
                              jnp.where(incl, g[p][n:, n:], 0.0)], axis=0).astype(BF16) for p in pairs]
    a_rb = [jnp.where(incl, g[p][n:, :n], 0.0).astype(BF16) for p in pairs]

    t_inv = [eye + jnp.where(first, a_ab[p], 0.0) for p in pairs]
    for off in levels:
        t_b = [t_inv[p].astype(BF16) for p in pairs]
        x = [_nn(t_b[p], jnp.where(off, a_ab[p], 0.0).astype(BF16)).astype(BF16) for p in pairs]
        t_inv = [t_inv[p] + _nn(x[p], t_b[p]) for p in pairs]
    t_b = [t_inv[p].astype(BF16) for p in pairs]

    zt = [zt_ref[0, p] for p in pairs]
    y = [_nt(ar[p], zt[p].astype(BF16)) + _nn(ak_rk[p], vs[p]) for p in pairs]
    u_b = [_nn(t_b[p], y[p][:n].astype(BF16)).astype(BF16) for p in pairs]
    for p in pairs:
        o_st = y[p][n:] + _nn(a_rb[p], u_b[p])
        o_ref[0, :, sls[p]] = o_st[:L] + o_st[L:]
        zt_ref[0, p] = zt[p] * p_last[p] + _tn(jnp.concatenate([u_b[p], vs[p]], axis=0), bk_h[p])


def _wkv_chunked(r, lw, k, v, a, b):
    bsz, t, d = r.shape
    npair = d // LANES
    L = WKV_CHUNK
    spec = pl.BlockSpec((1, L, d), lambda bi, c: (bi, c, 0))
    return pl.pallas_call(
        _wkv_chunk_kernel,
        grid=(bsz, t // L),
        in_specs=[spec] * 6,
        out_specs=[spec, pl.BlockSpec((1, npair, LANES, LANES), lambda bi, c: (bi, 0, 0, 0))],
        out_shape=[jax.ShapeDtypeStruct((bsz, t, d), F32),
                   jax.ShapeDtypeStruct((bsz, npair, LANES, LANES), F32)],
        compiler_params=pltpu.CompilerParams(
            dimension_semantics=("parallel", "arbitrary"), vmem_limit_bytes=VMEM_LIMIT),
        name="wkv_chunked",
    )(r, lw, k, v, a, b)


def _wkv_step_kernel(s_ref, r_ref, w_ref, k_ref, a_ref, b_ref, v_ref, so_ref, o_ref):
    s = s_ref[...]
    sa = jnp.sum(s * a_ref[...], axis=-1, keepdims=True)
    s_new = s * w_ref[...] + sa * b_ref[...] + v_ref[...] * k_ref[...]
    so_ref[...] = s_new
    o_ref[...] = jnp.sum(s_new * r_ref[...], axis=-1, keepdims=True)


def _wkv_step(s0, r, w, k, v, a, b):
    n, h, e, _ = s0.shape
    bb = _pick_tile(n, (8, 4, 2, 1))
    row = lambda x: x.reshape(n, h, 1, e)
    s_spec = pl.BlockSpec((bb, h, e, e), lambda i: (i, 0, 0, 0))
    row_spec = pl.BlockSpec((bb, h, 1, e), lambda i: (i, 0, 0, 0))
    col_spec = pl.BlockSpec((bb, h, e, 1), lambda i: (i, 0, 0, 0))
    s_new, o = pl.pallas_call(
        _wkv_step_kernel,
        grid=(n // bb,),
        in_specs=[s_spec] + [row_spec] * 5 + [col_spec],
        out_specs=[s_spec, col_spec],
        out_shape=[jax.ShapeDtypeStruct((n, h, e, e), F32), jax.ShapeDtypeStruct((n, h, e, 1), F32)],
        compiler_params=pltpu.CompilerParams(
            dimension_semantics=("parallel",), vmem_limit_bytes=VMEM_LIMIT),
        name="wkv_step",
    )(s0, row(r), row(w), row(k), row(a), row(b), v.reshape(n, h, e, 1))
    return o.reshape(n, h, e), s_new


def _fox_flash_kernel(qi_tab, kj_tab, q_ref, k_ref, v_ref, o_ref, m_ref, l_ref, acc_ref):
    step = pl.program_id(2)
    qi = qi_tab[step]
    kj = kj_tab[step]
    tq = q_ref.shape[1]
    tk = k_ref.shape[1]
    half = LANES // 2

    @pl.when(kj == 0)
    def _():
        m_ref[...] = jnp.full_like(m_ref, NEG_BIG)
        l_ref[...] = jnp.zeros_like(l_ref)
        acc_ref[...] = jnp.zeros_like(acc_ref)

    q_head0 = lax.broadcasted_iota(jnp.int32, (tq, LANES), 1) < half

    def update(on_diagonal):
        v = v_ref[0]
        v_head0 = lax.broadcasted_iota(jnp.int32, (tk, LANES), 1) < half
        zero = jnp.zeros_like(v)
        v_own = (jnp.where(v_head0, v, zero), jnp.where(v_head0, zero, v))
        if on_diagonal:
            keep = (lax.broadcasted_iota(jnp.int32, (tq, tk), 0)
                    >= lax.broadcasted_iota(jnp.int32, (tq, tk), 1))
        heads = range(2)
        s = [_nt(q_ref[0, :, h * LANES:(h + 1) * LANES], k_ref[0, :, h * LANES:(h + 1) * LANES])
             for h in heads]
        if on_diagonal:
            s = [jnp.where(keep, x, NEG_BIG) for x in s]
        m_prev = [m_ref[h] for h in heads]
        m_new = [jnp.maximum(m_prev[h], jnp.max(s[h], axis=1, keepdims=True)) for h in heads]
        alpha = [jnp.exp2(m_prev[h] - m_new[h]) for h in heads]
        pr = [jnp.exp2(s[h] - pltpu.repeat(m_new[h], tk // LANES, axis=1)) for h in heads]
        for h in heads:
            l_ref[h] = alpha[h] * l_ref[h] + jnp.sum(pr[h], axis=1, keepdims=True)
            m_ref[h] = m_new[h]
        pv = _nn(pr[0].astype(BF16), v_own[0]) + _nn(pr[1].astype(BF16), v_own[1])
        acc_ref[...] = jnp.where(q_head0, alpha[0], alpha[1]) * acc_ref[...] + pv

    @pl.when(kj < qi)
    def _():
        update(False)

    @pl.when(kj == qi)
    def _():
        update(True)
        o_ref[0] = acc_ref[...] / jnp.where(q_head0, l_ref[0], l_ref[1])


def _fox_flash(q, k, v, c, scale, tile=512):
    bsz, t, d = q.shape
    nh = c.shape[-1]
    npair = d // LANES
    e = d // nh
    tq = tk = _pick_tile(t, (tile, 256, 128))
    nq = t // tq
    top16 = lambda z: lax.bitcast_convert_type(
        lax.bitcast_convert_type(z, jnp.uint32) & jnp.uint32(0xFFFF0000), F32)
    c2 = c * LOG2E
    hi = top16(c2)
    mid = top16(c2 - hi)
    c_hi, c_mid, c_lo = hi.astype(BF16), mid.astype(BF16), top16(c2 - hi - mid).astype(BF16)
    one = jnp.ones_like(c_hi)
    pad = jnp.zeros((bsz, t, nh, e - 6), BF16)
    aug_q = jnp.concatenate([jnp.stack([c_hi, c_mid, c_lo, one, one, one], axis=-1), pad], axis=-1)
    aug_k = jnp.concatenate([jnp.stack([one, one, one, -c_hi, -c_mid, -c_lo], axis=-1), pad], axis=-1)
    q_exp = jnp.concatenate([(q * (scale * LOG2E)).astype(BF16).reshape(bsz, t, nh, e), aug_q],
                            axis=-1).reshape(bsz, t, 2 * d)
    k_exp = jnp.concatenate([k.astype(BF16).reshape(bsz, t, nh, e), aug_k], axis=-1).reshape(bsz, t, 2 * d)
    qi_tab = jnp.asarray([i for i in range(nq) for _ in range(i + 1)], jnp.int32)
    kj_tab = jnp.asarray([j for i in range(nq) for j in range(i + 1)], jnp.int32)
    grid_spec = pltpu.PrefetchScalarGridSpec(
        num_scalar_prefetch=2,
        grid=(bsz, npair, int(qi_tab.shape[0])),
        in_specs=[pl.BlockSpec((1, tq, 2 * LANES), lambda bi, p, s, qt, kt: (bi, qt[s], p)),
                  pl.BlockSpec((1, tk, 2 * LANES), lambda bi, p, s, qt, kt: (bi, kt[s], p)),
                  pl.BlockSpec((1, tk, LANES), lambda bi, p, s, qt, kt: (bi, kt[s], p))],
        out_specs=pl.BlockSpec((1, tq, LANES), lambda bi, p, s, qt, kt: (bi, qt[s], p)),
        scratch_shapes=[pltpu.VMEM((2, tq, LANES), F32), pltpu.VMEM((2, tq, LANES), F32),
                        pltpu.VMEM((tq, LANES), F32)],
    )
    return pl.pallas_call(
        _fox_flash_kernel,
        grid_spec=grid_spec,
        out_shape=jax.ShapeDtypeStruct((bsz, t, d), F32),
        compiler_params=pltpu.CompilerParams(
            dimension_semantics=("parallel", "parallel", "arbitrary"), vmem_limit_bytes=VMEM_LIMIT),
        name="fox_flash",
    )(qi_tab, kj_tab, q_exp, k_exp, v.astype(BF16))


def _fox_decode_kernel(pt_ref, q_ref, *refs, npg):
    k_refs = refs[:npg]
    v_refs = refs[npg:2 * npg]
    bias_ref, kn_ref, vn_ref, o_ref, m_ref, l_ref, acc_ref = refs[2 * npg:]
    pg = pl.program_id(1)

    @pl.when(pg == 0)
    def _():
        m_ref[...] = jnp.full_like(m_ref, NEG_BIG)
        l_ref[...] = jnp.zeros_like(l_ref)
        acc_ref[...] = jnp.zeros_like(acc_ref)

    q = q_ref[0]
    s = [jnp.sum(k_refs[t][0, 0] * q, axis=1, keepdims=True) + bias_ref[0, t] for t in range(npg)]
    m_prev = m_ref[...]
    m_new = m_prev
    for t in range(npg):
        m_new = jnp.maximum(m_new, jnp.max(s[t], axis=2, keepdims=True))
    alpha = jnp.exp(m_prev - m_new)
    l_new = alpha * l_ref[...]
    acc = alpha * acc_ref[...]
    for t in range(npg):
        pr = jnp.exp(s[t] - m_new)
        l_new = l_new + jnp.sum(pr, axis=2, keepdims=True)
        acc = acc + pr * v_refs[t][0, 0]
    l_ref[...] = l_new
    acc_ref[...] = acc
    m_ref[...] = m_new

    @pl.when(pg == pl.num_programs(1) - 1)
    def _():
        s_new = jnp.sum(q * kn_ref[0], axis=1, keepdims=True)
        m_fin = jnp.maximum(m_new, s_new)
        a_old = jnp.exp(m_new - m_fin)
        p_new = jnp.exp(s_new - m_fin)
        l_fin = l_new * a_old + p_new
        o_ref[0] = (jnp.sum(acc, axis=2, keepdims=True) * a_old + p_new * vn_ref[0]) / l_fin


def _fox_decode(layer, q, k_new, v_new, cache_kt, cache_vt, bias, page_table):
    n, nh, e = q.shape
    n_pages = page_table.shape[1]
    page = cache_kt.shape[4]
    npg = _pick_tile(n_pages, (4, 2, 1))

    def kv_spec(t):
        return pl.BlockSpec((1, 1, nh, e, page), lambda b, pg, pt: (layer, pt[b, pg * npg + t], 0, 0, 0))

    col = lambda x: x.reshape(n, nh, e, 1)
    col_spec = pl.BlockSpec((1, nh, e, 1), lambda b, pg, pt: (b, 0, 0, 0))
    grid_spec = pltpu.PrefetchScalarGridSpec(
        num_scalar_prefetch=1,
        grid=(n, n_pages // npg),
        in_specs=[col_spec] + [kv_spec(t) for t in range(npg)] * 2
                 + [pl.BlockSpec((1, npg, nh, 1, page), lambda b, pg, pt: (b, pg, 0, 0, 0)),
                    col_spec, col_spec],
        out_specs=col_spec,
        scratch_shapes=[pltpu.VMEM((nh, 1, 1), F32), pltpu.VMEM((nh, 1, 1), F32),
                        pltpu.VMEM((nh, e, page), F32)],
    )
    o = pl.pallas_call(
        functools.partial(_fox_decode_kernel, npg=npg),
        grid_spec=grid_spec,
        out_shape=jax.ShapeDtypeStruct((n, nh, e, 1), F32),
        compiler_params=pltpu.CompilerParams(
            dimension_semantics=("parallel", "arbitrary"), vmem_limit_bytes=VMEM_LIMIT),
        name="fox_decode",
    )(page_table, col(q), *([cache_kt] * npg), *([cache_vt] * npg),
      bias.reshape(n, n_pages, nh, 1, page), col(k_new), col(v_new))
    return o.reshape(n, nh, e)


def _rmsnorm(x, g):
    xf = x.astype(F32)
    y = xf * lax.rsqrt(jnp.mean(xf * xf, axis=-1, keepdims=True) + RMS_EPS)
    return (y * g.astype(F32)).astype(x.dtype)


def _rwkv_proj_kernel(x_ref, hp_ref, g_ref, mu_ref, w_ref, w0_ref, w1_ref, w2_ref, a0_ref, a1_ref, a2_ref,
                      kk_ref, ka_ref, seg_ref, segt_ref,
                      r_ref, lw_ref, kf_ref, v_ref, nkk_ref, kka_ref, gate_ref, carry_ref,
                      *, tiles_per_seq):
    x = x_ref[...]
    tm = x.shape[0]
    g = g_ref[...]
    h = x * lax.rsqrt(jnp.mean(x * x, axis=-1, keepdims=True) + RMS_EPS) * g
    if tiles_per_seq == 0:
        h_prev = hp_ref[...]
    else:
        first = pl.program_id(0) % tiles_per_seq == 0
        before = jnp.where(first, hp_ref[0], carry_ref[...])
        row = lax.broadcasted_iota(jnp.int32, x.shape, 0)
        h_prev = jnp.where(row == 0, before, pltpu.roll(h, 1, 0))
        carry_ref[...] = h[tm - 1:tm, :]
    xx = h_prev - h
    mix = lambda s: (h + xx * mu_ref[s:s + 1, :]).astype(BF16)
    r = _nn(mix(0), w_ref[0])
    k = _nn(mix(1), w_ref[1])
    v = _nn(mix(2), w_ref[2])
    gate_ref[...] = _nn(mix(3), w_ref[3])
    z = w0_ref[...] + _nn(jnp.tanh(_nn(mix(4), w1_ref[...])).astype(BF16), w2_ref[...])
    lw_ref[...] = -jnp.exp(jnp.minimum(z, 0.0) - jnp.log(1.0 + jnp.exp(-jnp.abs(z))) - 0.5)
    a = jax.nn.sigmoid(a0_ref[...] + _nn(_nn(mix(5), a1_ref[...]).astype(BF16), a2_ref[...]))
    kk = k * kk_ref[...]
    inv = 1.0 / jnp.maximum(jnp.sqrt(_seg_sum(kk * kk, seg_ref[...])), 1e-12)
    kk = kk * _seg_bcast(inv, segt_ref[...])
    r_ref[...] = r
    v_ref[...] = v
    kf_ref[...] = k * (1.0 + (a - 1.0) * ka_ref[...])
    nkk_ref[...] = -kk
    kka_ref[...] = kk * a


def _rwkv_out_kernel(o_ref, r_ref, kf_ref, v_ref, gate_ref, x_ref, lw_ref, lb_ref, rk_ref, w_ref,
                     seg_ref, segt_ref, y_ref):
    seg = seg_ref[...]
    seg_t = segt_ref[...]
    o = o_ref[...]
    inv_e = 1.0 / (o.shape[1] // seg.shape[1])
    cen = o - _seg_bcast(_seg_sum(o, seg) * inv_e, seg_t)
    var = _seg_sum(cen * cen, seg) * inv_e
    gn = cen * _seg_bcast(lax.rsqrt(var + LNX_EPS), seg_t) * lw_ref[...] + lb_ref[...]
    bonus = _seg_bcast(_seg_sum(r_ref[...] * kf_ref[...] * rk_ref[...], seg), seg_t) * v_ref[...]
    g = gate_ref[...]
    y = ((gn + bonus) * (g * jax.nn.sigmoid(g))).astype(BF16)
    y_ref[...] = x_ref[...] + _nn(y, w_ref[...])


def _rwkv_layer(x, h_prev0, s0, prm, nh):
    (norm, mu, w_in, w0, w1, w2, a0, a1, a2, k_k, k_a, r_k, lnx_w, lnx_b, w_out) = prm
    bsz, t, d = x.shape
    e = d // nh
    m = bsz * t
    seg, seg_t = _seg_mats(d, nh)
    row = lambda z: z.reshape(1, -1).astype(F32)
    full = lambda arr: pl.BlockSpec(arr.shape, lambda i: (0,) * arr.ndim)
    x2 = x.reshape(m, d)
    if t == 1:
        tm = _pick_tile(m, (256, 128))
        tiles_per_seq = 0
        hp = h_prev0.astype(F32)
        hp_spec = pl.BlockSpec((tm, d), lambda i: (i, 0))
    else:
        tm = _pick_tile(t, (256, 128))
        tiles_per_seq = t // tm
        hp = h_prev0.astype(F32).reshape(bsz, 1, d)
        hp_spec = pl.BlockSpec((1, 1, d), lambda i: (i // tiles_per_seq, 0, 0))
    tile = pl.BlockSpec((tm, d), lambda i: (i, 0))
    consts = [row(norm), mu.astype(F32), w_in.astype(BF16), row(w0), w1.astype(BF16), w2.astype(BF16),
              row(a0), a1.astype(BF16), a2.astype(BF16), row(k_k), row(k_a), seg, seg_t]
    r, lw, kf, v, neg_kk, kk_a, gate = pl.pallas_call(
        functools.partial(_rwkv_proj_kernel, tiles_per_seq=tiles_per_seq),
        grid=(m // tm,),
        in_specs=[tile, hp_spec] + [full(c) for c in consts],
        out_specs=[tile] * 7,
        out_shape=[jax.ShapeDtypeStruct((m, d), F32)] * 7,
        scratch_shapes=[pltpu.VMEM((1, d), F32)],
        compiler_params=pltpu.CompilerParams(
            dimension_semantics=("arbitrary",), vmem_limit_bytes=VMEM_LIMIT),
        name="rwkv_proj",
    )(x2, hp, *consts)

    hd = lambda z: z.reshape(m, nh, e)
    if s0 is None:
        seq = lambda z: z.reshape(bsz, t, d)
        o, zt = _wkv_chunked(seq(r), seq(lw), seq(kf), seq(v), seq(neg_kk), seq(kk_a))
        zt = zt.reshape(bsz, d // LANES, 2, e, 2, e)
        s_fin = jnp.stack([zt[:, :, 0, :, 0, :], zt[:, :, 1, :, 1, :]], axis=2).reshape(bsz, nh, e, e)
    else:
        o, s_fin = _wkv_step(s0, hd(r), hd(jnp.exp(lw)), hd(kf), hd(v), hd(neg_kk), hd(kk_a))

    tmo = _pick_tile(m, (256, 128))
    tile_o = pl.BlockSpec((tmo, d), lambda i: (i, 0))
    consts_o = [row(lnx_w), row(lnx_b), row(r_k), w_out.astype(BF16), seg, seg_t]
    x_new = pl.pallas_call(
        _rwkv_out_kernel,
        grid=(m // tmo,),
        in_specs=[tile_o] * 6 + [full(c) for c in consts_o],
        out_specs=tile_o,
        out_shape=jax.ShapeDtypeStruct((m, d), F32),
        compiler_params=pltpu.CompilerParams(
            dimension_semantics=("parallel",), vmem_limit_bytes=VMEM_LIMIT),
        name="rwkv_out",
    )(o.reshape(m, d), r, kf, v, gate, x2, *consts_o)
    h_last = _rmsnorm(x[:, -1, :], norm)
    return x_new.reshape(bsz, t, d), s_fin, h_last


def _split2(x):
    hi = x.astype(BF16)
    return hi, (x - hi.astype(F32)).astype(BF16)


def _seg_sum(x, seg):
    hi, lo = _split2(x)
    return _nn(hi, seg) + _nn(lo, seg)


def _seg_bcast(y, seg_t):
    hi, lo = _split2(y)
    return _nn(hi, seg_t) + _nn(lo, seg_t)


def _seg_mats(d, nh):
    lane_head = jnp.arange(d, dtype=jnp.int32) // (d // nh)
    seg = (lane_head[:, None] == jnp.arange(nh, dtype=jnp.int32)[None, :]).astype(BF16)
    return seg, seg.T


def _fox_proj_kernel(x_ref, g_ref, w_ref, wf_ref, bf_ref, qn_ref, kn_ref, seg_ref, segt_ref,
                     q_ref, k_ref, v_ref, gate_ref, lf_ref):
    x = x_ref[...]
    d = x.shape[1]
    hb = (x * lax.rsqrt(jnp.mean(x * x, axis=-1, keepdims=True) + RMS_EPS) * g_ref[...]).astype(BF16)
    seg = seg_ref[...]
    seg_t = segt_ref[...]
    inv_e = 1.0 / (d // seg.shape[1])

    def head_norm(z, gain):
        inv = lax.rsqrt(_seg_sum(z * z, seg) * inv_e + RMS_EPS)
        return z * _seg_bcast(inv, seg_t) * gain

    q_ref[...] = head_norm(_nn(hb, w_ref[:, 0:d]), qn_ref[...])
    k_ref[...] = head_norm(_nn(hb, w_ref[:, d:2 * d]), kn_ref[...])
    v_ref[...] = _nn(hb, w_ref[:, 2 * d:3 * d])
    gate_ref[...] = _nn(hb, w_ref[:, 3 * d:4 * d])
    f = _nn(hb, wf_ref[...]) + bf_ref[...]
    lf_ref[...] = jnp.minimum(f, 0.0) - jnp.log(1.0 + jnp.exp(-jnp.abs(f)))


def _fox_project(x, norm, w_in, b_f, qn_g, kn_g, nh):
    bsz, t, d = x.shape
    m = bsz * t
    tm = _pick_tile(m, (256, 128))
    seg, seg_t = _seg_mats(d, nh)
    row = lambda z: z.reshape(1, -1).astype(F32)
    full = lambda a: pl.BlockSpec(a.shape, lambda i: (0,) * a.ndim)
    consts = [row(norm), w_in[:, :4 * d].astype(BF16), w_in[:, 4 * d:].astype(BF16), row(b_f),
              row(jnp.tile(qn_g, nh)), row(jnp.tile(kn_g, nh)), seg, seg_t]
    tile = pl.BlockSpec((tm, d), lambda i: (i, 0))
    q, k, v, gate, logf = pl.pallas_call(
        _fox_proj_kernel,
        grid=(m // tm,),
        in_specs=[tile] + [full(a) for a in consts],
        out_specs=[tile] * 4 + [pl.BlockSpec((tm, nh), lambda i: (i, 0))],
        out_shape=[jax.ShapeDtypeStruct((m, d), F32)] * 4 + [jax.ShapeDtypeStruct((m, nh), F32)],
        compiler_params=pltpu.CompilerParams(
            dimension_semantics=("parallel",), vmem_limit_bytes=VMEM_LIMIT),
        name="fox_proj",
    )(x.reshape(m, d), *consts)
    seq = lambda z: z.reshape(bsz, t, d)
    return seq(q), seq(k), seq(v), gate, logf.reshape(bsz, t, nh)


def _gate_out_kernel(o_ref, gate_ref, x_ref, w_ref, y_ref):
    g = gate_ref[...]
    y = (o_ref[...] * (g * jax.nn.sigmoid(g))).astype(BF16)
    y_ref[...] = x_ref[...] + _nn(y, w_ref[...])


def _fox_finish(x, o, gate, w_out):
    bsz, t, d = x.shape
    m = bsz * t
    tm = _pick_tile(m, (512, 256, 128))
    tile = pl.BlockSpec((tm, d), lambda i: (i, 0))
    y = pl.pallas_call(
        _gate_out_kernel,
        grid=(m // tm,),
        in_specs=[tile, tile, tile, pl.BlockSpec((d, d), lambda i: (0, 0))],
        out_specs=tile,
        out_shape=jax.ShapeDtypeStruct((m, d), F32),
        compiler_params=pltpu.CompilerParams(
            dimension_semantics=("parallel",), vmem_limit_bytes=VMEM_LIMIT),
        name="gate_out",
    )(o.reshape(m, d), gate, x.reshape(m, d), w_out.astype(BF16))
    return y.reshape(bsz, t, d)


def kernel(x_prompt, x_sample, state_wkv, state_shift, cache_k, cache_v, cache_logf, page_table,
           norm_a, mu_a, w_in_a, w0_a, w1_a, w2_a, a0_a, a1_a, a2_a, kk_a, ka_a, rk_a, lnx_w_a, lnx_b_a, w_out_a,
           norm_b, w_in_b, bf_b, qn_b, kn_b, w_out_b):
    bsz, t, d = x_prompt.shape
    nb, ts, _ = x_sample.shape
    assert ts == 1, "the sample group carries one new token per sequence"
    nh, e = rk_a.shape[1], rk_a.shape[2]
    assert d == nh * e and 2 * e == LANES and t % WKV_CHUNK == 0
    n_layers_a = norm_a.shape[0]
    n_layers_b = norm_b.shape[0]
    depth = n_layers_a + n_layers_b
    n_pool, page = cache_k.shape[1], cache_k.shape[2]
    n_pages = page_table.shape[1]
    past = n_pages * page
    scale = e ** -0.5
    ckt = jnp.transpose(cache_k, (0, 1, 3, 4, 2))
    cvt = jnp.transpose(cache_v, (0, 1, 3, 4, 2))

    xp, xs = x_prompt, x_sample
    kp_l, vp_l, fp_l, sp_l, hp_l = [], [], [], [], []
    ks_l, vs_l, fs_l, ss_l, hs_l = [], [], [], [], []
    for i in range(depth):
        j = i // 2
        if i % 2 == 0:
            prm = (norm_a[j], mu_a[j], w_in_a[j], w0_a[j], w1_a[j], w2_a[j], a0_a[j], a1_a[j], a2_a[j],
                   kk_a[j], ka_a[j], rk_a[j], lnx_w_a[j], lnx_b_a[j], w_out_a[j])
            xp, s_p, l_p = _rwkv_layer(xp, jnp.zeros((bsz, d), xp.dtype), None, prm, nh)
            xs, s_s, l_s = _rwkv_layer(xs, state_shift[j], state_wkv[j], prm, nh)
            sp_l.append(s_p); hp_l.append(l_p); ss_l.append(s_s); hs_l.append(l_s)
        else:
            qp, kp, vp, gp, lfp = _fox_project(xp, norm_b[j], w_in_b[j], bf_b[j], qn_b[j], kn_b[j], nh)
            cp = jnp.cumsum(lfp, axis=1)
            op = _fox_flash(qp, kp, vp, cp, scale)
            xp = _fox_finish(xp, op, gp, w_out_b[j])

            qs, ksn, vsn, gs, lfs = _fox_project(xs, norm_b[j], w_in_b[j], bf_b[j], qn_b[j], kn_b[j], nh)
            f_past = cache_logf[j][page_table].reshape(nb, past, nh).astype(F32)
            c_all = jnp.cumsum(jnp.concatenate([f_past, lfs], axis=1), axis=1)
            bias = c_all[:, past:, :] - c_all[:, :past, :]
            bias_t = bias.reshape(nb, n_pages, page, nh).transpose(0, 1, 3, 2)
            osm = _fox_decode(j, (qs * scale).reshape(nb, nh, e), ksn.reshape(nb, nh, e),
                              vsn.reshape(nb, nh, e), ckt, cvt, bias_t, page_table)
            xs = _fox_finish(xs, osm.reshape(nb, 1, d), gs, w_out_b[j])
            hd4 = lambda z, n_, t_: z.reshape(n_, t_, nh, e)
            kp_l.append(hd4(kp, bsz, t)); vp_l.append(hd4(vp, bsz, t)); fp_l.append(lfp)
            ks_l.append(hd4(ksn, nb, ts)); vs_l.append(hd4(vsn, nb, ts)); fs_l.append(lfs)
    return (xp, xs,
            jnp.stack(kp_l), jnp.stack(vp_l), jnp.stack(fp_l), jnp.stack(sp_l), jnp.stack(hp_l),
            jnp.stack(ks_l), jnp.stack(vs_l), jnp.stack(fs_l), jnp.stack(ss_l), jnp.stack(hs_l))
```

```python
import functools

import jax
import jax.numpy as jnp
from jax import lax
from jax.experimental import pallas as pl
from jax.experimental.pallas import tpu as pltpu

F32 = jnp.float32
BF16 = jnp.bfloat16
HIGHEST = lax.Precision.HIGHEST

LANES = 128
RMS_EPS = 1e-6
LNX_EPS = 64e-5
NEG_BIG = -1e30
LOG2E = 1.4426950408889634
WKV_CHUNK = 64
VMEM_LIMIT = 48 * 1024 * 1024


def _nt(x, y, precision=None):
    return lax.dot_general(x, y, (((1,), (1,)), ((), ())), precision=precision,
                           preferred_element_type=F32)


def _tn(x, y, precision=None):
    return lax.dot_general(x, y, (((0,), (0,)), ((), ())), precision=precision,
                           preferred_element_type=F32)


def _nn(x, y, precision=None):
    return jnp.dot(x, y, precision=precision, preferred_element_type=F32)


def _pick_tile(n, candidates):
    for c in candidates:
        if n % c == 0:
            return c
    return n


def _wkv_chunk_kernel(r_ref, lw_ref, k_ref, v_ref, a_ref, b_ref, o_ref, zt_ref):
    c = pl.program_id(1)

    @pl.when(c == 0)
    def _():
        zt_ref[...] = jnp.zeros_like(zt_ref)

    L = r_ref.shape[1]
    n = 2 * L
    half = LANES // 2
    npair = r_ref.shape[2] // LANES

    ti = lax.broadcasted_iota(jnp.int32, (L, L), 0)
    tj = lax.broadcasted_iota(jnp.int32, (L, L), 1)
    tri = (ti >= tj).astype(BF16)
    head0 = lax.broadcasted_iota(jnp.int32, (L, LANES), 1) < half
    i = lax.broadcasted_iota(jnp.int32, (n, n), 0)
    j = lax.broadcasted_iota(jnp.int32, (n, n), 1)
    strict = i > j
    incl = i >= j
    eye = jnp.where(i == j, 1.0, 0.0)
    first = (i >> 1) == (j >> 1)
    levels = []
    lvl = 1
    while (2 << lvl) <= L:
        levels.append(((i >> (lvl + 1)) == (j >> (lvl + 1))) & ((i >> lvl) != (j >> lvl)))
        lvl += 1

    def stack(x):
        return jnp.concatenate([jnp.where(head0, x, 0.0), jnp.where(head0, 0.0, x)], axis=0)

    pairs = range(npair)
    sls = [slice(p * LANES, (p + 1) * LANES) for p in pairs]
    ar, bk, bk_h, vs, p_last = [], [], [], [], []
    for p in pairs:
        r, lw, k, v, a, b = (ref[0, :, sls[p]] for ref in (r_ref, lw_ref, k_ref, v_ref, a_ref, b_ref))
        lw1 = lw.astype(BF16)
        rem = lw - lw1.astype(F32)
        lw2 = rem.astype(BF16)
        lw3 = (rem - lw2.astype(F32)).astype(BF16)
        cum = _nn(tri, lw1) + (_nn(tri, lw2) + _nn(tri, lw3))
        p_inc = jnp.exp(cum)
        p_exc = jnp.exp(cum - lw)
        p_inv = jnp.exp(-cum)
        pl_ = p_inc[L - 1:L, :]
        bk_f = jnp.concatenate([stack(b * p_inv), stack(k * p_inv)], axis=0)
        ar.append(jnp.concatenate([stack(a * p_exc), stack(r * p_inc)], axis=0).astype(BF16))
        bk.append(bk_f.astype(BF16))
        bk_h.append((bk_f * pl_).astype(BF16))
        vs.append(stack(v).astype(BF16))
        p_last.append(pl_)

    g = [_nt(ar[p], bk[p]) for p in pairs]
    a_ab = [jnp.where(strict, g[p][:n, :n], 0.0) for p in pairs]
    ak_rk = [jnp.concatenate([jnp.where(strict, g[p][:n, n:], 0.0),
                              jnp.where(incl, g[p][n:, n:], 0.0)], axis=0).astype(BF16) for p in pairs]
    a_rb = [jnp.where(incl, g[p][n:, :n], 0.0).astype(BF16) for p in pairs]

    t_inv = [eye + jnp.where(first, a_ab[p], 0.0) for p in pairs]
    for off in levels:
        t_b = [t_inv[p].astype(BF16) for p in pairs]
        x = [_nn(t_b[p], jnp.where(off, a_ab[p], 0.0).astype(BF16)).astype(BF16) for p in pairs]
        t_inv = [t_inv[p] + _nn(x[p], t_b[p]) for p in pairs]
    t_b = [t_inv[p].astype(BF16) for p in pairs]

    zt = [zt_ref[0, p] for p in pairs]
    y = [_nt(ar[p], zt[p].astype(BF16)) + _nn(ak_rk[p], vs[p]) for p in pairs]
    u_b = [_nn(t_b[p], y[p][:n].astype(BF16)).astype(BF16) for p in pairs]
    for p in pairs:
        o_st = y[p][n:] + _nn(a_rb[p], u_b[p])
        o_ref[0, :, sls[p]] = o_st[:L] + o_st[L:]
        zt_ref[0, p] = zt[p] * p_last[p] + _tn(jnp.concatenate([u_b[p], vs[p]], axis=0), bk_h[p])


def _wkv_chunked(r, lw, k, v, a, b):
    bsz, t, d = r.shape
    npair = d // LANES
    L = WKV_CHUNK
    spec = pl.BlockSpec((1, L, d), lambda bi, c: (bi, c, 0))
    return pl.pallas_call(
        _wkv_chunk_kernel,
        grid=(bsz, t // L),
        in_specs=[spec] * 6,
        out_specs=[spec, pl.BlockSpec((1, npair, LANES, LANES), lambda bi, c: (bi, 0, 0, 0))],
        out_shape=[jax.ShapeDtypeStruct((bsz, t, d), F32),
                   jax.ShapeDtypeStruct((bsz, npair, LANES, LANES), F32)],
        compiler_params=pltpu.CompilerParams(
            dimension_semantics=("parallel", "arbitrary"), vmem_limit_bytes=VMEM_LIMIT),
        name="wkv_chunked",
    )(r, lw, k, v, a, b)


def _wkv_step_kernel(s_ref, r_ref, lw_ref, k_ref, v_ref, a_ref, b_ref, so_ref, o_ref):
    bb, nh, e, _ = s_ref.shape
    half = LANES // 2
    ii = lax.broadcasted_iota(jnp.int32, (e, LANES), 0)
    jj = lax.broadcasted_iota(jnp.int32, (e, LANES), 1)
    eye_lo = ii == jj
    eye_hi = ii + half == jj

    for i in range(bb):
        units = []
        for h in range(nh):
            p, par = divmod(h, 2)
            sl = slice(p * LANES, (p + 1) * LANES)
            r, lw, k, v, a, b = (ref[i:i + 1, sl] for ref in (r_ref, lw_ref, k_ref, v_ref, a_ref, b_ref))
            w = jnp.exp(lw)
            if par == 0:
                rh, wh, kh, ah, bh = (z[:, :e] for z in (r, w, k, a, b))
            else:
                rh, wh, kh, ah, bh = (pltpu.roll(z, half, 1)[:, :e] for z in (r, w, k, a, b))
            s = s_ref[i, h]
            v_col = jnp.sum(jnp.where(eye_hi if par else eye_lo, v, 0.0), axis=1, keepdims=True)
            sa = jnp.sum(s * ah, axis=1, keepdims=True)
            units.append((s, rh, wh, kh, bh, v_col, sa))
        o_cols = []
        for h, (s, rh, wh, kh, bh, v_col, sa) in enumerate(units):
            s_new = s * wh + sa * bh + v_col * kh
            so_ref[i, h] = s_new
            o_cols.append(jnp.sum(s_new * rh, axis=1, keepdims=True))
        for p in range(nh // 2):
            o_ref[i:i + 1, p * LANES:(p + 1) * LANES] = (
                jnp.sum(jnp.where(eye_lo, o_cols[2 * p], 0.0), axis=0, keepdims=True)
                + jnp.sum(jnp.where(eye_hi, o_cols[2 * p + 1], 0.0), axis=0, keepdims=True))


def _wkv_step(s0, r, lw, k, v, a, b):
    n, h, e, _ = s0.shape
    d = h * e
    bb = _pick_tile(n, (8,))
    s_spec = pl.BlockSpec((bb, h, e, e), lambda i: (i, 0, 0, 0))
    row_spec = pl.BlockSpec((bb, d), lambda i: (i, 0))
    s_new, o = pl.pallas_call(
        _wkv_step_kernel,
        grid=(n // bb,),
        in_specs=[s_spec] + [row_spec] * 6,
        out_specs=[s_spec, row_spec],
        out_shape=[jax.ShapeDtypeStruct((n, h, e, e), F32), jax.ShapeDtypeStruct((n, d), F32)],
        compiler_params=pltpu.CompilerParams(
            dimension_semantics=("parallel",), vmem_limit_bytes=VMEM_LIMIT),
        name="wkv_step",
    )(s0, r, lw, k, v, a, b)
    return o, s_new


def _fox_flash_kernel(qi_tab, kj_tab, q_ref, k_ref, v_ref, cq_ref, ck_ref, o_ref, m_ref, l_ref, acc_ref):
    p = pl.program_id(1)
    step = pl.program_id(2)
    qi = qi_tab[step]
    kj = kj_tab[step]
    tq = q_ref.shape[1]
    tk = k_ref.shape[1]
    half = LANES // 2
    n_piece_rows = cq_ref.shape[2]
    nh = n_piece_rows // 3

    @pl.when(kj == 0)
    def _():
        m_ref[...] = jnp.full_like(m_ref, NEG_BIG)
        l_ref[...] = jnp.zeros_like(l_ref)
        acc_ref[...] = jnp.zeros_like(acc_ref)

    q_head0 = lax.broadcasted_iota(jnp.int32, (tq, LANES), 1) < half
    k_head0 = lax.broadcasted_iota(jnp.int32, (tk, LANES), 1) < half

    prow = lax.broadcasted_iota(jnp.int32, (n_piece_rows, LANES), 0)
    plane = lax.broadcasted_iota(jnp.int32, (n_piece_rows, LANES), 1)
    piece, head = prow >> (nh.bit_length() - 1), prow & (nh - 1)
    base = jnp.where(head == 2 * p, half, jnp.where(head == 2 * p + 1, 0, -LANES))
    place_q = jnp.where(plane == base + piece, 1.0, 0.0).astype(BF16)
    place_k = jnp.where(plane == base + piece + 3, -1.0, 0.0).astype(BF16)
    slot = lax.broadcasted_iota(jnp.int32, (1, LANES), 1) & (half - 1)
    ones_q = jnp.where((slot >= 3) & (slot < 6), 1.0, 0.0)
    ones_k = jnp.where(slot < 3, 1.0, 0.0)

    def update(on_diagonal):
        v = v_ref[0]
        zero = jnp.zeros_like(v)
        v_own = (jnp.where(k_head0, v, zero), jnp.where(k_head0, zero, v))
        q = q_ref[0]
        k = k_ref[0]
        q_aug = (_nn(cq_ref[0], place_q) + ones_q).astype(BF16)
        k_aug = (_nn(ck_ref[0], place_k) + ones_k).astype(BF16)
        q_ops = (jnp.where(q_head0, q, q_aug), jnp.where(q_head0, q_aug, q))
        k_ops = (jnp.where(k_head0, k, k_aug), jnp.where(k_head0, k_aug, k))
        if on_diagonal:
            keep = (lax.broadcasted_iota(jnp.int32, (tq, tk), 0)
                    >= lax.broadcasted_iota(jnp.int32, (tq, tk), 1))
        heads = range(2)
        s = [_nt(q_ops[h], k_ops[h]) for h in heads]
        if on_diagonal:
            s = [jnp.where(keep, x, NEG_BIG) for x in s]
        m_prev = [m_ref[h] for h in heads]
        m_new = [jnp.maximum(m_prev[h], jnp.max(s[h], axis=1, keepdims=True)) for h in heads]
        alpha = [jnp.exp2(m_prev[h] - m_new[h]) for h in heads]
        pr = [jnp.exp2(s[h] - jnp.concatenate([m_new[h]] * (tk // LANES), axis=1)) for h in heads]
        for h in heads:
            l_ref[h] = alpha[h] * l_ref[h] + jnp.sum(pr[h], axis=1, keepdims=True)
            m_ref[h] = m_new[h]
        pv = _nn(pr[0].astype(BF16), v_own[0]) + _nn(pr[1].astype(BF16), v_own[1])
        acc_ref[...] = jnp.where(q_head0, alpha[0], alpha[1]) * acc_ref[...] + pv

    @pl.when(kj < qi)
    def _():
        update(False)

    @pl.when(kj == qi)
    def _():
        update(True)
        o_ref[0] = acc_ref[...] / jnp.where(q_head0, l_ref[0], l_ref[1])


def _fox_flash(q_b, k_b, v_b, c, tile=512):
    bsz, t, d = q_b.shape
    nh = c.shape[-1]
    assert nh & (nh - 1) == 0, "head count must be a power of two"
    npair = d // LANES
    tq = tk = _pick_tile(t, (tile, 256, 128))
    nq = t // tq
    top16 = lambda z: lax.bitcast_convert_type(
        lax.bitcast_convert_type(z, jnp.uint32) & jnp.uint32(0xFFFF0000), F32)
    c2 = c * LOG2E
    hi = top16(c2)
    mid = top16(c2 - hi)
    pieces = jnp.concatenate([hi, mid, top16(c2 - hi - mid)], axis=-1).astype(BF16)
    qi_tab = jnp.asarray([i for i in range(nq) for _ in range(i + 1)], jnp.int32)
    kj_tab = jnp.asarray([j for i in range(nq) for j in range(i + 1)], jnp.int32)
    q_map = lambda bi, p, s, qt, kt: (bi, qt[s], p)
    k_map = lambda bi, p, s, qt, kt: (bi, kt[s], p)
    grid_spec = pltpu.PrefetchScalarGridSpec(
        num_scalar_prefetch=2,
        grid=(bsz, npair, int(qi_tab.shape[0])),
        in_specs=[pl.BlockSpec((1, tq, LANES), q_map),
                  pl.BlockSpec((1, tk, LANES), k_map),
                  pl.BlockSpec((1, tk, LANES), k_map),
                  pl.BlockSpec((1, tq, 3 * nh), lambda bi, p, s, qt, kt: (bi, qt[s], 0)),
                  pl.BlockSpec((1, tk, 3 * nh), lambda bi, p, s, qt, kt: (bi, kt[s], 0))],
        out_specs=pl.BlockSpec((1, tq, LANES), q_map),
        scratch_shapes=[pltpu.VMEM((2, tq, LANES), F32), pltpu.VMEM((2, tq, LANES), F32),
                        pltpu.VMEM((tq, LANES), F32)],
    )
    return pl.pallas_call(
        _fox_flash_kernel,
        grid_spec=grid_spec,
        out_shape=jax.ShapeDtypeStruct((bsz, t, d), F32),
        compiler_params=pltpu.CompilerParams(
            dimension_semantics=("parallel", "parallel", "arbitrary"), vmem_limit_bytes=VMEM_LIMIT),
        name="fox_flash",
    )(qi_tab, kj_tab, q_b, k_b, v_b, pieces, pieces)


def _fox_decode_kernel(pt_ref, q_ref, *refs, npg):
    k_refs = refs[:npg]
    v_refs = refs[npg:2 * npg]
    bias_ref, kn_ref, vn_ref, o_ref, m_ref, l_ref, acc_ref = refs[2 * npg:]
    pg = pl.program_id(1)

    @pl.when(pg == 0)
    def _():
        m_ref[...] = jnp.full_like(m_ref, NEG_BIG)
        l_ref[...] = jnp.zeros_like(l_ref)
        acc_ref[...] = jnp.zeros_like(acc_ref)

    nh, e, page = k_refs[0].shape[2:]
    d = nh * e
    row = lax.broadcasted_iota(jnp.int32, (nh, d), 0)
    lane = lax.broadcasted_iota(jnp.int32, (nh, d), 1)
    own = (lane >= row * e) & (lane < (row + 1) * e)
    q_bd = jnp.where(own, q_ref[0], 0.0)
    q_b = q_bd.astype(BF16)
    s = [_nn(q_b, k_refs[t][0, 0].reshape(d, page).astype(BF16)) + bias_ref[0, t] for t in range(npg)]
    m_prev = m_ref[...]
    m_new = m_prev
    for t in range(npg):
        m_new = jnp.maximum(m_new, jnp.max(s[t], axis=1, keepdims=True))
    alpha = jnp.exp(m_prev - m_new)
    l_new = alpha * l_ref[...]
    acc = alpha * acc_ref[...]
    for t in range(npg):
        pr = jnp.exp(s[t] - m_new)
        l_new = l_new + jnp.sum(pr, axis=1, keepdims=True)
        acc = acc + _nt(pr.astype(BF16), v_refs[t][0, 0].reshape(d, page).astype(BF16))
    l_ref[...] = l_new
    acc_ref[...] = acc
    m_ref[...] = m_new

    @pl.when(pg == pl.num_programs(1) - 1)
    def _():
        s_new = jnp.sum(q_bd * kn_ref[0], axis=1, keepdims=True)
        m_fin = jnp.maximum(m_new, s_new)
        a_old = jnp.exp(m_new - m_fin)
        p_new = jnp.exp(s_new - m_fin)
        l_fin = l_new * a_old + p_new
        full = (acc * a_old + p_new * vn_ref[0]) / l_fin
        o_ref[0] = jnp.sum(jnp.where(own, full, 0.0), axis=0, keepdims=True)


def _fox_decode(layer, q, k_new, v_new, cache_kt, cache_vt, bias, page_table):
    n, d = q.shape
    n_pages = page_table.shape[1]
    nh, e, page = cache_kt.shape[2:]
    npg = _pick_tile(n_pages, (8, 4, 2, 1))

    def kv_spec(t):
        return pl.BlockSpec((1, 1, nh, e, page), lambda b, pg, pt: (layer, pt[b, pg * npg + t], 0, 0, 0))

    row_spec = pl.BlockSpec((1, 1, d), lambda b, pg, pt: (b, 0, 0))
    grid_spec = pltpu.PrefetchScalarGridSpec(
        num_scalar_prefetch=1,
        grid=(n, n_pages // npg),
        in_specs=[row_spec] + [kv_spec(t) for t in range(npg)] * 2
                 + [pl.BlockSpec((1, npg, nh, page), lambda b, pg, pt: (b, pg, 0, 0)), row_spec, row_spec],
        out_specs=row_spec,
        scratch_shapes=[pltpu.VMEM((nh, 1), F32), pltpu.VMEM((nh, 1), F32), pltpu.VMEM((nh, d), F32)],
    )
    o = pl.pallas_call(
        functools.partial(_fox_decode_kernel, npg=npg),
        grid_spec=grid_spec,
        out_shape=jax.ShapeDtypeStruct((n, 1, d), F32),
        compiler_params=pltpu.CompilerParams(
            dimension_semantics=("parallel", "arbitrary"), vmem_limit_bytes=VMEM_LIMIT),
        name="fox_decode",
    )(page_table, q.reshape(n, 1, d), *([cache_kt] * npg), *([cache_vt] * npg),
      bias, k_new.reshape(n, 1, d), v_new.reshape(n, 1, d))
    return o.reshape(n, d)


def _rmsnorm(x, g):
    xf = x.astype(F32)
    y = xf * lax.rsqrt(jnp.mean(xf * xf, axis=-1, keepdims=True) + RMS_EPS)
    return (y * g.astype(F32)).astype(x.dtype)


def _rwkv_proj_kernel(x_ref, hp_ref, g_ref, mu_ref, w_ref, w0_ref, w1_ref, w2_ref, a0_ref, a1_ref, a2_ref,
                      kk_ref, ka_ref, seg_ref, segt_ref,
                      r_ref, lw_ref, kf_ref, v_ref, nkk_ref, kka_ref, gate_ref, carry_ref,
                      *, tiles_per_seq):
    x = x_ref[...]
    tm = x.shape[0]
    g = g_ref[...]
    h = x * lax.rsqrt(jnp.mean(x * x, axis=-1, keepdims=True) + RMS_EPS) * g
    if tiles_per_seq == 0:
        h_prev = hp_ref[...]
    else:
        first = pl.program_id(0) % tiles_per_seq == 0
        before = jnp.where(first, hp_ref[0], carry_ref[...])
        row = lax.broadcasted_iota(jnp.int32, x.shape, 0)
        h_prev = jnp.where(row == 0, before, pltpu.roll(h, 1, 0))
        carry_ref[...] = h[tm - 1:tm, :]
    xx = h_prev - h
    mix = lambda s: (h + xx * mu_ref[s:s + 1, :]).astype(BF16)
    r = _nn(mix(0), w_ref[0])
    k = _nn(mix(1), w_ref[1])
    v = _nn(mix(2), w_ref[2])
    gate_ref[...] = _nn(mix(3), w_ref[3])
    z = w0_ref[...] + _nn(jnp.tanh(_nn(mix(4), w1_ref[...])).astype(BF16), w2_ref[...])
    lw_ref[...] = -jnp.exp(jnp.minimum(z, 0.0) - jnp.log(1.0 + jnp.exp(-jnp.abs(z))) - 0.5)
    a = jax.nn.sigmoid(a0_ref[...] + _nn(_nn(mix(5), a1_ref[...]).astype(BF16), a2_ref[...]))
    kk = k * kk_ref[...]
    inv = 1.0 / jnp.maximum(jnp.sqrt(_seg_sum(kk * kk, seg_ref[...])), 1e-12)
    kk = kk * _seg_bcast(inv, segt_ref[...])
    r_ref[...] = r
    v_ref[...] = v
    kf_ref[...] = k * (1.0 + (a - 1.0) * ka_ref[...])
    nkk_ref[...] = -kk
    kka_ref[...] = kk * a


def _rwkv_out_kernel(o_ref, r_ref, kf_ref, v_ref, gate_ref, x_ref, lw_ref, lb_ref, rk_ref, w_ref,
                     seg_ref, segt_ref, y_ref):
    seg = seg_ref[...]
    seg_t = segt_ref[...]
    o = o_ref[...]
    inv_e = 1.0 / (o.shape[1] // seg.shape[1])
    cen = o - _seg_bcast(_seg_sum(o, seg) * inv_e, seg_t)
    var = _seg_sum(cen * cen, seg) * inv_e
    gn = cen * _seg_bcast(lax.rsqrt(var + LNX_EPS), seg_t) * lw_ref[...] + lb_ref[...]
    bonus = _seg_bcast(_seg_sum(r_ref[...] * kf_ref[...] * rk_ref[...], seg), seg_t) * v_ref[...]
    g = gate_ref[...]
    y = ((gn + bonus) * (g * jax.nn.sigmoid(g))).astype(BF16)
    y_ref[...] = x_ref[...] + _nn(y, w_ref[...])


def _rwkv_layer(x, h_prev0, s0, prm, nh):
    (norm, mu, w_in, w0, w1, w2, a0, a1, a2, k_k, k_a, r_k, lnx_w, lnx_b, w_out) = prm
    bsz, t, d = x.shape
    e = d // nh
    m = bsz * t
    seg, seg_t = _seg_mats(d, nh)
    row = lambda z: z.reshape(1, -1).astype(F32)
    full = lambda arr: pl.BlockSpec(arr.shape, lambda i: (0,) * arr.ndim)
    x2 = x.reshape(m, d)
    if t == 1:
        tm = _pick_tile(m, (256, 128))
        tiles_per_seq = 0
        hp = h_prev0.astype(F32)
        hp_spec = pl.BlockSpec((tm, d), lambda i: (i, 0))
    else:
        tm = _pick_tile(t, (256, 128))
        tiles_per_seq = t // tm
        hp = h_prev0.astype(F32).reshape(bsz, 1, d)
        hp_spec = pl.BlockSpec((1, 1, d), lambda i: (i // tiles_per_seq, 0, 0))
    tile = pl.BlockSpec((tm, d), lambda i: (i, 0))
    consts = [row(norm), mu.astype(F32), w_in.astype(BF16), row(w0), w1.astype(BF16), w2.astype(BF16),
              row(a0), a1.astype(BF16), a2.astype(BF16), row(k_k), row(k_a), seg, seg_t]
    r, lw, kf, v, neg_kk, kk_a, gate = pl.pallas_call(
        functools.partial(_rwkv_proj_kernel, tiles_per_seq=tiles_per_seq),
        grid=(m // tm,),
        in_specs=[tile, hp_spec] + [full(c) for c in consts],
        out_specs=[tile] * 7,
        out_shape=[jax.ShapeDtypeStruct((m, d), F32)] * 7,
        scratch_shapes=[pltpu.VMEM((1, d), F32)],
        compiler_params=pltpu.CompilerParams(
            dimension_semantics=("arbitrary",), vmem_limit_bytes=VMEM_LIMIT),
        name="rwkv_proj",
    )(x2, hp, *consts)

    if s0 is None:
        seq = lambda z: z.reshape(bsz, t, d)
        o, zt = _wkv_chunked(seq(r), seq(lw), seq(kf), seq(v), seq(neg_kk), seq(kk_a))
        zt = zt.reshape(bsz, d // LANES, 2, e, 2, e)
        s_fin = jnp.stack([zt[:, :, 0, :, 0, :], zt[:, :, 1, :, 1, :]], axis=2).reshape(bsz, nh, e, e)
    else:
        o, s_fin = _wkv_step(s0, r, lw, kf, v, neg_kk, kk_a)

    tmo = _pick_tile(m, (256, 128))
    tile_o = pl.BlockSpec((tmo, d), lambda i: (i, 0))
    consts_o = [row(lnx_w), row(lnx_b), row(r_k), w_out.astype(BF16), seg, seg_t]
    x_new = pl.pallas_call(
        _rwkv_out_kernel,
        grid=(m // tmo,),
        in_specs=[tile_o] * 6 + [full(c) for c in consts_o],
        out_specs=tile_o,
        out_shape=jax.ShapeDtypeStruct((m, d), F32),
        compiler_params=pltpu.CompilerParams(
            dimension_semantics=("parallel",), vmem_limit_bytes=VMEM_LIMIT),
        name="rwkv_out",
    )(o.reshape(m, d), r, kf, v, gate, x2, *consts_o)
    h_last = _rmsnorm(x[:, -1, :], norm)
    return x_new.reshape(bsz, t, d), s_fin, h_last


def _split2(x):
    hi = x.astype(BF16)
    return hi, (x - hi.astype(F32)).astype(BF16)


def _seg_sum(x, seg):
    hi, lo = _split2(x)
    return _nn(hi, seg) + _nn(lo, seg)


def _seg_bcast(y, seg_t):
    hi, lo = _split2(y)
    return _nn(hi, seg_t) + _nn(lo, seg_t)


def _seg_mats(d, nh):
    lane_head = jnp.arange(d, dtype=jnp.int32) // (d // nh)
    seg = (lane_head[:, None] == jnp.arange(nh, dtype=jnp.int32)[None, :]).astype(BF16)
    return seg, seg.T


def _fox_proj_kernel(x_ref, g_ref, w_ref, wf_ref, bf_ref, qn_ref, kn_ref, seg_ref, segt_ref,
                     q_ref, k_ref, v_ref, gate_ref, lf_ref, *attn_refs, q_scale):
    x = x_ref[...]
    d = x.shape[1]
    hb = (x * lax.rsqrt(jnp.mean(x * x, axis=-1, keepdims=True) + RMS_EPS) * g_ref[...]).astype(BF16)
    seg = seg_ref[...]
    seg_t = segt_ref[...]
    inv_e = 1.0 / (d // seg.shape[1])

    def head_norm(z, gain):
        inv = lax.rsqrt(_seg_sum(z * z, seg) * inv_e + RMS_EPS)
        return z * _seg_bcast(inv, seg_t) * gain

    q = head_norm(_nn(hb, w_ref[:, 0:d]), qn_ref[...])
    k = head_norm(_nn(hb, w_ref[:, d:2 * d]), kn_ref[...])
    v = _nn(hb, w_ref[:, 2 * d:3 * d])
    q_ref[...] = q
    k_ref[...] = k
    v_ref[...] = v
    if attn_refs:
        qb_ref, kb_ref, vb_ref = attn_refs
        qb_ref[...] = (q * q_scale).astype(BF16)
        kb_ref[...] = k.astype(BF16)
        vb_ref[...] = v.astype(BF16)
    gate_ref[...] = _nn(hb, w_ref[:, 3 * d:4 * d])
    f = _nn(hb, wf_ref[...]) + bf_ref[...]
    lf_ref[...] = jnp.minimum(f, 0.0) - jnp.log(1.0 + jnp.exp(-jnp.abs(f)))


def _fox_project(x, norm, w_in, b_f, qn_g, kn_g, nh, attn_q_scale=None):
    bsz, t, d = x.shape
    m = bsz * t
    tm = _pick_tile(m, (256, 128))
    seg, seg_t = _seg_mats(d, nh)
    row = lambda z: z.reshape(1, -1).astype(F32)
    full = lambda a: pl.BlockSpec(a.shape, lambda i: (0,) * a.ndim)
    consts = [row(norm), w_in[:, :4 * d].astype(BF16), w_in[:, 4 * d:].astype(BF16), row(b_f),
              row(jnp.tile(qn_g, nh)), row(jnp.tile(kn_g, nh)), seg, seg_t]
    tile = pl.BlockSpec((tm, d), lambda i: (i, 0))
    n_attn = 0 if attn_q_scale is None else 3
    outs = pl.pallas_call(
        functools.partial(_fox_proj_kernel, q_scale=attn_q_scale),
        grid=(m // tm,),
        in_specs=[tile] + [full(a) for a in consts],
        out_specs=[tile] * 4 + [pl.BlockSpec((tm, nh), lambda i: (i, 0))] + [tile] * n_attn,
        out_shape=([jax.ShapeDtypeStruct((m, d), F32)] * 4 + [jax.ShapeDtypeStruct((m, nh), F32)]
                   + [jax.ShapeDtypeStruct((m, d), BF16)] * n_attn),
        compiler_params=pltpu.CompilerParams(
            dimension_semantics=("parallel",), vmem_limit_bytes=VMEM_LIMIT),
        name="fox_proj",
    )(x.reshape(m, d), *consts)
    q, k, v, gate, logf = outs[:5]
    seq = lambda z: z.reshape(bsz, t, d)
    return seq(q), seq(k), seq(v), gate, logf.reshape(bsz, t, nh), tuple(seq(z) for z in outs[5:])


def _gate_out_kernel(o_ref, gate_ref, x_ref, w_ref, y_ref):
    g = gate_ref[...]
    y = (o_ref[...] * (g * jax.nn.sigmoid(g))).astype(BF16)
    y_ref[...] = x_ref[...] + _nn(y, w_ref[...])


def _fox_finish(x, o, gate, w_out):
    bsz, t, d = x.shape
    m = bsz * t
    tm = _pick_tile(m, (512, 256, 128))
    tile = pl.BlockSpec((tm, d), lambda i: (i, 0))
    y = pl.pallas_call(
        _gate_out_kernel,
        grid=(m // tm,),
        in_specs=[tile, tile, tile, pl.BlockSpec((d, d), lambda i: (0, 0))],
        out_specs=tile,
        out_shape=jax.ShapeDtypeStruct((m, d), F32),
        compiler_params=pltpu.CompilerParams(
            dimension_semantics=("parallel",), vmem_limit_bytes=VMEM_LIMIT),
        name="gate_out",
    )(o.reshape(m, d), gate, x.reshape(m, d), w_out.astype(BF16))
    return y.reshape(bsz, t, d)


def kernel(x_prompt, x_sample, state_wkv, state_shift, cache_k, cache_v, cache_logf, page_table,
           norm_a, mu_a, w_in_a, w0_a, w1_a, w2_a, a0_a, a1_a, a2_a, kk_a, ka_a, rk_a, lnx_w_a, lnx_b_a, w_out_a,
           norm_b, w_in_b, bf_b, qn_b, kn_b, w_out_b):
    bsz, t, d = x_prompt.shape
    nb, ts, _ = x_sample.shape
    assert ts == 1, "the sample group carries one new token per sequence"
    nh, e = rk_a.shape[1], rk_a.shape[2]
    assert d == nh * e and 2 * e == LANES and t % WKV_CHUNK == 0
    n_layers_a = norm_a.shape[0]
    n_layers_b = norm_b.shape[0]
    depth = n_layers_a + n_layers_b
    n_pool, page = cache_k.shape[1], cache_k.shape[2]
    n_pages = page_table.shape[1]
    past = n_pages * page
    scale = e ** -0.5
    ckt = jnp.transpose(cache_k, (0, 1, 3, 4, 2))
    cvt = jnp.transpose(cache_v, (0, 1, 3, 4, 2))

    xp, xs = x_prompt, x_sample
    kp_l, vp_l, fp_l, sp_l, hp_l = [], [], [], [], []
    ks_l, vs_l, fs_l, ss_l, hs_l = [], [], [], [], []
    for i in range(depth):
        j = i // 2
        if i % 2 == 0:
            prm = (norm_a[j], mu_a[j], w_in_a[j], w0_a[j], w1_a[j], w2_a[j], a0_a[j], a1_a[j], a2_a[j],
                   kk_a[j], ka_a[j], rk_a[j], lnx_w_a[j], lnx_b_a[j], w_out_a[j])
            xp, s_p, l_p = _rwkv_layer(xp, jnp.zeros((bsz, d), xp.dtype), None, prm, nh)
            xs, s_s, l_s = _rwkv_layer(xs, state_shift[j], state_wkv[j], prm, nh)
            sp_l.append(s_p); hp_l.append(l_p); ss_l.append(s_s); hs_l.append(l_s)
        else:
            _, kp, vp, gp, lfp, attn_ops = _fox_project(xp, norm_b[j], w_in_b[j], bf_b[j], qn_b[j], kn_b[j], nh,
                                                        attn_q_scale=scale * LOG2E)
            cp = jnp.cumsum(lfp, axis=1)
            op = _fox_flash(*attn_ops, cp)
            xp = _fox_finish(xp, op, gp, w_out_b[j])

            qs, ksn, vsn, gs, lfs, _ = _fox_project(xs, norm_b[j], w_in_b[j], bf_b[j], qn_b[j], kn_b[j], nh)
            f_past = cache_logf[j][page_table].reshape(nb, past, nh).astype(F32)
            c_all = jnp.cumsum(jnp.concatenate([f_past, lfs], axis=1), axis=1)
            bias = c_all[:, past:, :] - c_all[:, :past, :]
            bias_t = bias.reshape(nb, n_pages, page, nh).transpose(0, 1, 3, 2)
            osm = _fox_decode(j, (qs * scale).reshape(nb, d), ksn.reshape(nb, d), vsn.reshape(nb, d),
                              ckt, cvt, bias_t, page_table)
            xs = _fox_finish(xs, osm.reshape(nb, 1, d), gs, w_out_b[j])
            hd4 = lambda z, n_, t_: z.reshape(n_, t_, nh, e)
            kp_l.append(hd4(kp, bsz, t)); vp_l.append(hd4(vp, bsz, t)); fp_l.append(lfp)
            ks_l.append(hd4(ksn, nb, ts)); vs_l.append(hd4(vsn, nb, ts)); fs_l.append(lfs)
    return (xp, xs,
            jnp.stack(kp_l), jnp.stack(vp_l), jnp.stack(fp_l), jnp.stack(sp_l), jnp.stack(hp_l),
            jnp.stack(ks_l), jnp.stack(vs_l), jnp.stack(fs_l), jnp.stack(ss_l), jnp.stack(hs_l))
```

```python
import functools

import jax
import jax.numpy as jnp
from jax import lax
from jax.experimental import pallas as pl
from jax.experimental.pallas import tpu as pltpu

F32 = jnp.float32
BF16 = jnp.bfloat16
HIGHEST = lax.Precision.HIGHEST

LANES = 128
RMS_EPS = 1e-6
LNX_EPS = 64e-5
NEG_BIG = -1e30
LOG2E = 1.4426950408889634
WKV_CHUNK = 64
VMEM_LIMIT = 48 * 1024 * 1024


def _nt(x, y, precision=None):
    return lax.dot_general(x, y, (((1,), (1,)), ((), ())), precision=precision,
                           preferred_element_type=F32)


def _tn(x, y, precision=None):
    return lax.dot_general(x, y, (((0,), (0,)), ((), ())), precision=precision,
                           preferred_element_type=F32)


def _nn(x, y, precision=None):
    return jnp.dot(x, y, precision=precision, preferred_element_type=F32)


def _pick_tile(n, candidates):
    for c in candidates:
        if n % c == 0:
            return c
    return n


def _wkv_chunk_kernel(r_ref, lw_ref, k_ref, v_ref, a_ref, b_ref, o_ref, zt_ref):
    c = pl.program_id(1)

    @pl.when(c == 0)
    def _():
        zt_ref[...] = jnp.zeros_like(zt_ref)

    L = r_ref.shape[1]
    n = 2 * L
    half = LANES // 2
    npair = r_ref.shape[2] // LANES

    ti = lax.broadcasted_iota(jnp.int32, (L, L), 0)
    tj = lax.broadcasted_iota(jnp.int32, (L, L), 1)
    tri = (ti >= tj).astype(BF16)
    head0 = lax.broadcasted_iota(jnp.int32, (L, LANES), 1) < half
    i = lax.broadcasted_iota(jnp.int32, (n, n), 0)
    j = lax.broadcasted_iota(jnp.int32, (n, n), 1)
    strict = i > j
    incl = i >= j
    eye = jnp.where(i == j, 1.0, 0.0)
    first = (i >> 1) == (j >> 1)
    levels = []
    lvl = 1
    while (2 << lvl) <= L:
        levels.append(((i >> (lvl + 1)) == (j >> (lvl + 1))) & ((i >> lvl) != (j >> lvl)))
        lvl += 1

    def stack(x):
        return jnp.concatenate([jnp.where(head0, x, 0.0), jnp.where(head0, 0.0, x)], axis=0)

    pairs = range(npair)
    sls = [slice(p * LANES, (p + 1) * LANES) for p in pairs]
    ar, bk, bk_h, vs, p_last = [], [], [], [], []
    for p in pairs:
        r, lw, k, v, a, b = (ref[0, :, sls[p]] for ref in (r_ref, lw_ref, k_ref, v_ref, a_ref, b_ref))
        lw1 = lw.astype(BF16)
        rem = lw - lw1.astype(F32)
        lw2 = rem.astype(BF16)
        lw3 = (rem - lw2.astype(F32)).astype(BF16)
        cum = _nn(tri, lw1) + (_nn(tri, lw2) + _nn(tri, lw3))
        p_inc = jnp.exp(cum)
        p_exc = jnp.exp(cum - lw)
        p_inv = jnp.exp(-cum)
        pl_ = p_inc[L - 1:L, :]
        bk_f = jnp.concatenate([stack(b * p_inv), stack(k * p_inv)], axis=0)
        ar.append(jnp.concatenate([stack(a * p_exc), stack(r * p_inc)], axis=0).astype(BF16))
        bk.append(bk_f.astype(BF16))
        bk_h.append((bk_f * pl_).astype(BF16))
        vs.append(stack(v).astype(BF16))
        p_last.append(pl_)

    g = [_nt(ar[p], bk[p]) for p in pairs]
    a_ab = [jnp.where(strict, g[p][:n, :n], 0.0) for p in pairs]
    ak_rk = [jnp.concatenate([jnp.where(strict, g[p][:n, n:], 0.0),
                              jnp.where(incl, g[p][n:, n:], 0.0)], axis=0).astype(BF16) for p in pairs]
    a_rb = [jnp.where(incl, g[p][n:, :n], 0.0).astype(BF16) for p in pairs]

    t_inv = [eye + jnp.where(first, a_ab[p], 0.0) for p in pairs]
    for off in levels:
        t_b = [t_inv[p].astype(BF16) for p in pairs]
        x = [_nn(t_b[p], jnp.where(off, a_ab[p], 0.0).astype(BF16)).astype(BF16) for p in pairs]
        t_inv = [t_inv[p] + _nn(x[p], t_b[p]) for p in pairs]
    t_b = [t_inv[p].astype(BF16) for p in pairs]

    zt = [zt_ref[0, p] for p in pairs]
    y = [_nt(ar[p], zt[p].astype(BF16)) + _nn(ak_rk[p], vs[p]) for p in pairs]
    u_b = [_nn(t_b[p], y[p][:n].astype(BF16)).astype(BF16) for p in pairs]
    for p in pairs:
        o_st = y[p][n:] + _nn(a_rb[p], u_b[p])
        o_ref[0, :, sls[p]] = o_st[:L] + o_st[L:]
        zt_ref[0, p] = zt[p] * p_last[p] + _tn(jnp.concatenate([u_b[p], vs[p]], axis=0), bk_h[p])


def _wkv_chunked(r, lw, k, v, a, b):
    bsz, t, d = r.shape
    npair = d // LANES
    L = WKV_CHUNK
    spec = pl.BlockSpec((1, L, d), lambda bi, c: (bi, c, 0))
    return pl.pallas_call(
        _wkv_chunk_kernel,
        grid=(bsz, t // L),
        in_specs=[spec] * 6,
        out_specs=[spec, pl.BlockSpec((1, npair, LANES, LANES), lambda bi, c: (bi, 0, 0, 0))],
        out_shape=[jax.ShapeDtypeStruct((bsz, t, d), F32),
                   jax.ShapeDtypeStruct((bsz, npair, LANES, LANES), F32)],
        compiler_params=pltpu.CompilerParams(
            dimension_semantics=("parallel", "arbitrary"), vmem_limit_bytes=VMEM_LIMIT),
        name="wkv_chunked",
    )(r, lw, k, v, a, b)


def _wkv_step_kernel(s_ref, r_ref, lw_ref, k_ref, v_ref, a_ref, b_ref, so_ref, o_ref):
    _, bb, nh, e, _ = s_ref.shape
    half = LANES // 2
    ii = lax.broadcasted_iota(jnp.int32, (e, LANES), 0)
    jj = lax.broadcasted_iota(jnp.int32, (e, LANES), 1)
    eye_lo = ii == jj
    eye_hi = ii + half == jj

    for i in range(bb):
        units = []
        for h in range(nh):
            p, par = divmod(h, 2)
            sl = slice(p * LANES, (p + 1) * LANES)
            r, lw, k, v, a, b = (ref[i:i + 1, sl] for ref in (r_ref, lw_ref, k_ref, v_ref, a_ref, b_ref))
            w = jnp.exp(lw)
            if par == 0:
                rh, wh, kh, ah, bh = (z[:, :e] for z in (r, w, k, a, b))
            else:
                rh, wh, kh, ah, bh = (pltpu.roll(z, half, 1)[:, :e] for z in (r, w, k, a, b))
            s = s_ref[0, i, h]
            v_col = jnp.sum(jnp.where(eye_hi if par else eye_lo, v, 0.0), axis=1, keepdims=True)
            sa = jnp.sum(s * ah, axis=1, keepdims=True)
            units.append((s, rh, wh, kh, bh, v_col, sa))
        o_cols = []
        for h, (s, rh, wh, kh, bh, v_col, sa) in enumerate(units):
            s_new = s * wh + sa * bh + v_col * kh
            so_ref[0, i, h] = s_new
            o_cols.append(jnp.sum(s_new * rh, axis=1, keepdims=True))
        for p in range(nh // 2):
            o_ref[i:i + 1, p * LANES:(p + 1) * LANES] = (
                jnp.sum(jnp.where(eye_lo, o_cols[2 * p], 0.0), axis=0, keepdims=True)
                + jnp.sum(jnp.where(eye_hi, o_cols[2 * p + 1], 0.0), axis=0, keepdims=True))


def _wkv_step(states, layer, r, lw, k, v, a, b):
    _, n, h, e, _ = states.shape
    d = h * e
    bb = _pick_tile(n, (8,))
    s_spec = pl.BlockSpec((1, bb, h, e, e), lambda i: (layer, i, 0, 0, 0))
    row_spec = pl.BlockSpec((bb, d), lambda i: (i, 0))
    states, o = pl.pallas_call(
        _wkv_step_kernel,
        grid=(n // bb,),
        in_specs=[s_spec] + [row_spec] * 6,
        out_specs=[s_spec, row_spec],
        out_shape=[jax.ShapeDtypeStruct(states.shape, F32), jax.ShapeDtypeStruct((n, d), F32)],
        input_output_aliases={0: 0},
        compiler_params=pltpu.CompilerParams(
            dimension_semantics=("parallel",), vmem_limit_bytes=VMEM_LIMIT),
        name="wkv_step",
    )(states, r, lw, k, v, a, b)
    return o, states


def _fox_flash_kernel(q_ref, k_ref, v_ref, cq_ref, ck_ref, o_ref, s_ref, m_ref, l_ref, acc_ref, *, tk):
    p = pl.program_id(1)
    qi = pl.program_id(2)
    tq = q_ref.shape[1]
    half = LANES // 2
    n_piece_rows = cq_ref.shape[2]
    nh = n_piece_rows // 3
    heads = range(2)

    q_head0 = lax.broadcasted_iota(jnp.int32, (tq, LANES), 1) < half
    k_head0 = lax.broadcasted_iota(jnp.int32, (tk, LANES), 1) < half

    prow = lax.broadcasted_iota(jnp.int32, (n_piece_rows, LANES), 0)
    plane = lax.broadcasted_iota(jnp.int32, (n_piece_rows, LANES), 1)
    piece, head = prow >> (nh.bit_length() - 1), prow & (nh - 1)
    base = jnp.where(head == 2 * p, half, jnp.where(head == 2 * p + 1, 0, -LANES))
    place_q = jnp.where(plane == base + piece, 1.0, 0.0).astype(BF16)
    place_k = jnp.where(plane == base + piece + 3, -1.0, 0.0).astype(BF16)
    slot = lax.broadcasted_iota(jnp.int32, (1, LANES), 1) & (half - 1)
    ones_q = jnp.where((slot >= 3) & (slot < 6), 1.0, 0.0)
    ones_k = jnp.where(slot < 3, 1.0, 0.0)

    q = q_ref[0]
    q_aug = (_nn(cq_ref[0], place_q) + ones_q).astype(BF16)
    q_ops = (jnp.where(q_head0, q, q_aug), jnp.where(q_head0, q_aug, q))

    m_ref[...] = jnp.full_like(m_ref, NEG_BIG)
    l_ref[...] = jnp.zeros_like(l_ref)
    acc_ref[...] = jnp.zeros_like(acc_ref)

    def rows(kj):
        return slice(kj * tk, (kj + 1) * tk)

    def scores(kj, slot_idx):
        k = k_ref[0, rows(kj), :]
        k_aug = (_nn(ck_ref[0, rows(kj), :], place_k) + ones_k).astype(BF16)
        k_ops = (jnp.where(k_head0, k, k_aug), jnp.where(k_head0, k_aug, k))
        for h in heads:
            s_ref[slot_idx, h] = _nt(q_ops[h], k_ops[h])

    def consume(kj, slot_idx, on_diagonal):
        v = v_ref[0, rows(kj), :]
        zero = jnp.zeros_like(v)
        v_own = (jnp.where(k_head0, v, zero), jnp.where(k_head0, zero, v))
        if on_diagonal:
            keep = (lax.broadcasted_iota(jnp.int32, (tq, tk), 0)
                    >= lax.broadcasted_iota(jnp.int32, (tq, tk), 1))
        alpha, pv = [], []
        for h in heads:
            s = s_ref[slot_idx, h]
            if on_diagonal:
                s = jnp.where(keep, s, NEG_BIG)
            m_prev = m_ref[h]
            m_new = jnp.maximum(m_prev, jnp.max(s, axis=1, keepdims=True))
            a = jnp.exp2(m_prev - m_new)
            pr = jnp.exp2(s - jnp.concatenate([m_new] * (tk // LANES), axis=1))
            l_ref[h] = a * l_ref[h] + jnp.sum(pr, axis=1, keepdims=True)
            m_ref[h] = m_new
            alpha.append(a)
            pv.append(_nn(pr.astype(BF16), v_own[h]))
        acc_ref[...] = jnp.where(q_head0, alpha[0], alpha[1]) * acc_ref[...] + (pv[0] + pv[1])

    def run(n_before):
        scores(0, 0)
        for kj in range(n_before):
            scores(kj + 1, (kj + 1) % 2)
            consume(kj, kj % 2, False)
        consume(n_before, n_before % 2, True)
        o_ref[0] = acc_ref[...] / jnp.where(q_head0, l_ref[0], l_ref[1])

    for n_before in range(k_ref.shape[1] // tk):
        pl.when(qi == n_before)(functools.partial(run, n_before))


def _fox_flash(q_b, k_b, v_b, c, tile=512):
    bsz, t, d = q_b.shape
    nh = c.shape[-1]
    assert nh & (nh - 1) == 0, "head count must be a power of two"
    npair = d // LANES
    tq = tk = _pick_tile(t, (tile, 256, 128))
    top16 = lambda z: lax.bitcast_convert_type(
        lax.bitcast_convert_type(z, jnp.uint32) & jnp.uint32(0xFFFF0000), F32)
    c2 = c * LOG2E
    hi = top16(c2)
    mid = top16(c2 - hi)
    pieces = jnp.concatenate([hi, mid, top16(c2 - hi - mid)], axis=-1).astype(BF16)
    q_map = lambda bi, p, qi: (bi, qi, p)
    seq_map = lambda bi, p, qi: (bi, 0, p)
    return pl.pallas_call(
        functools.partial(_fox_flash_kernel, tk=tk),
        grid=(bsz, npair, t // tq),
        in_specs=[pl.BlockSpec((1, tq, LANES), q_map),
                  pl.BlockSpec((1, t, LANES), seq_map),
                  pl.BlockSpec((1, t, LANES), seq_map),
                  pl.BlockSpec((1, tq, 3 * nh), lambda bi, p, qi: (bi, qi, 0)),
                  pl.BlockSpec((1, t, 3 * nh), lambda bi, p, qi: (bi, 0, 0))],
        out_specs=pl.BlockSpec((1, tq, LANES), q_map),
        out_shape=jax.ShapeDtypeStruct((bsz, t, d), F32),
        scratch_shapes=[pltpu.VMEM((2, 2, tq, tk), F32),
                        pltpu.VMEM((2, tq, LANES), F32), pltpu.VMEM((2, tq, LANES), F32),
                        pltpu.VMEM((tq, LANES), F32)],
        compiler_params=pltpu.CompilerParams(
            dimension_semantics=("parallel", "parallel", "parallel"), vmem_limit_bytes=VMEM_LIMIT),
        name="fox_flash",
    )(q_b, k_b, v_b, pieces, pieces)


def _fox_decode_kernel(pt_ref, q_ref, *refs, npg):
    k_refs = refs[:npg]
    v_refs = refs[npg:2 * npg]
    bias_ref, kn_ref, vn_ref, o_ref, m_ref, l_ref, acc_ref = refs[2 * npg:]
    pg = pl.program_id(1)

    @pl.when(pg == 0)
    def _():
        m_ref[...] = jnp.full_like(m_ref, NEG_BIG)
        l_ref[...] = jnp.zeros_like(l_ref)
        acc_ref[...] = jnp.zeros_like(acc_ref)

    nh, e, page = k_refs[0].shape[2:]
    d = nh * e
    row = lax.broadcasted_iota(jnp.int32, (nh, d), 0)
    lane = lax.broadcasted_iota(jnp.int32, (nh, d), 1)
    own = (lane >= row * e) & (lane < (row + 1) * e)
    q_bd = jnp.where(own, q_ref[0], 0.0)
    q_b = q_bd.astype(BF16)
    s = [_nn(q_b, k_refs[t][0, 0].reshape(d, page).astype(BF16)) + bias_ref[0, t] for t in range(npg)]
    m_prev = m_ref[...]
    m_new = m_prev
    for t in range(npg):
        m_new = jnp.maximum(m_new, jnp.max(s[t], axis=1, keepdims=True))
    alpha = jnp.exp(m_prev - m_new)
    l_new = alpha * l_ref[...]
    acc = alpha * acc_ref[...]
    for t in range(npg):
        pr = jnp.exp(s[t] - m_new)
        l_new = l_new + jnp.sum(pr, axis=1, keepdims=True)
        acc = acc + _nt(pr.astype(BF16), v_refs[t][0, 0].reshape(d, page).astype(BF16))
    l_ref[...] = l_new
    acc_ref[...] = acc
    m_ref[...] = m_new

    @pl.when(pg == pl.num_programs(1) - 1)
    def _():
        s_new = jnp.sum(q_bd * kn_ref[0], axis=1, keepdims=True)
        m_fin = jnp.maximum(m_new, s_new)
        a_old = jnp.exp(m_new - m_fin)
        p_new = jnp.exp(s_new - m_fin)
        l_fin = l_new * a_old + p_new
        full = (acc * a_old + p_new * vn_ref[0]) / l_fin
        o_ref[0] = jnp.sum(jnp.where(own, full, 0.0), axis=0, keepdims=True)


def _fox_decode(layer, q, k_new, v_new, cache_kt, cache_vt, bias, page_table):
    n, d = q.shape
    n_pages = page_table.shape[1]
    nh, e, page = cache_kt.shape[2:]
    npg = _pick_tile(n_pages, (8, 4, 2, 1))

    def kv_spec(t):
        return pl.BlockSpec((1, 1, nh, e, page), lambda b, pg, pt: (layer, pt[b, pg * npg + t], 0, 0, 0))

    row_spec = pl.BlockSpec((1, 1, d), lambda b, pg, pt: (b, 0, 0))
    grid_spec = pltpu.PrefetchScalarGridSpec(
        num_scalar_prefetch=1,
        grid=(n, n_pages // npg),
        in_specs=[row_spec] + [kv_spec(t) for t in range(npg)] * 2
                 + [pl.BlockSpec((1, npg, nh, page), lambda b, pg, pt: (b, pg, 0, 0)), row_spec, row_spec],
        out_specs=row_spec,
        scratch_shapes=[pltpu.VMEM((nh, 1), F32), pltpu.VMEM((nh, 1), F32), pltpu.VMEM((nh, d), F32)],
    )
    o = pl.pallas_call(
        functools.partial(_fox_decode_kernel, npg=npg),
        grid_spec=grid_spec,
        out_shape=jax.ShapeDtypeStruct((n, 1, d), F32),
        compiler_params=pltpu.CompilerParams(
            dimension_semantics=("parallel", "arbitrary"), vmem_limit_bytes=VMEM_LIMIT),
        name="fox_decode",
    )(page_table, q.reshape(n, 1, d), *([cache_kt] * npg), *([cache_vt] * npg),
      bias, k_new.reshape(n, 1, d), v_new.reshape(n, 1, d))
    return o.reshape(n, d)


def _rmsnorm(x, g):
    xf = x.astype(F32)
    y = xf * lax.rsqrt(jnp.mean(xf * xf, axis=-1, keepdims=True) + RMS_EPS)
    return (y * g.astype(F32)).astype(x.dtype)


def _rwkv_proj_kernel(x_ref, hp_ref, g_ref, mu_ref, w_ref, w0_ref, w1_ref, w2_ref, a0_ref, a1_ref, a2_ref,
                      kk_ref, ka_ref, seg_ref, segt_ref,
                      r_ref, lw_ref, kf_ref, v_ref, nkk_ref, kka_ref, gate_ref, carry_ref,
                      *, tiles_per_seq):
    x = x_ref[...]
    tm = x.shape[0]
    g = g_ref[...]
    h = x * lax.rsqrt(jnp.mean(x * x, axis=-1, keepdims=True) + RMS_EPS) * g
    if tiles_per_seq == 0:
        h_prev = hp_ref[...]
    else:
        first = pl.program_id(0) % tiles_per_seq == 0
        before = jnp.where(first, hp_ref[0], carry_ref[...])
        row = lax.broadcasted_iota(jnp.int32, x.shape, 0)
        h_prev = jnp.where(row == 0, before, pltpu.roll(h, 1, 0))
        carry_ref[...] = h[tm - 1:tm, :]
    xx = h_prev - h
    mix = lambda s: (h + xx * mu_ref[s:s + 1, :]).astype(BF16)
    r = _nn(mix(0), w_ref[0])
    k = _nn(mix(1), w_ref[1])
    v = _nn(mix(2), w_ref[2])
    gate_ref[...] = _nn(mix(3), w_ref[3])
    z = w0_ref[...] + _nn(jnp.tanh(_nn(mix(4), w1_ref[...])).astype(BF16), w2_ref[...])
    lw_ref[...] = -jnp.exp(jnp.minimum(z, 0.0) - jnp.log(1.0 + jnp.exp(-jnp.abs(z))) - 0.5)
    a = jax.nn.sigmoid(a0_ref[...] + _nn(_nn(mix(5), a1_ref[...]).astype(BF16), a2_ref[...]))
    kk = k * kk_ref[...]
    inv = 1.0 / jnp.maximum(jnp.sqrt(_seg_sum(kk * kk, seg_ref[...])), 1e-12)
    kk = kk * _seg_bcast(inv, segt_ref[...])
    r_ref[...] = r
    v_ref[...] = v
    kf_ref[...] = k * (1.0 + (a - 1.0) * ka_ref[...])
    nkk_ref[...] = -kk
    kka_ref[...] = kk * a


def _rwkv_out_kernel(o_ref, r_ref, kf_ref, v_ref, gate_ref, x_ref, lw_ref, lb_ref, rk_ref, w_ref,
                     seg_ref, segt_ref, y_ref):
    seg = seg_ref[...]
    seg_t = segt_ref[...]
    o = o_ref[...]
    inv_e = 1.0 / (o.shape[1] // seg.shape[1])
    cen = o - _seg_bcast(_seg_sum(o, seg) * inv_e, seg_t)
    var = _seg_sum(cen * cen, seg) * inv_e
    gn = cen * _seg_bcast(lax.rsqrt(var + LNX_EPS), seg_t) * lw_ref[...] + lb_ref[...]
    bonus = _seg_bcast(_seg_sum(r_ref[...] * kf_ref[...] * rk_ref[...], seg), seg_t) * v_ref[...]
    g = gate_ref[...]
    y = ((gn + bonus) * (g * jax.nn.sigmoid(g))).astype(BF16)
    y_ref[...] = x_ref[...] + _nn(y, w_ref[...])


def _rwkv_layer(x, h_prev0, states, layer, prm, nh):
    (norm, mu, w_in, w0, w1, w2, a0, a1, a2, k_k, k_a, r_k, lnx_w, lnx_b, w_out) = prm
    bsz, t, d = x.shape
    e = d // nh
    m = bsz * t
    seg, seg_t = _seg_mats(d, nh)
    row = lambda z: z.reshape(1, -1).astype(F32)
    full = lambda arr: pl.BlockSpec(arr.shape, lambda i: (0,) * arr.ndim)
    x2 = x.reshape(m, d)
    if t == 1:
        tm = _pick_tile(m, (256, 128))
        tiles_per_seq = 0
        hp = h_prev0.astype(F32)
        hp_spec = pl.BlockSpec((tm, d), lambda i: (i, 0))
    else:
        tm = _pick_tile(t, (256, 128))
        tiles_per_seq = t // tm
        hp = h_prev0.astype(F32).reshape(bsz, 1, d)
        hp_spec = pl.BlockSpec((1, 1, d), lambda i: (i // tiles_per_seq, 0, 0))
    tile = pl.BlockSpec((tm, d), lambda i: (i, 0))
    consts = [row(norm), mu.astype(F32), w_in.astype(BF16), row(w0), w1.astype(BF16), w2.astype(BF16),
              row(a0), a1.astype(BF16), a2.astype(BF16), row(k_k), row(k_a), seg, seg_t]
    r, lw, kf, v, neg_kk, kk_a, gate = pl.pallas_call(
        functools.partial(_rwkv_proj_kernel, tiles_per_seq=tiles_per_seq),
        grid=(m // tm,),
        in_specs=[tile, hp_spec] + [full(c) for c in consts],
        out_specs=[tile] * 7,
        out_shape=[jax.ShapeDtypeStruct((m, d), F32)] * 7,
        scratch_shapes=[pltpu.VMEM((1, d), F32)],
        compiler_params=pltpu.CompilerParams(
            dimension_semantics=("arbitrary",), vmem_limit_bytes=VMEM_LIMIT),
        name="rwkv_proj",
    )(x2, hp, *consts)

    if states is None:
        seq = lambda z: z.reshape(bsz, t, d)
        o, zt = _wkv_chunked(seq(r), seq(lw), seq(kf), seq(v), seq(neg_kk), seq(kk_a))
        zt = zt.reshape(bsz, d // LANES, 2, e, 2, e)
        s_fin = jnp.stack([zt[:, :, 0, :, 0, :], zt[:, :, 1, :, 1, :]], axis=2).reshape(bsz, nh, e, e)
    else:
        o, s_fin = _wkv_step(states, layer, r, lw, kf, v, neg_kk, kk_a)

    tmo = _pick_tile(m, (256, 128))
    tile_o = pl.BlockSpec((tmo, d), lambda i: (i, 0))
    consts_o = [row(lnx_w), row(lnx_b), row(r_k), w_out.astype(BF16), seg, seg_t]
    x_new = pl.pallas_call(
        _rwkv_out_kernel,
        grid=(m // tmo,),
        in_specs=[tile_o] * 6 + [full(c) for c in consts_o],
        out_specs=tile_o,
        out_shape=jax.ShapeDtypeStruct((m, d), F32),
        compiler_params=pltpu.CompilerParams(
            dimension_semantics=("parallel",), vmem_limit_bytes=VMEM_LIMIT),
        name="rwkv_out",
    )(o.reshape(m, d), r, kf, v, gate, x2, *consts_o)
    h_last = _rmsnorm(x[:, -1, :], norm)
    return x_new.reshape(bsz, t, d), s_fin, h_last


def _split2(x):
    hi = x.astype(BF16)
    return hi, (x - hi.astype(F32)).astype(BF16)


def _seg_sum(x, seg):
    hi, lo = _split2(x)
    return _nn(hi, seg) + _nn(lo, seg)


def _seg_bcast(y, seg_t):
    hi, lo = _split2(y)
    return _nn(hi, seg_t) + _nn(lo, seg_t)


def _seg_mats(d, nh):
    lane_head = jnp.arange(d, dtype=jnp.int32) // (d // nh)
    seg = (lane_head[:, None] == jnp.arange(nh, dtype=jnp.int32)[None, :]).astype(BF16)
    return seg, seg.T


def _fox_proj_kernel(x_ref, g_ref, w_ref, wf_ref, bf_ref, qn_ref, kn_ref, seg_ref, segt_ref,
                     q_ref, k_ref, v_ref, gate_ref, lf_ref, *attn_refs, q_scale):
    x = x_ref[...]
    d = x.shape[1]
    hb = (x * lax.rsqrt(jnp.mean(x * x, axis=-1, keepdims=True) + RMS_EPS) * g_ref[...]).astype(BF16)
    seg = seg_ref[...]
    seg_t = segt_ref[...]
    inv_e = 1.0 / (d // seg.shape[1])

    def head_norm(z, gain):
        inv = lax.rsqrt(_seg_sum(z * z, seg) * inv_e + RMS_EPS)
        return z * _seg_bcast(inv, seg_t) * gain

    q = head_norm(_nn(hb, w_ref[:, 0:d]), qn_ref[...])
    k = head_norm(_nn(hb, w_ref[:, d:2 * d]), kn_ref[...])
    v = _nn(hb, w_ref[:, 2 * d:3 * d])
    q_ref[...] = q
    k_ref[...] = k
    v_ref[...] = v
    if attn_refs:
        qb_ref, kb_ref, vb_ref = attn_refs
        qb_ref[...] = (q * q_scale).astype(BF16)
        kb_ref[...] = k.astype(BF16)
        vb_ref[...] = v.astype(BF16)
    gate_ref[...] = _nn(hb, w_ref[:, 3 * d:4 * d])
    f = _nn(hb, wf_ref[...]) + bf_ref[...]
    lf_ref[...] = jnp.minimum(f, 0.0) - jnp.log(1.0 + jnp.exp(-jnp.abs(f)))


def _fox_project(x, norm, w_in, b_f, qn_g, kn_g, nh, attn_q_scale=None):
    bsz, t, d = x.shape
    m = bsz * t
    tm = _pick_tile(m, (256, 128))
    seg, seg_t = _seg_mats(d, nh)
    row = lambda z: z.reshape(1, -1).astype(F32)
    full = lambda a: pl.BlockSpec(a.shape, lambda i: (0,) * a.ndim)
    consts = [row(norm), w_in[:, :4 * d].astype(BF16), w_in[:, 4 * d:].astype(BF16), row(b_f),
              row(jnp.tile(qn_g, nh)), row(jnp.tile(kn_g, nh)), seg, seg_t]
    tile = pl.BlockSpec((tm, d), lambda i: (i, 0))
    n_attn = 0 if attn_q_scale is None else 3
    outs = pl.pallas_call(
        functools.partial(_fox_proj_kernel, q_scale=attn_q_scale),
        grid=(m // tm,),
        in_specs=[tile] + [full(a) for a in consts],
        out_specs=[tile] * 4 + [pl.BlockSpec((tm, nh), lambda i: (i, 0))] + [tile] * n_attn,
        out_shape=([jax.ShapeDtypeStruct((m, d), F32)] * 4 + [jax.ShapeDtypeStruct((m, nh), F32)]
                   + [jax.ShapeDtypeStruct((m, d), BF16)] * n_attn),
        compiler_params=pltpu.CompilerParams(
            dimension_semantics=("parallel",), vmem_limit_bytes=VMEM_LIMIT),
        name="fox_proj",
    )(x.reshape(m, d), *consts)
    q, k, v, gate, logf = outs[:5]
    seq = lambda z: z.reshape(bsz, t, d)
    return seq(q), seq(k), seq(v), gate, logf.reshape(bsz, t, nh), tuple(seq(z) for z in outs[5:])


def _gate_out_kernel(o_ref, gate_ref, x_ref, w_ref, y_ref):
    g = gate_ref[...]
    y = (o_ref[...] * (g * jax.nn.sigmoid(g))).astype(BF16)
    y_ref[...] = x_ref[...] + _nn(y, w_ref[...])


def _fox_finish(x, o, gate, w_out):
    bsz, t, d = x.shape
    m = bsz * t
    tm = _pick_tile(m, (512, 256, 128))
    tile = pl.BlockSpec((tm, d), lambda i: (i, 0))
    y = pl.pallas_call(
        _gate_out_kernel,
        grid=(m // tm,),
        in_specs=[tile, tile, tile, pl.BlockSpec((d, d), lambda i: (0, 0))],
        out_specs=tile,
        out_shape=jax.ShapeDtypeStruct((m, d), F32),
        compiler_params=pltpu.CompilerParams(
            dimension_semantics=("parallel",), vmem_limit_bytes=VMEM_LIMIT),
        name="gate_out",
    )(o.reshape(m, d), gate, x.reshape(m, d), w_out.astype(BF16))
    return y.reshape(bsz, t, d)


def kernel(x_prompt, x_sample, state_wkv, state_shift, cache_k, cache_v, cache_logf, page_table,
           norm_a, mu_a, w_in_a, w0_a, w1_a, w2_a, a0_a, a1_a, a2_a, kk_a, ka_a, rk_a, lnx_w_a, lnx_b_a, w_out_a,
           norm_b, w_in_b, bf_b, qn_b, kn_b, w_out_b):
    bsz, t, d = x_prompt.shape
    nb, ts, _ = x_sample.shape
    assert ts == 1, "the sample group carries one new token per sequence"
    nh, e = rk_a.shape[1], rk_a.shape[2]
    assert d == nh * e and 2 * e == LANES and t % WKV_CHUNK == 0
    n_layers_a = norm_a.shape[0]
    n_layers_b = norm_b.shape[0]
    depth = n_layers_a + n_layers_b
    n_pool, page = cache_k.shape[1], cache_k.shape[2]
    n_pages = page_table.shape[1]
    past = n_pages * page
    scale = e ** -0.5
    ckt = jnp.transpose(cache_k, (0, 1, 3, 4, 2))
    cvt = jnp.transpose(cache_v, (0, 1, 3, 4, 2))

    xp, xs = x_prompt, x_sample
    kp_l, vp_l, fp_l, sp_l, hp_l = [], [], [], [], []
    ks_l, vs_l, fs_l, hs_l = [], [], [], []
    states_s = state_wkv.astype(F32)
    for i in range(depth):
        j = i // 2
        if i % 2 == 0:
            prm = (norm_a[j], mu_a[j], w_in_a[j], w0_a[j], w1_a[j], w2_a[j], a0_a[j], a1_a[j], a2_a[j],
                   kk_a[j], ka_a[j], rk_a[j], lnx_w_a[j], lnx_b_a[j], w_out_a[j])
            xp, s_p, l_p = _rwkv_layer(xp, jnp.zeros((bsz, d), xp.dtype), None, j, prm, nh)
            xs, states_s, l_s = _rwkv_layer(xs, state_shift[j], states_s, j, prm, nh)
            sp_l.append(s_p); hp_l.append(l_p); hs_l.append(l_s)
        else:
            _, kp, vp, gp, lfp, attn_ops = _fox_project(xp, norm_b[j], w_in_b[j], bf_b[j], qn_b[j], kn_b[j], nh,
                                                        attn_q_scale=scale * LOG2E)
            cp = jnp.cumsum(lfp, axis=1)
            op = _fox_flash(*attn_ops, cp)
            xp = _fox_finish(xp, op, gp, w_out_b[j])

            qs, ksn, vsn, gs, lfs, _ = _fox_project(xs, norm_b[j], w_in_b[j], bf_b[j], qn_b[j], kn_b[j], nh)
            f_past = cache_logf[j][page_table].reshape(nb, past, nh).astype(F32)
            c_all = jnp.cumsum(jnp.concatenate([f_past, lfs], axis=1), axis=1)
            bias = c_all[:, past:, :] - c_all[:, :past, :]
            bias_t = bias.reshape(nb, n_pages, page, nh).transpose(0, 1, 3, 2)
            osm = _fox_decode(j, (qs * scale).reshape(nb, d), ksn.reshape(nb, d), vsn.reshape(nb, d),
                              ckt, cvt, bias_t, page_table)
            xs = _fox_finish(xs, osm.reshape(nb, 1, d), gs, w_out_b[j])
            hd4 = lambda z, n_, t_: z.reshape(n_, t_, nh, e)
            kp_l.append(hd4(kp, bsz, t)); vp_l.append(hd4(vp, bsz, t)); fp_l.append(lfp)
            ks_l.append(hd4(ksn, nb, ts)); vs_l.append(hd4(vsn, nb, ts)); fs_l.append(lfs)
    return (xp, xs,
            jnp.stack(kp_l), jnp.stack(vp_l), jnp.stack(fp_l), jnp.stack(sp_l), jnp.stack(hp_l),
            jnp.stack(ks_l), jnp.stack(vs_l), jnp.stack(fs_l), states_s, jnp.stack(hs_l))
```

```python
import functools

import jax
import jax.numpy as jnp
from jax import lax
from jax.experimental import pallas as pl
from jax.experimental.pallas import tpu as pltpu

F32 = jnp.float32
BF16 = jnp.bfloat16

LANES = 128
RMS_EPS = 1e-6
LNX_EPS = 64e-5
NEG_BIG = -1e30
LOG2E = 1.4426950408889634
WKV_CHUNK = 64
WKV_CHUNKS_PER_STEP = (4, 2, 1)
VMEM_LIMIT = 48 * 1024 * 1024


def _nt(x, y, precision=None):
    return lax.dot_general(x, y, (((1,), (1,)), ((), ())), precision=precision,
                           preferred_element_type=F32)


def _tn(x, y, precision=None):
    return lax.dot_general(x, y, (((0,), (0,)), ((), ())), precision=precision,
                           preferred_element_type=F32)


def _nn(x, y, precision=None):
    return jnp.dot(x, y, precision=precision, preferred_element_type=F32)


def _pick_tile(n, candidates):
    for c in candidates:
        if n % c == 0:
            return c
    return n


def _wkv_chunk_kernel(r_ref, lw_ref, k_ref, v_ref, a_ref, b_ref, o_ref, zt_ref):
    c = pl.program_id(1)

    @pl.when(c == 0)
    def _():
        zt_ref[...] = jnp.zeros_like(zt_ref)

    L = WKV_CHUNK
    n_chunks = r_ref.shape[1] // L
    n = 2 * L
    half = LANES // 2
    npair = r_ref.shape[2] // LANES

    ti = lax.broadcasted_iota(jnp.int32, (L, L), 0)
    tj = lax.broadcasted_iota(jnp.int32, (L, L), 1)
    tri = (ti >= tj).astype(BF16)
    head0 = lax.broadcasted_iota(jnp.int32, (L, LANES), 1) < half
    i = lax.broadcasted_iota(jnp.int32, (n, n), 0)
    j = lax.broadcasted_iota(jnp.int32, (n, n), 1)
    strict = i > j
    incl = i >= j
    eye = jnp.where(i == j, 1.0, 0.0)
    first = (i >> 1) == (j >> 1)
    levels = []
    lvl = 1
    while (2 << lvl) <= L:
        levels.append(((i >> (lvl + 1)) == (j >> (lvl + 1))) & ((i >> lvl) != (j >> lvl)))
        lvl += 1

    def stack(x):
        return jnp.concatenate([jnp.where(head0, x, 0.0), jnp.where(head0, 0.0, x)], axis=0)

    pairs = range(npair)
    units = range(n_chunks * npair)
    sls = [(slice((q // npair) * L, (q // npair + 1) * L), slice((q % npair) * LANES, (q % npair + 1) * LANES))
           for q in units]
    ar, bk, bk_h, vs, p_last = [], [], [], [], []
    for q in units:
        r, lw, k, v, a, b = (ref[0, sls[q][0], sls[q][1]] for ref in (r_ref, lw_ref, k_ref, v_ref, a_ref, b_ref))
        lw1 = lw.astype(BF16)
        rem = lw - lw1.astype(F32)
        lw2 = rem.astype(BF16)
        lw3 = (rem - lw2.astype(F32)).astype(BF16)
        cum = _nn(tri, lw1) + (_nn(tri, lw2) + _nn(tri, lw3))
        p_inc = jnp.exp(cum)
        p_exc = jnp.exp(cum - lw)
        p_inv = jnp.exp(-cum)
        pl_ = p_inc[L - 1:L, :]
        bk_f = jnp.concatenate([stack(b * p_inv), stack(k * p_inv)], axis=0)
        ar.append(jnp.concatenate([stack(a * p_exc), stack(r * p_inc)], axis=0).astype(BF16))
        bk.append(bk_f.astype(BF16))
        bk_h.append((bk_f * pl_).astype(BF16))
        vs.append(stack(v).astype(BF16))
        p_last.append(pl_)

    g = [_nt(ar[q], bk[q]) for q in units]
    a_ab = [jnp.where(strict, g[q][:n, :n], 0.0) for q in units]
    ak_rk = [jnp.concatenate([jnp.where(strict, g[q][:n, n:], 0.0),
                              jnp.where(incl, g[q][n:, n:], 0.0)], axis=0).astype(BF16) for q in units]
    a_rb = [jnp.where(incl, g[q][n:, :n], 0.0).astype(BF16) for q in units]

    def compress(full, size):
        return functools.reduce(lambda x, y: x + y, [full[i * size:(i + 1) * size] for i in range(n // size)])

    def expand(comp, size):
        lane_blk = lax.broadcasted_iota(jnp.int32, (size, n), 1) >> (size.bit_length() - 1)
        return jnp.concatenate([jnp.where(lane_blk == i, comp, 0.0) for i in range(n // size)], axis=0)

    sub = L // 2
    t_full = [eye + jnp.where(first, a_ab[q], 0.0) for q in units]
    t_comp = [compress(t_full[q], sub) for q in units]
    for off in levels[:-1]:
        x = [_nn(t_comp[q].astype(BF16), jnp.where(off, a_ab[q], 0.0).astype(BF16)).astype(BF16) for q in units]
        t_comp = [t_comp[q] + _nn(x[q], t_full[q].astype(BF16)) for q in units]
        t_full = [expand(t_comp[q], sub) for q in units]
    t_comp = [compress(t_full[q], L) for q in units]
    x = [_nn(t_comp[q].astype(BF16), jnp.where(levels[-1], a_ab[q], 0.0).astype(BF16)).astype(BF16) for q in units]
    t_comp = [(t_comp[q] + _nn(x[q], t_full[q].astype(BF16))).astype(BF16) for q in units]

    zt = [zt_ref[0, p] for p in pairs]
    for c in range(n_chunks):
        qs = [c * npair + p for p in pairs]
        y = [_nt(ar[q], zt[p].astype(BF16)) + _nn(ak_rk[q], vs[q]) for p, q in zip(pairs, qs)]
        u_b = [stack(_nn(t_comp[q], y[p][:n].astype(BF16))).astype(BF16) for p, q in zip(pairs, qs)]
        for p, q in zip(pairs, qs):
            o_st = y[p][n:] + _nn(a_rb[q], u_b[p])
            o_ref[0, sls[q][0], sls[q][1]] = o_st[:L] + o_st[L:]
            zt[p] = zt[p] * p_last[q] + _tn(jnp.concatenate([u_b[p], vs[q]], axis=0), bk_h[q])
    for p in pairs:
        zt_ref[0, p] = zt[p]


def _wkv_chunked(r, lw, k, v, a, b):
    bsz, t, d = r.shape
    npair = d // LANES
    tb = _pick_tile(t, tuple(WKV_CHUNK * c for c in WKV_CHUNKS_PER_STEP))
    spec = pl.BlockSpec((1, tb, d), lambda bi, c: (bi, c, 0))
    return pl.pallas_call(
        _wkv_chunk_kernel,
        grid=(bsz, t // tb),
        in_specs=[spec] * 6,
        out_specs=[spec, pl.BlockSpec((1, npair, LANES, LANES), lambda bi, c: (bi, 0, 0, 0))],
        out_shape=[jax.ShapeDtypeStruct((bsz, t, d), F32),
                   jax.ShapeDtypeStruct((bsz, npair, LANES, LANES), F32)],
        compiler_params=pltpu.CompilerParams(
            dimension_semantics=("parallel", "arbitrary"), vmem_limit_bytes=VMEM_LIMIT),
        name="wkv_chunked",
    )(r, lw, k, v, a, b)


def _wkv_step_kernel(s_ref, r_ref, lw_ref, k_ref, v_ref, a_ref, b_ref, so_ref, o_ref):
    _, bb, nh, e, _ = s_ref.shape
    half = LANES // 2
    ii = lax.broadcasted_iota(jnp.int32, (e, LANES), 0)
    jj = lax.broadcasted_iota(jnp.int32, (e, LANES), 1)
    eye_lo = ii == jj
    eye_hi = ii + half == jj

    for i in range(bb):
        units = []
        for h in range(nh):
            p, par = divmod(h, 2)
            sl = slice(p * LANES, (p + 1) * LANES)
            r, lw, k, v, a, b = (ref[i:i + 1, sl] for ref in (r_ref, lw_ref, k_ref, v_ref, a_ref, b_ref))
            w = jnp.exp(lw)
            if par == 0:
                rh, wh, kh, ah, bh = (z[:, :e] for z in (r, w, k, a, b))
            else:
                rh, wh, kh, ah, bh = (pltpu.roll(z, half, 1)[:, :e] for z in (r, w, k, a, b))
            s = s_ref[0, i, h]
            v_col = jnp.sum(jnp.where(eye_hi if par else eye_lo, v, 0.0), axis=1, keepdims=True)
            sa = jnp.sum(s * ah, axis=1, keepdims=True)
            units.append((s, rh, wh, kh, bh, v_col, sa))
        o_cols = []
        for h, (s, rh, wh, kh, bh, v_col, sa) in enumerate(units):
            s_new = s * wh + sa * bh + v_col * kh
            so_ref[0, i, h] = s_new
            o_cols.append(jnp.sum(s_new * rh, axis=1, keepdims=True))
        for p in range(nh // 2):
            o_ref[i:i + 1, p * LANES:(p + 1) * LANES] = (
                jnp.sum(jnp.where(eye_lo, o_cols[2 * p], 0.0), axis=0, keepdims=True)
                + jnp.sum(jnp.where(eye_hi, o_cols[2 * p + 1], 0.0), axis=0, keepdims=True))


def _wkv_step(states, layer, r, lw, k, v, a, b):
    _, n, h, e, _ = states.shape
    d = h * e
    bb = _pick_tile(n, (8,))
    s_spec = pl.BlockSpec((1, bb, h, e, e), lambda i: (layer, i, 0, 0, 0))
    row_spec = pl.BlockSpec((bb, d), lambda i: (i, 0))
    states, o = pl.pallas_call(
        _wkv_step_kernel,
        grid=(n // bb,),
        in_specs=[s_spec] + [row_spec] * 6,
        out_specs=[s_spec, row_spec],
        out_shape=[jax.ShapeDtypeStruct(states.shape, F32), jax.ShapeDtypeStruct((n, d), F32)],
        input_output_aliases={0: 0},
        compiler_params=pltpu.CompilerParams(
            dimension_semantics=("parallel",), vmem_limit_bytes=VMEM_LIMIT),
        name="wkv_step",
    )(states, r, lw, k, v, a, b)
    return o, states


def _fox_flash_kernel(q_ref, k_ref, v_ref, cq_ref, ck_ref, o_ref, s_ref, m_ref, l_ref, acc_ref, *, tk):
    p = pl.program_id(1)
    qi = pl.program_id(2)
    tq = q_ref.shape[1]
    half = LANES // 2
    n_piece_rows = cq_ref.shape[2]
    nh = n_piece_rows // 3
    heads = range(2)

    q_head0 = lax.broadcasted_iota(jnp.int32, (tq, LANES), 1) < half
    k_head0 = lax.broadcasted_iota(jnp.int32, (tk, LANES), 1) < half

    prow = lax.broadcasted_iota(jnp.int32, (n_piece_rows, LANES), 0)
    plane = lax.broadcasted_iota(jnp.int32, (n_piece_rows, LANES), 1)
    piece, head = prow >> (nh.bit_length() - 1), prow & (nh - 1)
    base = jnp.where(head == 2 * p, half, jnp.where(head == 2 * p + 1, 0, -LANES))
    place_q = jnp.where(plane == base + piece, 1.0, 0.0).astype(BF16)
    place_k = jnp.where(plane == base + piece + 3, -1.0, 0.0).astype(BF16)
    slot = lax.broadcasted_iota(jnp.int32, (1, LANES), 1) & (half - 1)
    ones_q = jnp.where((slot >= 3) & (slot < 6), 1.0, 0.0)
    ones_k = jnp.where(slot < 3, 1.0, 0.0)

    q = q_ref[0]
    q_aug = (_nn(cq_ref[0], place_q) + ones_q).astype(BF16)
    q_ops = (jnp.where(q_head0, q, q_aug), jnp.where(q_head0, q_aug, q))

    m_ref[...] = jnp.full_like(m_ref, NEG_BIG)
    l_ref[...] = jnp.zeros_like(l_ref)
    acc_ref[...] = jnp.zeros_like(acc_ref)

    def rows(kj):
        return slice(kj * tk, (kj + 1) * tk)

    def scores(kj, slot_idx):
        k = k_ref[0, rows(kj), :]
        k_aug = (_nn(ck_ref[0, rows(kj), :], place_k) + ones_k).astype(BF16)
        k_ops = (jnp.where(k_head0, k, k_aug), jnp.where(k_head0, k_aug, k))
        for h in heads:
            s_ref[slot_idx, h] = _nt(q_ops[h], k_ops[h])

    def consume(kj, slot_idx, on_diagonal):
        v = v_ref[0, rows(kj), :]
        zero = jnp.zeros_like(v)
        v_own = (jnp.where(k_head0, v, zero), jnp.where(k_head0, zero, v))
        if on_diagonal:
            keep = (lax.broadcasted_iota(jnp.int32, (tq, tk), 0)
                    >= lax.broadcasted_iota(jnp.int32, (tq, tk), 1))
        alpha, pv = [], []
        for h in heads:
            s = s_ref[slot_idx, h]
            if on_diagonal:
                s = jnp.where(keep, s, NEG_BIG)
            m_prev = m_ref[h]
            m_new = jnp.maximum(m_prev, jnp.max(s, axis=1, keepdims=True))
            a = jnp.exp2(m_prev - m_new)
            pr = jnp.exp2(s - jnp.concatenate([m_new] * (tk // LANES), axis=1))
            l_ref[h] = a * l_ref[h] + jnp.sum(pr, axis=1, keepdims=True)
            m_ref[h] = m_new
            alpha.append(a)
            pv.append(_nn(pr.astype(BF16), v_own[h]))
        acc_ref[...] = jnp.where(q_head0, alpha[0], alpha[1]) * acc_ref[...] + (pv[0] + pv[1])

    def run(n_before):
        scores(0, 0)
        for kj in range(n_before):
            scores(kj + 1, (kj + 1) % 2)
            consume(kj, kj % 2, False)
        consume(n_before, n_before % 2, True)
        o_ref[0] = acc_ref[...] / jnp.where(q_head0, l_ref[0], l_ref[1])

    for n_before in range(k_ref.shape[1] // tk):
        pl.when(qi == n_before)(functools.partial(run, n_before))


def _fox_flash(q_b, k_b, v_b, c, tile=512):
    bsz, t, d = q_b.shape
    nh = c.shape[-1]
    assert nh & (nh - 1) == 0, "head count must be a power of two"
    npair = d // LANES
    tq = tk = _pick_tile(t, (tile, 256, 128))
    top16 = lambda z: lax.bitcast_convert_type(
        lax.bitcast_convert_type(z, jnp.uint32) & jnp.uint32(0xFFFF0000), F32)
    c2 = c * LOG2E
    hi = top16(c2)
    mid = top16(c2 - hi)
    pieces = jnp.concatenate([hi, mid, top16(c2 - hi - mid)], axis=-1).astype(BF16)
    q_map = lambda bi, p, qi: (bi, qi, p)
    seq_map = lambda bi, p, qi: (bi, 0, p)
    return pl.pallas_call(
        functools.partial(_fox_flash_kernel, tk=tk),
        grid=(bsz, npair, t // tq),
        in_specs=[pl.BlockSpec((1, tq, LANES), q_map),
                  pl.BlockSpec((1, t, LANES), seq_map),
                  pl.BlockSpec((1, t, LANES), seq_map),
                  pl.BlockSpec((1, tq, 3 * nh), lambda bi, p, qi: (bi, qi, 0)),
                  pl.BlockSpec((1, t, 3 * nh), lambda bi, p, qi: (bi, 0, 0))],
        out_specs=pl.BlockSpec((1, tq, LANES), q_map),
        out_shape=jax.ShapeDtypeStruct((bsz, t, d), F32),
        scratch_shapes=[pltpu.VMEM((2, 2, tq, tk), F32),
                        pltpu.VMEM((2, tq, LANES), F32), pltpu.VMEM((2, tq, LANES), F32),
                        pltpu.VMEM((tq, LANES), F32)],
        compiler_params=pltpu.CompilerParams(
            dimension_semantics=("parallel", "parallel", "parallel"), vmem_limit_bytes=VMEM_LIMIT),
        name="fox_flash",
    )(q_b, k_b, v_b, pieces, pieces)


def _fox_decode_kernel(pt_ref, q_ref, *refs, npg):
    k_refs = refs[:npg]
    v_refs = refs[npg:2 * npg]
    bias_ref, kn_ref, vn_ref, o_ref, m_ref, l_ref, acc_ref = refs[2 * npg:]
    pg = pl.program_id(1)

    @pl.when(pg == 0)
    def _():
        m_ref[...] = jnp.full_like(m_ref, NEG_BIG)
        l_ref[...] = jnp.zeros_like(l_ref)
        acc_ref[...] = jnp.zeros_like(acc_ref)

    nh, e, page = k_refs[0].shape[2:]
    d = nh * e
    row = lax.broadcasted_iota(jnp.int32, (nh, d), 0)
    lane = lax.broadcasted_iota(jnp.int32, (nh, d), 1)
    own = (lane >= row * e) & (lane < (row + 1) * e)
    q_bd = jnp.where(own, q_ref[0], 0.0)
    q_b = q_bd.astype(BF16)
    s = [_nn(q_b, k_refs[t][0, 0].reshape(d, page).astype(BF16)) + bias_ref[0, t] for t in range(npg)]
    m_prev = m_ref[...]
    m_new = m_prev
    for t in range(npg):
        m_new = jnp.maximum(m_new, jnp.max(s[t], axis=1, keepdims=True))
    alpha = jnp.exp(m_prev - m_new)
    l_new = alpha * l_ref[...]
    acc = alpha * acc_ref[...]
    for t in range(npg):
        pr = jnp.exp(s[t] - m_new)
        l_new = l_new + jnp.sum(pr, axis=1, keepdims=True)
        acc = acc + _nt(pr.astype(BF16), v_refs[t][0, 0].reshape(d, page).astype(BF16))
    l_ref[...] = l_new
    acc_ref[...] = acc
    m_ref[...] = m_new

    @pl.when(pg == pl.num_programs(1) - 1)
    def _():
        s_new = jnp.sum(q_bd * kn_ref[0], axis=1, keepdims=True)
        m_fin = jnp.maximum(m_new, s_new)
        a_old = jnp.exp(m_new - m_fin)
        p_new = jnp.exp(s_new - m_fin)
        l_fin = l_new * a_old + p_new
        full = (acc * a_old + p_new * vn_ref[0]) / l_fin
        o_ref[0] = jnp.sum(jnp.where(own, full, 0.0), axis=0, keepdims=True)


def _fox_decode(layer, q, k_new, v_new, cache_kt, cache_vt, bias, page_table):
    n, d = q.shape
    n_pages = page_table.shape[1]
    nh, e, page = cache_kt.shape[2:]
    npg = _pick_tile(n_pages, (8, 4, 2, 1))

    def kv_spec(t):
        return pl.BlockSpec((1, 1, nh, e, page), lambda b, pg, pt: (layer, pt[b, pg * npg + t], 0, 0, 0))

    row_spec = pl.BlockSpec((1, 1, d), lambda b, pg, pt: (b, 0, 0))
    grid_spec = pltpu.PrefetchScalarGridSpec(
        num_scalar_prefetch=1,
        grid=(n, n_pages // npg),
        in_specs=[row_spec] + [kv_spec(t) for t in range(npg)] * 2
                 + [pl.BlockSpec((1, npg, nh, page), lambda b, pg, pt: (b, pg, 0, 0)), row_spec, row_spec],
        out_specs=row_spec,
        scratch_shapes=[pltpu.VMEM((nh, 1), F32), pltpu.VMEM((nh, 1), F32), pltpu.VMEM((nh, d), F32)],
    )
    o = pl.pallas_call(
        functools.partial(_fox_decode_kernel, npg=npg),
        grid_spec=grid_spec,
        out_shape=jax.ShapeDtypeStruct((n, 1, d), F32),
        compiler_params=pltpu.CompilerParams(
            dimension_semantics=("parallel", "arbitrary"), vmem_limit_bytes=VMEM_LIMIT),
        name="fox_decode",
    )(page_table, q.reshape(n, 1, d), *([cache_kt] * npg), *([cache_vt] * npg),
      bias, k_new.reshape(n, 1, d), v_new.reshape(n, 1, d))
    return o.reshape(n, d)


def _rmsnorm(x, g):
    xf = x.astype(F32)
    y = xf * lax.rsqrt(jnp.mean(xf * xf, axis=-1, keepdims=True) + RMS_EPS)
    return (y * g.astype(F32)).astype(x.dtype)


def _rwkv_proj_kernel(x_ref, hp_ref, g_ref, mu_ref, w_ref, w0_ref, w1_ref, w2_ref, a0_ref, a1_ref, a2_ref,
                      kk_ref, ka_ref, seg_ref, segt_ref,
                      r_ref, lw_ref, kf_ref, v_ref, nkk_ref, kka_ref, gate_ref, carry_ref,
                      *, tiles_per_seq):
    x = x_ref[...]
    tm = x.shape[0]
    g = g_ref[...]
    h = x * lax.rsqrt(jnp.mean(x * x, axis=-1, keepdims=True) + RMS_EPS) * g
    if tiles_per_seq == 0:
        h_prev = hp_ref[...]
    else:
        first = pl.program_id(0) % tiles_per_seq == 0
        before = jnp.where(first, hp_ref[0], carry_ref[...])
        row = lax.broadcasted_iota(jnp.int32, x.shape, 0)
        h_prev = jnp.where(row == 0, before, pltpu.roll(h, 1, 0))
        carry_ref[...] = h[tm - 1:tm, :]
    xx = h_prev - h
    mix = lambda s: (h + xx * mu_ref[s:s + 1, :]).astype(BF16)
    r = _nn(mix(0), w_ref[0])
    k = _nn(mix(1), w_ref[1])
    v = _nn(mix(2), w_ref[2])
    gate_ref[...] = _nn(mix(3), w_ref[3])
    z = w0_ref[...] + _nn(jnp.tanh(_nn(mix(4), w1_ref[...])).astype(BF16), w2_ref[...])
    lw_ref[...] = -jnp.exp(jnp.minimum(z, 0.0) - jnp.log(1.0 + jnp.exp(-jnp.abs(z))) - 0.5)
    a = jax.nn.sigmoid(a0_ref[...] + _nn(_nn(mix(5), a1_ref[...]).astype(BF16), a2_ref[...]))
    kk = k * kk_ref[...]
    inv = 1.0 / jnp.maximum(jnp.sqrt(_seg_sum(kk * kk, seg_ref[...])), 1e-12)
    kk = kk * _seg_bcast(inv, segt_ref[...])
    r_ref[...] = r
    v_ref[...] = v
    kf_ref[...] = k * (1.0 + (a - 1.0) * ka_ref[...])
    nkk_ref[...] = -kk
    kka_ref[...] = kk * a


def _rwkv_out_kernel(o_ref, r_ref, kf_ref, v_ref, gate_ref, x_ref, lw_ref, lb_ref, rk_ref, w_ref,
                     seg_ref, segt_ref, y_ref):
    seg = seg_ref[...]
    seg_t = segt_ref[...]
    o = o_ref[...]
    inv_e = 1.0 / (o.shape[1] // seg.shape[1])
    cen = o - _seg_bcast(_seg_sum(o, seg) * inv_e, seg_t)
    var = _seg_sum(cen * cen, seg) * inv_e
    gn = cen * _seg_bcast(lax.rsqrt(var + LNX_EPS), seg_t) * lw_ref[...] + lb_ref[...]
    bonus = _seg_bcast(_seg_sum(r_ref[...] * kf_ref[...] * rk_ref[...], seg), seg_t) * v_ref[...]
    g = gate_ref[...]
    y = ((gn + bonus) * (g * jax.nn.sigmoid(g))).astype(BF16)
    y_ref[...] = x_ref[...] + _nn(y, w_ref[...])


def _rwkv_layer(x, h_prev0, states, layer, prm, nh):
    (norm, mu, w_in, w0, w1, w2, a0, a1, a2, k_k, k_a, r_k, lnx_w, lnx_b, w_out) = prm
    bsz, t, d = x.shape
    e = d // nh
    m = bsz * t
    seg, seg_t = _seg_mats(d, nh)
    row = lambda z: z.reshape(1, -1).astype(F32)
    full = lambda arr: pl.BlockSpec(arr.shape, lambda i: (0,) * arr.ndim)
    x2 = x.reshape(m, d)
    if t == 1:
        tm = _pick_tile(m, (256, 128))
        tiles_per_seq = 0
        hp = h_prev0.astype(F32)
        hp_spec = pl.BlockSpec((tm, d), lambda i: (i, 0))
    else:
        tm = _pick_tile(t, (256, 128))
        tiles_per_seq = t // tm
        hp = h_prev0.astype(F32).reshape(bsz, 1, d)
        hp_spec = pl.BlockSpec((1, 1, d), lambda i: (i // tiles_per_seq, 0, 0))
    tile = pl.BlockSpec((tm, d), lambda i: (i, 0))
    consts = [row(norm), mu.astype(F32), w_in.astype(BF16), row(w0), w1.astype(BF16), w2.astype(BF16),
              row(a0), a1.astype(BF16), a2.astype(BF16), row(k_k), row(k_a), seg, seg_t]
    r, lw, kf, v, neg_kk, kk_a, gate = pl.pallas_call(
        functools.partial(_rwkv_proj_kernel, tiles_per_seq=tiles_per_seq),
        grid=(m // tm,),
        in_specs=[tile, hp_spec] + [full(c) for c in consts],
        out_specs=[tile] * 7,
        out_shape=[jax.ShapeDtypeStruct((m, d), F32)] * 7,
        scratch_shapes=[pltpu.VMEM((1, d), F32)],
        compiler_params=pltpu.CompilerParams(
            dimension_semantics=("arbitrary",), vmem_limit_bytes=VMEM_LIMIT),
        name="rwkv_proj",
    )(x2, hp, *consts)

    if states is None:
        seq = lambda z: z.reshape(bsz, t, d)
        o, zt = _wkv_chunked(seq(r), seq(lw), seq(kf), seq(v), seq(neg_kk), seq(kk_a))
        zt = zt.reshape(bsz, d // LANES, 2, e, 2, e)
        s_fin = jnp.stack([zt[:, :, 0, :, 0, :], zt[:, :, 1, :, 1, :]], axis=2).reshape(bsz, nh, e, e)
    else:
        o, s_fin = _wkv_step(states, layer, r, lw, kf, v, neg_kk, kk_a)

    tmo = _pick_tile(m, (256, 128))
    tile_o = pl.BlockSpec((tmo, d), lambda i: (i, 0))
    consts_o = [row(lnx_w), row(lnx_b), row(r_k), w_out.astype(BF16), seg, seg_t]
    x_new = pl.pallas_call(
        _rwkv_out_kernel,
        grid=(m // tmo,),
        in_specs=[tile_o] * 6 + [full(c) for c in consts_o],
        out_specs=tile_o,
        out_shape=jax.ShapeDtypeStruct((m, d), F32),
        compiler_params=pltpu.CompilerParams(
            dimension_semantics=("parallel",), vmem_limit_bytes=VMEM_LIMIT),
        name="rwkv_out",
    )(o.reshape(m, d), r, kf, v, gate, x2, *consts_o)
    h_last = _rmsnorm(x[:, -1, :], norm)
    return x_new.reshape(bsz, t, d), s_fin, h_last


def _split2(x):
    hi = x.astype(BF16)
    return hi, (x - hi.astype(F32)).astype(BF16)


def _seg_sum(x, seg):
    hi, lo = _split2(x)
    return _nn(hi, seg) + _nn(lo, seg)


def _seg_bcast(y, seg_t):
    hi, lo = _split2(y)
    return _nn(hi, seg_t) + _nn(lo, seg_t)


def _seg_mats(d, nh):
    lane_head = jnp.arange(d, dtype=jnp.int32) // (d // nh)
    seg = (lane_head[:, None] == jnp.arange(nh, dtype=jnp.int32)[None, :]).astype(BF16)
    return seg, seg.T


def _fox_proj_kernel(x_ref, g_ref, w_ref, wf_ref, bf_ref, qn_ref, kn_ref, seg_ref, segt_ref,
                     q_ref, k_ref, v_ref, gate_ref, lf_ref, *attn_refs, q_scale):
    x = x_ref[...]
    d = x.shape[1]
    hb = (x * lax.rsqrt(jnp.mean(x * x, axis=-1, keepdims=True) + RMS_EPS) * g_ref[...]).astype(BF16)
    seg = seg_ref[...]
    seg_t = segt_ref[...]
    inv_e = 1.0 / (d // seg.shape[1])

    def head_norm(z, gain):
        inv = lax.rsqrt(_seg_sum(z * z, seg) * inv_e + RMS_EPS)
        return z * _seg_bcast(inv, seg_t) * gain

    q = head_norm(_nn(hb, w_ref[:, 0:d]), qn_ref[...])
    k = head_norm(_nn(hb, w_ref[:, d:2 * d]), kn_ref[...])
    v = _nn(hb, w_ref[:, 2 * d:3 * d])
    k_ref[...] = k
    v_ref[...] = v
    if attn_refs:
        kb_ref, vb_ref = attn_refs
        q_ref[...] = (q * q_scale).astype(BF16)
        kb_ref[...] = k.astype(BF16)
        vb_ref[...] = v.astype(BF16)
    else:
        q_ref[...] = q
    gate_ref[...] = _nn(hb, w_ref[:, 3 * d:4 * d])
    f = _nn(hb, wf_ref[...]) + bf_ref[...]
    lf_ref[...] = jnp.minimum(f, 0.0) - jnp.log(1.0 + jnp.exp(-jnp.abs(f)))


def _fox_project(x, norm, w_in, b_f, qn_g, kn_g, nh, attn_q_scale=None):
    bsz, t, d = x.shape
    m = bsz * t
    tm = _pick_tile(m, (256, 128))
    seg, seg_t = _seg_mats(d, nh)
    row = lambda z: z.reshape(1, -1).astype(F32)
    full = lambda a: pl.BlockSpec(a.shape, lambda i: (0,) * a.ndim)
    consts = [row(norm), w_in[:, :4 * d].astype(BF16), w_in[:, 4 * d:].astype(BF16), row(b_f),
              row(jnp.tile(qn_g, nh)), row(jnp.tile(kn_g, nh)), seg, seg_t]
    tile = pl.BlockSpec((tm, d), lambda i: (i, 0))
    n_attn = 0 if attn_q_scale is None else 2
    q_dtype = F32 if attn_q_scale is None else BF16
    outs = pl.pallas_call(
        functools.partial(_fox_proj_kernel, q_scale=attn_q_scale),
        grid=(m // tm,),
        in_specs=[tile] + [full(a) for a in consts],
        out_specs=[tile] * 4 + [pl.BlockSpec((tm, nh), lambda i: (i, 0))] + [tile] * n_attn,
        out_shape=([jax.ShapeDtypeStruct((m, d), q_dtype)] + [jax.ShapeDtypeStruct((m, d), F32)] * 3
                   + [jax.ShapeDtypeStruct((m, nh), F32)] + [jax.ShapeDtypeStruct((m, d), BF16)] * n_attn),
        compiler_params=pltpu.CompilerParams(
            dimension_semantics=("parallel",), vmem_limit_bytes=VMEM_LIMIT),
        name="fox_proj",
    )(x.reshape(m, d), *consts)
    q, k, v, gate, logf = outs[:5]
    seq = lambda z: z.reshape(bsz, t, d)
    return seq(q), seq(k), seq(v), gate, logf.reshape(bsz, t, nh), tuple(seq(z) for z in outs[5:])


def _gate_out_kernel(o_ref, gate_ref, x_ref, w_ref, y_ref):
    g = gate_ref[...]
    y = (o_ref[...] * (g * jax.nn.sigmoid(g))).astype(BF16)
    y_ref[...] = x_ref[...] + _nn(y, w_ref[...])


def _fox_finish(x, o, gate, w_out):
    bsz, t, d = x.shape
    m = bsz * t
    tm = _pick_tile(m, (512, 256, 128))
    tile = pl.BlockSpec((tm, d), lambda i: (i, 0))
    y = pl.pallas_call(
        _gate_out_kernel,
        grid=(m // tm,),
        in_specs=[tile, tile, tile, pl.BlockSpec((d, d), lambda i: (0, 0))],
        out_specs=tile,
        out_shape=jax.ShapeDtypeStruct((m, d), F32),
        compiler_params=pltpu.CompilerParams(
            dimension_semantics=("parallel",), vmem_limit_bytes=VMEM_LIMIT),
        name="gate_out",
    )(o.reshape(m, d), gate, x.reshape(m, d), w_out.astype(BF16))
    return y.reshape(bsz, t, d)


def kernel(x_prompt, x_sample, state_wkv, state_shift, cache_k, cache_v, cache_logf, page_table,
           norm_a, mu_a, w_in_a, w0_a, w1_a, w2_a, a0_a, a1_a, a2_a, kk_a, ka_a, rk_a, lnx_w_a, lnx_b_a, w_out_a,
           norm_b, w_in_b, bf_b, qn_b, kn_b, w_out_b):
    bsz, t, d = x_prompt.shape
    nb, ts, _ = x_sample.shape
    assert ts == 1, "the sample group carries one new token per sequence"
    nh, e = rk_a.shape[1], rk_a.shape[2]
    assert d == nh * e and 2 * e == LANES and t % WKV_CHUNK == 0
    n_layers_a = norm_a.shape[0]
    n_layers_b = norm_b.shape[0]
    depth = n_layers_a + n_layers_b
    n_pool, page = cache_k.shape[1], cache_k.shape[2]
    n_pages = page_table.shape[1]
    past = n_pages * page
    scale = e ** -0.5
    ckt = jnp.transpose(cache_k, (0, 1, 3, 4, 2))
    cvt = jnp.transpose(cache_v, (0, 1, 3, 4, 2))

    xp, xs = x_prompt, x_sample
    kp_l, vp_l, fp_l, sp_l, hp_l = [], [], [], [], []
    ks_l, vs_l, fs_l, hs_l = [], [], [], []
    states_s = state_wkv.astype(F32)
    for i in range(depth):
        j = i // 2
        if i % 2 == 0:
            prm = (norm_a[j], mu_a[j], w_in_a[j], w0_a[j], w1_a[j], w2_a[j], a0_a[j], a1_a[j], a2_a[j],
                   kk_a[j], ka_a[j], rk_a[j], lnx_w_a[j], lnx_b_a[j], w_out_a[j])
            xp, s_p, l_p = _rwkv_layer(xp, jnp.zeros((bsz, d), xp.dtype), None, j, prm, nh)
            xs, states_s, l_s = _rwkv_layer(xs, state_shift[j], states_s, j, prm, nh)
            sp_l.append(s_p); hp_l.append(l_p); hs_l.append(l_s)
        else:
            qp_b, kp, vp, gp, lfp, kv_b = _fox_project(xp, norm_b[j], w_in_b[j], bf_b[j], qn_b[j], kn_b[j], nh,
                                                        attn_q_scale=scale * LOG2E)
            cp = jnp.cumsum(lfp, axis=1)
            op = _fox_flash(qp_b, *kv_b, cp)
            xp = _fox_finish(xp, op, gp, w_out_b[j])

            qs, ksn, vsn, gs, lfs, _ = _fox_project(xs, norm_b[j], w_in_b[j], bf_b[j], qn_b[j], kn_b[j], nh)
            f_past = cache_logf[j][page_table].reshape(nb, past, nh).astype(F32)
            c_all = jnp.cumsum(jnp.concatenate([f_past, lfs], axis=1), axis=1)
            bias = c_all[:, past:, :] - c_all[:, :past, :]
            bias_t = bias.reshape(nb, n_pages, page, nh).transpose(0, 1, 3, 2)
            osm = _fox_decode(j, (qs * scale).reshape(nb, d), ksn.reshape(nb, d), vsn.reshape(nb, d),
                              ckt, cvt, bias_t, page_table)
            xs = _fox_finish(xs, osm.reshape(nb, 1, d), gs, w_out_b[j])
            hd4 = lambda z, n_, t_: z.reshape(n_, t_, nh, e)
            kp_l.append(hd4(kp, bsz, t)); vp_l.append(hd4(vp, bsz, t)); fp_l.append(lfp)
            ks_l.append(hd4(ksn, nb, ts)); vs_l.append(hd4(vsn, nb, ts)); fs_l.append(lfs)
    return (xp, xs,
            jnp.stack(kp_l), jnp.stack(vp_l), jnp.stack(fp_l), jnp.stack(sp_l), jnp.stack(hp_l),
            jnp.stack(ks_l), jnp.stack(vs_l), jnp.stack(fs_l), states_s, jnp.stack(hs_l))
```

```python
import functools

import jax
import jax.numpy as jnp
from jax import lax
from jax.experimental import pallas as pl
from jax.experimental.pallas import tpu as pltpu

F32 = jnp.float32
BF16 = jnp.bfloat16

LANES = 128
RMS_EPS = 1e-6
LNX_EPS = 64e-5
NEG_BIG = -1e30
LOG2E = 1.4426950408889634
WKV_CHUNK = 64
WKV_CHUNKS_PER_STEP = (4, 2, 1)
VMEM_LIMIT = 48 * 1024 * 1024


def _nt(x, y, precision=None):
    return lax.dot_general(x, y, (((1,), (1,)), ((), ())), precision=precision,
                           preferred_element_type=F32)


def _tn(x, y, precision=None):
    return lax.dot_general(x, y, (((0,), (0,)), ((), ())), precision=precision,
                           preferred_element_type=F32)


def _nn(x, y, precision=None):
    return jnp.dot(x, y, precision=precision, preferred_element_type=F32)


def _pick_tile(n, candidates):
    for c in candidates:
        if n % c == 0:
            return c
    return n


def _wkv_chunk_kernel(r_ref, lw_ref, k_ref, v_ref, a_ref, b_ref, o_ref, zt_ref):
    c = pl.program_id(1)

    @pl.when(c == 0)
    def _():
        zt_ref[...] = jnp.zeros_like(zt_ref)

    L = WKV_CHUNK
    n_chunks = r_ref.shape[1] // L
    n = 2 * L
    half = LANES // 2
    npair = r_ref.shape[2] // LANES

    ti = lax.broadcasted_iota(jnp.int32, (L, L), 0)
    tj = lax.broadcasted_iota(jnp.int32, (L, L), 1)
    tri = (ti >= tj).astype(BF16)
    head0 = lax.broadcasted_iota(jnp.int32, (L, LANES), 1) < half
    i = lax.broadcasted_iota(jnp.int32, (n, n), 0)
    j = lax.broadcasted_iota(jnp.int32, (n, n), 1)
    strict = i > j
    incl = i >= j
    eye = jnp.where(i == j, 1.0, 0.0)
    first = (i >> 1) == (j >> 1)
    levels = []
    lvl = 1
    while (2 << lvl) <= L:
        levels.append(((i >> (lvl + 1)) == (j >> (lvl + 1))) & ((i >> lvl) != (j >> lvl)))
        lvl += 1

    def stack(x):
        return jnp.concatenate([jnp.where(head0, x, 0.0), jnp.where(head0, 0.0, x)], axis=0)

    pairs = range(npair)
    units = range(n_chunks * npair)
    sls = [(slice((q // npair) * L, (q // npair + 1) * L), slice((q % npair) * LANES, (q % npair + 1) * LANES))
           for q in units]
    ar, bk, bk_h, vs, p_last = [], [], [], [], []
    for q in units:
        r, lw, k, v, a, b = (ref[0, sls[q][0], sls[q][1]] for ref in (r_ref, lw_ref, k_ref, v_ref, a_ref, b_ref))
        lw1 = lw.astype(BF16)
        rem = lw - lw1.astype(F32)
        lw2 = rem.astype(BF16)
        lw3 = (rem - lw2.astype(F32)).astype(BF16)
        cum = _nn(tri, lw1) + (_nn(tri, lw2) + _nn(tri, lw3))
        p_inc = jnp.exp(cum)
        p_exc = jnp.exp(cum - lw)
        p_inv = jnp.exp(-cum)
        pl_ = p_inc[L - 1:L, :]
        bk_f = jnp.concatenate([stack(b * p_inv), stack(k * p_inv)], axis=0)
        ar.append(jnp.concatenate([stack(a * p_exc), stack(r * p_inc)], axis=0).astype(BF16))
        bk.append(bk_f.astype(BF16))
        bk_h.append((bk_f * pl_).astype(BF16))
        vs.append(stack(v).astype(BF16))
        p_last.append(pl_)

    g = [_nt(ar[q], bk[q]) for q in units]
    a_ab = [jnp.where(strict, g[q][:n, :n], 0.0) for q in units]
    ak_rk = [jnp.concatenate([jnp.where(strict, g[q][:n, n:], 0.0),
                              jnp.where(incl, g[q][n:, n:], 0.0)], axis=0).astype(BF16) for q in units]
    a_rb = [jnp.where(incl, g[q][n:, :n], 0.0).astype(BF16) for q in units]

    def compress(full, size):
        return functools.reduce(lambda x, y: x + y, [full[i * size:(i + 1) * size] for i in range(n // size)])

    def expand(comp, size):
        lane_blk = lax.broadcasted_iota(jnp.int32, (size, n), 1) >> (size.bit_length() - 1)
        return jnp.concatenate([jnp.where(lane_blk == i, comp, 0.0) for i in range(n // size)], axis=0)

    sub = L // 2
    t_full = [eye + jnp.where(first, a_ab[q], 0.0) for q in units]
    t_comp = [compress(t_full[q], sub) for q in units]
    for off in levels[:-1]:
        x = [_nn(t_comp[q].astype(BF16), jnp.where(off, a_ab[q], 0.0).astype(BF16)).astype(BF16) for q in units]
        t_comp = [t_comp[q] + _nn(x[q], t_full[q].astype(BF16)) for q in units]
        t_full = [expand(t_comp[q], sub) for q in units]
    t_comp = [compress(t_full[q], L) for q in units]
    x = [_nn(t_comp[q].astype(BF16), jnp.where(levels[-1], a_ab[q], 0.0).astype(BF16)).astype(BF16) for q in units]
    t_comp = [(t_comp[q] + _nn(x[q], t_full[q].astype(BF16))).astype(BF16) for q in units]

    zt = [zt_ref[0, p] for p in pairs]
    for c in range(n_chunks):
        qs = [c * npair + p for p in pairs]
        y = [_nt(ar[q], zt[p].astype(BF16)) + _nn(ak_rk[q], vs[q]) for p, q in zip(pairs, qs)]
        u_b = [stack(_nn(t_comp[q], y[p][:n].astype(BF16))).astype(BF16) for p, q in zip(pairs, qs)]
        for p, q in zip(pairs, qs):
            o_st = y[p][n:] + _nn(a_rb[q], u_b[p])
            o_ref[0, sls[q][0], sls[q][1]] = o_st[:L] + o_st[L:]
            zt[p] = zt[p] * p_last[q] + _tn(jnp.concatenate([u_b[p], vs[q]], axis=0), bk_h[q])
    for p in pairs:
        zt_ref[0, p] = zt[p]


def _wkv_chunked(r, lw, k, v, a, b):
    bsz, t, d = r.shape
    npair = d // LANES
    tb = _pick_tile(t, tuple(WKV_CHUNK * c for c in WKV_CHUNKS_PER_STEP))
    spec = pl.BlockSpec((1, tb, d), lambda bi, c: (bi, c, 0))
    return pl.pallas_call(
        _wkv_chunk_kernel,
        grid=(bsz, t // tb),
        in_specs=[spec] * 6,
        out_specs=[spec, pl.BlockSpec((1, npair, LANES, LANES), lambda bi, c: (bi, 0, 0, 0))],
        out_shape=[jax.ShapeDtypeStruct((bsz, t, d), F32),
                   jax.ShapeDtypeStruct((bsz, npair, LANES, LANES), F32)],
        compiler_params=pltpu.CompilerParams(
            dimension_semantics=("parallel", "arbitrary"), vmem_limit_bytes=VMEM_LIMIT),
        name="wkv_chunked",
    )(r, lw, k, v, a, b)


def _wkv_step_kernel(s_ref, r_ref, lw_ref, k_ref, v_ref, a_ref, b_ref, so_ref, o_ref):
    _, bb, nh, e, _ = s_ref.shape
    half = LANES // 2
    ii = lax.broadcasted_iota(jnp.int32, (e, LANES), 0)
    jj = lax.broadcasted_iota(jnp.int32, (e, LANES), 1)
    eye_lo = ii == jj
    eye_hi = ii + half == jj

    for i in range(bb):
        units = []
        for h in range(nh):
            p, par = divmod(h, 2)
            sl = slice(p * LANES, (p + 1) * LANES)
            r, lw, k, v, a, b = (ref[i:i + 1, sl] for ref in (r_ref, lw_ref, k_ref, v_ref, a_ref, b_ref))
            w = jnp.exp(lw)
            if par == 0:
                rh, wh, kh, ah, bh = (z[:, :e] for z in (r, w, k, a, b))
            else:
                rh, wh, kh, ah, bh = (pltpu.roll(z, half, 1)[:, :e] for z in (r, w, k, a, b))
            s = s_ref[0, i, h]
            v_col = jnp.sum(jnp.where(eye_hi if par else eye_lo, v, 0.0), axis=1, keepdims=True)
            sa = jnp.sum(s * ah, axis=1, keepdims=True)
            units.append((s, rh, wh, kh, bh, v_col, sa))
        o_cols = []
        for h, (s, rh, wh, kh, bh, v_col, sa) in enumerate(units):
            s_new = s * wh + sa * bh + v_col * kh
            so_ref[0, i, h] = s_new
            o_cols.append(jnp.sum(s_new * rh, axis=1, keepdims=True))
        for p in range(nh // 2):
            o_ref[i:i + 1, p * LANES:(p + 1) * LANES] = (
                jnp.sum(jnp.where(eye_lo, o_cols[2 * p], 0.0), axis=0, keepdims=True)
                + jnp.sum(jnp.where(eye_hi, o_cols[2 * p + 1], 0.0), axis=0, keepdims=True))


def _wkv_step(states, layer, r, lw, k, v, a, b):
    _, n, h, e, _ = states.shape
    d = h * e
    bb = _pick_tile(n, (8,))
    s_spec = pl.BlockSpec((1, bb, h, e, e), lambda i: (layer, i, 0, 0, 0))
    row_spec = pl.BlockSpec((bb, d), lambda i: (i, 0))
    states, o = pl.pallas_call(
        _wkv_step_kernel,
        grid=(n // bb,),
        in_specs=[s_spec] + [row_spec] * 6,
        out_specs=[s_spec, row_spec],
        out_shape=[jax.ShapeDtypeStruct(states.shape, F32), jax.ShapeDtypeStruct((n, d), F32)],
        input_output_aliases={0: 0},
        compiler_params=pltpu.CompilerParams(
            dimension_semantics=("parallel",), vmem_limit_bytes=VMEM_LIMIT),
        name="wkv_step",
    )(states, r, lw, k, v, a, b)
    return o, states


def _fox_flash_kernel(*refs, tk):
    _flash_body(pl.program_id(1), pl.program_id(2), *refs, tk=tk)


def _flash_body(p, qi, q_ref, k_ref, v_ref, cq_ref, ck_ref, o_ref, s_ref, m_ref, l_ref, acc_ref, *, tk):
    tq = q_ref.shape[1]
    half = LANES // 2
    n_piece_rows = cq_ref.shape[2]
    nh = n_piece_rows // 3
    heads = range(2)

    q_head0 = lax.broadcasted_iota(jnp.int32, (tq, LANES), 1) < half
    k_head0 = lax.broadcasted_iota(jnp.int32, (tk, LANES), 1) < half

    prow = lax.broadcasted_iota(jnp.int32, (n_piece_rows, LANES), 0)
    plane = lax.broadcasted_iota(jnp.int32, (n_piece_rows, LANES), 1)
    piece, head = prow >> (nh.bit_length() - 1), prow & (nh - 1)
    base = jnp.where(head == 2 * p, half, jnp.where(head == 2 * p + 1, 0, -LANES))
    place_q = jnp.where(plane == base + piece, 1.0, 0.0).astype(BF16)
    place_k = jnp.where(plane == base + piece + 3, -1.0, 0.0).astype(BF16)
    slot = lax.broadcasted_iota(jnp.int32, (1, LANES), 1) & (half - 1)
    ones_q = jnp.where((slot >= 3) & (slot < 6), 1.0, 0.0)
    ones_k = jnp.where(slot < 3, 1.0, 0.0)

    q = q_ref[0]
    q_aug = (_nn(cq_ref[0], place_q) + ones_q).astype(BF16)
    q_ops = (jnp.where(q_head0, q, q_aug), jnp.where(q_head0, q_aug, q))

    m_ref[...] = jnp.full_like(m_ref, NEG_BIG)
    l_ref[...] = jnp.zeros_like(l_ref)
    acc_ref[...] = jnp.zeros_like(acc_ref)

    def rows(kj):
        return slice(kj * tk, (kj + 1) * tk)

    def scores(kj, slot_idx):
        k = k_ref[0, rows(kj), :]
        k_aug = (_nn(ck_ref[0, rows(kj), :], place_k) + ones_k).astype(BF16)
        k_ops = (jnp.where(k_head0, k, k_aug), jnp.where(k_head0, k_aug, k))
        for h in heads:
            s_ref[slot_idx, h] = _nt(q_ops[h], k_ops[h])

    def consume(kj, slot_idx, on_diagonal):
        v = v_ref[0, rows(kj), :]
        zero = jnp.zeros_like(v)
        v_own = (jnp.where(k_head0, v, zero), jnp.where(k_head0, zero, v))
        if on_diagonal:
            keep = (lax.broadcasted_iota(jnp.int32, (tq, tk), 0)
                    >= lax.broadcasted_iota(jnp.int32, (tq, tk), 1))
        alpha, pv = [], []
        for h in heads:
            s = s_ref[slot_idx, h]
            if on_diagonal:
                s = jnp.where(keep, s, NEG_BIG)
            m_prev = m_ref[h]
            m_new = jnp.maximum(m_prev, jnp.max(s, axis=1, keepdims=True))
            a = jnp.exp2(m_prev - m_new)
            pr = jnp.exp2(s - jnp.concatenate([m_new] * (tk // LANES), axis=1))
            l_ref[h] = a * l_ref[h] + jnp.sum(pr, axis=1, keepdims=True)
            m_ref[h] = m_new
            alpha.append(a)
            pv.append(_nn(pr.astype(BF16), v_own[h]))
        acc_ref[...] = jnp.where(q_head0, alpha[0], alpha[1]) * acc_ref[...] + (pv[0] + pv[1])

    def run(n_before):
        scores(0, 0)
        for kj in range(n_before):
            scores(kj + 1, (kj + 1) % 2)
            consume(kj, kj % 2, False)
        consume(n_before, n_before % 2, True)
        o_ref[0] = acc_ref[...] / jnp.where(q_head0, l_ref[0], l_ref[1])

    for n_before in range(k_ref.shape[1] // tk):
        pl.when(qi == n_before)(functools.partial(run, n_before))


def _bias_pieces(c):
    top16 = lambda z: lax.bitcast_convert_type(
        lax.bitcast_convert_type(z, jnp.uint32) & jnp.uint32(0xFFFF0000), F32)
    c2 = c * LOG2E
    hi = top16(c2)
    mid = top16(c2 - hi)
    return jnp.concatenate([hi, mid, top16(c2 - hi - mid)], axis=-1).astype(BF16)


def _fox_flash(q_b, k_b, v_b, c, tile=512):
    bsz, t, d = q_b.shape
    nh = c.shape[-1]
    assert nh & (nh - 1) == 0, "head count must be a power of two"
    npair = d // LANES
    tq = tk = _pick_tile(t, (tile, 256, 128))
    pieces = _bias_pieces(c)
    q_map = lambda bi, p, qi: (bi, qi, p)
    seq_map = lambda bi, p, qi: (bi, 0, p)
    return pl.pallas_call(
        functools.partial(_fox_flash_kernel, tk=tk),
        grid=(bsz, npair, t // tq),
        in_specs=[pl.BlockSpec((1, tq, LANES), q_map),
                  pl.BlockSpec((1, t, LANES), seq_map),
                  pl.BlockSpec((1, t, LANES), seq_map),
                  pl.BlockSpec((1, tq, 3 * nh), lambda bi, p, qi: (bi, qi, 0)),
                  pl.BlockSpec((1, t, 3 * nh), lambda bi, p, qi: (bi, 0, 0))],
        out_specs=pl.BlockSpec((1, tq, LANES), q_map),
        out_shape=jax.ShapeDtypeStruct((bsz, t, d), F32),
        scratch_shapes=[pltpu.VMEM((2, 2, tq, tk), F32),
                        pltpu.VMEM((2, tq, LANES), F32), pltpu.VMEM((2, tq, LANES), F32),
                        pltpu.VMEM((tq, LANES), F32)],
        compiler_params=pltpu.CompilerParams(
            dimension_semantics=("parallel", "parallel", "parallel"), vmem_limit_bytes=VMEM_LIMIT),
        name="fox_flash",
    )(q_b, k_b, v_b, pieces, pieces)


def _fox_decode_kernel(pt_ref, q_ref, *refs, npg):
    k_refs = refs[:npg]
    v_refs = refs[npg:2 * npg]
    _decode_body(pl.program_id(1), pl.num_programs(1), q_ref, k_refs, v_refs, *refs[2 * npg:])


def _decode_body(pg, n_groups, q_ref, k_refs, v_refs, bias_ref, kn_ref, vn_ref, o_ref, m_ref, l_ref, acc_ref):
    npg = len(k_refs)

    @pl.when(pg == 0)
    def _():
        m_ref[...] = jnp.full_like(m_ref, NEG_BIG)
        l_ref[...] = jnp.zeros_like(l_ref)
        acc_ref[...] = jnp.zeros_like(acc_ref)

    nh, e, page = k_refs[0].shape[2:]
    d = nh * e
    row = lax.broadcasted_iota(jnp.int32, (nh, d), 0)
    lane = lax.broadcasted_iota(jnp.int32, (nh, d), 1)
    own = (lane >= row * e) & (lane < (row + 1) * e)
    q_bd = jnp.where(own, q_ref[0], 0.0)
    q_b = q_bd.astype(BF16)
    s = [_nn(q_b, k_refs[t][0, 0].reshape(d, page).astype(BF16)) + bias_ref[0, t] for t in range(npg)]
    m_prev = m_ref[...]
    m_new = m_prev
    for t in range(npg):
        m_new = jnp.maximum(m_new, jnp.max(s[t], axis=1, keepdims=True))
    alpha = jnp.exp(m_prev - m_new)
    l_new = alpha * l_ref[...]
    acc = alpha * acc_ref[...]
    for t in range(npg):
        pr = jnp.exp(s[t] - m_new)
        l_new = l_new + jnp.sum(pr, axis=1, keepdims=True)
        acc = acc + _nt(pr.astype(BF16), v_refs[t][0, 0].reshape(d, page).astype(BF16))
    l_ref[...] = l_new
    acc_ref[...] = acc
    m_ref[...] = m_new

    @pl.when(pg == n_groups - 1)
    def _():
        s_new = jnp.sum(q_bd * kn_ref[0], axis=1, keepdims=True)
        m_fin = jnp.maximum(m_new, s_new)
        a_old = jnp.exp(m_new - m_fin)
        p_new = jnp.exp(s_new - m_fin)
        l_fin = l_new * a_old + p_new
        full = (acc * a_old + p_new * vn_ref[0]) / l_fin
        o_ref[0] = jnp.sum(jnp.where(own, full, 0.0), axis=0, keepdims=True)


def _fox_decode(layer, q, k_new, v_new, cache_kt, cache_vt, bias, page_table):
    n, d = q.shape
    n_pages = page_table.shape[1]
    nh, e, page = cache_kt.shape[2:]
    npg = _pick_tile(n_pages, (8, 4, 2, 1))

    def kv_spec(t):
        return pl.BlockSpec((1, 1, nh, e, page), lambda b, pg, pt: (layer, pt[b, pg * npg + t], 0, 0, 0))

    row_spec = pl.BlockSpec((1, 1, d), lambda b, pg, pt: (b, 0, 0))
    grid_spec = pltpu.PrefetchScalarGridSpec(
        num_scalar_prefetch=1,
        grid=(n, n_pages // npg),
        in_specs=[row_spec] + [kv_spec(t) for t in range(npg)] * 2
                 + [pl.BlockSpec((1, npg, nh, page), lambda b, pg, pt: (b, pg, 0, 0)), row_spec, row_spec],
        out_specs=row_spec,
        scratch_shapes=[pltpu.VMEM((nh, 1), F32), pltpu.VMEM((nh, 1), F32), pltpu.VMEM((nh, d), F32)],
    )
    o = pl.pallas_call(
        functools.partial(_fox_decode_kernel, npg=npg),
        grid_spec=grid_spec,
        out_shape=jax.ShapeDtypeStruct((n, 1, d), F32),
        compiler_params=pltpu.CompilerParams(
            dimension_semantics=("parallel", "arbitrary"), vmem_limit_bytes=VMEM_LIMIT),
        name="fox_decode",
    )(page_table, q.reshape(n, 1, d), *([cache_kt] * npg), *([cache_vt] * npg),
      bias, k_new.reshape(n, 1, d), v_new.reshape(n, 1, d))
    return o.reshape(n, d)


def _fox_attn_fused_kernel(pt_ref, fq_ref, fk_ref, fv_ref, fcq_ref, fck_ref, dq_ref, *refs, npg, tk, n_groups):
    k_refs = refs[:npg]
    v_refs = refs[npg:2 * npg]
    (bias_ref, kn_ref, vn_ref, o_ref, do_ref,
     s_ref, m_ref, l_ref, acc_ref, dm_ref, dl_ref, dacc_ref) = refs[2 * npg:]
    p, qi = pl.program_id(1), pl.program_id(2)
    step = (pl.program_id(0) * pl.num_programs(1) + p) * pl.num_programs(2) + qi
    _decode_body(step % n_groups, n_groups, dq_ref, k_refs, v_refs, bias_ref, kn_ref, vn_ref,
                 do_ref, dm_ref, dl_ref, dacc_ref)
    _flash_body(p, qi, fq_ref, fk_ref, fv_ref, fcq_ref, fck_ref, o_ref, s_ref, m_ref, l_ref, acc_ref, tk=tk)


def _fox_attention(layer, q_b, k_b, v_b, c, dq, dk_new, dv_new, cache_kt, cache_vt, bias, page_table, tile=512):
    bsz, t, d = q_b.shape
    n = dq.shape[0]
    nh, e, page = cache_kt.shape[2:]
    n_pages = page_table.shape[1]
    npair = d // LANES
    tq = tk = _pick_tile(t, (tile, 256, 128))
    nq = t // tq
    npg = _pick_tile(n_pages, (8, 4, 2, 1))
    n_groups = n_pages // npg
    if bsz * npair * nq != n * n_groups:
        return (_fox_flash(q_b, k_b, v_b, c, tile),
                _fox_decode(layer, dq, dk_new, dv_new, cache_kt, cache_vt, bias, page_table))
    pieces = _bias_pieces(c)

    def seq_of(bi, p, qi):
        return ((bi * npair + p) * nq + qi) // n_groups

    def grp_of(bi, p, qi):
        return ((bi * npair + p) * nq + qi) % n_groups

    def kv_spec(j):
        return pl.BlockSpec((1, 1, nh, e, page),
                            lambda bi, p, qi, pt: (layer, pt[seq_of(bi, p, qi), grp_of(bi, p, qi) * npg + j], 0, 0, 0))

    q_map = lambda bi, p, qi, pt: (bi, qi, p)
    seq_map = lambda bi, p, qi, pt: (bi, 0, p)
    row_spec = pl.BlockSpec((1, 1, d), lambda bi, p, qi, pt: (seq_of(bi, p, qi), 0, 0))
    grid_spec = pltpu.PrefetchScalarGridSpec(
        num_scalar_prefetch=1,
        grid=(bsz, npair, nq),
        in_specs=[pl.BlockSpec((1, tq, LANES), q_map),
                  pl.BlockSpec((1, t, LANES), seq_map),
                  pl.BlockSpec((1, t, LANES), seq_map),
                  pl.BlockSpec((1, tq, 3 * nh), lambda bi, p, qi, pt: (bi, qi, 0)),
                  pl.BlockSpec((1, t, 3 * nh), lambda bi, p, qi, pt: (bi, 0, 0)),
                  row_spec]
                 + [kv_spec(j) for j in range(npg)] * 2
                 + [pl.BlockSpec((1, npg, nh, page),
                                 lambda bi, p, qi, pt: (seq_of(bi, p, qi), grp_of(bi, p, qi), 0, 0)),
                    row_spec, row_spec],
        out_specs=[pl.BlockSpec((1, tq, LANES), q_map), row_spec],
        scratch_shapes=[pltpu.VMEM((2, 2, tq, tk), F32),
                        pltpu.VMEM((2, tq, LANES), F32), pltpu.VMEM((2, tq, LANES), F32),
                        pltpu.VMEM((tq, LANES), F32),
                        pltpu.VMEM((nh, 1), F32), pltpu.VMEM((nh, 1), F32), pltpu.VMEM((nh, d), F32)],
    )
    o, do = pl.pallas_call(
        functools.partial(_fox_attn_fused_kernel, npg=npg, tk=tk, n_groups=n_groups),
        grid_spec=grid_spec,
        out_shape=[jax.ShapeDtypeStruct((bsz, t, d), F32), jax.ShapeDtypeStruct((n, 1, d), F32)],
        compiler_params=pltpu.CompilerParams(
            dimension_semantics=("arbitrary", "arbitrary", "arbitrary"), vmem_limit_bytes=VMEM_LIMIT),
        name="fox_attention",
    )(page_table, q_b, k_b, v_b, pieces, pieces, dq.reshape(n, 1, d),
      *([cache_kt] * npg), *([cache_vt] * npg), bias, dk_new.reshape(n, 1, d), dv_new.reshape(n, 1, d))
    return o, do.reshape(n, d)


def _rmsnorm(x, g):
    xf = x.astype(F32)
    y = xf * lax.rsqrt(jnp.mean(xf * xf, axis=-1, keepdims=True) + RMS_EPS)
    return (y * g.astype(F32)).astype(x.dtype)


def _rwkv_proj_kernel(x_ref, hp_ref, g_ref, mu_ref, w_ref, w0_ref, w1_ref, w2_ref, a0_ref, a1_ref, a2_ref,
                      kk_ref, ka_ref, seg_ref, segt_ref,
                      r_ref, lw_ref, kf_ref, v_ref, nkk_ref, kka_ref, gate_ref, carry_ref,
                      *, tiles_per_seq):
    x = x_ref[...]
    tm = x.shape[0]
    g = g_ref[...]
    h = x * lax.rsqrt(jnp.mean(x * x, axis=-1, keepdims=True) + RMS_EPS) * g
    if tiles_per_seq == 0:
        h_prev = hp_ref[...]
    else:
        first = pl.program_id(0) % tiles_per_seq == 0
        before = jnp.where(first, hp_ref[0], carry_ref[...])
        row = lax.broadcasted_iota(jnp.int32, x.shape, 0)
        h_prev = jnp.where(row == 0, before, pltpu.roll(h, 1, 0))
        carry_ref[...] = h[tm - 1:tm, :]
    xx = h_prev - h
    mix = lambda s: (h + xx * mu_ref[s:s + 1, :]).astype(BF16)
    k = _nn(mix(1), w_ref[1])
    w_mid = _nn(mix(4), w1_ref[...])
    a_mid = _nn(mix(5), a1_ref[...])
    r_ref[...] = _nn(mix(0), w_ref[0])
    kk = k * kk_ref[...]
    kk_sq = _seg_sum(kk * kk, seg_ref[...])
    z = w0_ref[...] + _nn(jnp.tanh(w_mid).astype(BF16), w2_ref[...])
    a_pre = _nn(a_mid.astype(BF16), a2_ref[...])
    v_ref[...] = _nn(mix(2), w_ref[2])
    kk = kk * _seg_bcast(1.0 / jnp.maximum(jnp.sqrt(kk_sq), 1e-12), segt_ref[...])
    gate_ref[...] = _nn(mix(3), w_ref[3])
    lw_ref[...] = -jnp.exp(jnp.minimum(z, 0.0) - jnp.log(1.0 + jnp.exp(-jnp.abs(z))) - 0.5)
    a = jax.nn.sigmoid(a0_ref[...] + a_pre)
    kf_ref[...] = k * (1.0 + (a - 1.0) * ka_ref[...])
    nkk_ref[...] = -kk
    kka_ref[...] = kk * a


def _rwkv_out_kernel(o_ref, r_ref, kf_ref, v_ref, gate_ref, x_ref, lw_ref, lb_ref, rk_ref, w_ref,
                     seg_ref, segt_ref, y_ref):
    seg = seg_ref[...]
    seg_t = segt_ref[...]
    tm, d = o_ref.shape
    inv_e = 1.0 / (d // seg.shape[1])
    n_groups = 2 if tm % 16 == 0 else 1
    rows = [slice(i * (tm // n_groups), (i + 1) * (tm // n_groups)) for i in range(n_groups)]
    o = [o_ref[rs, :] for rs in rows]
    o_sum = [_seg_sum(x, seg) for x in o]
    rk_sum = [_seg_sum(r_ref[rs, :] * kf_ref[rs, :] * rk_ref[...], seg) for rs in rows]
    cen = [x - _seg_bcast(s * inv_e, seg_t) for x, s in zip(o, o_sum)]
    var = [_seg_sum(c * c, seg) * inv_e for c in cen]
    bonus = [_seg_bcast(s, seg_t) * v_ref[rs, :] for s, rs in zip(rk_sum, rows)]
    gn = [c * _seg_bcast(lax.rsqrt(vr + LNX_EPS), seg_t) * lw_ref[...] + lb_ref[...] for c, vr in zip(cen, var)]
    for rs, gn_i, bonus_i in zip(rows, gn, bonus):
        g = gate_ref[rs, :]
        y = ((gn_i + bonus_i) * (g * jax.nn.sigmoid(g))).astype(BF16)
        y_ref[rs, :] = x_ref[rs, :] + _nn(y, w_ref[...])


def _rwkv_layer(x, h_prev0, states, layer, prm, nh):
    (norm, mu, w_in, w0, w1, w2, a0, a1, a2, k_k, k_a, r_k, lnx_w, lnx_b, w_out) = prm
    bsz, t, d = x.shape
    e = d // nh
    m = bsz * t
    seg, seg_t = _seg_mats(d, nh)
    row = lambda z: z.reshape(1, -1).astype(F32)
    full = lambda arr: pl.BlockSpec(arr.shape, lambda i: (0,) * arr.ndim)
    x2 = x.reshape(m, d)
    if t == 1:
        tm = _pick_tile(m, (256, 128))
        tiles_per_seq = 0
        hp = h_prev0.astype(F32)
        hp_spec = pl.BlockSpec((tm, d), lambda i: (i, 0))
    else:
        tm = _pick_tile(t, (256, 128))
        tiles_per_seq = t // tm
        hp = h_prev0.astype(F32).reshape(bsz, 1, d)
        hp_spec = pl.BlockSpec((1, 1, d), lambda i: (i // tiles_per_seq, 0, 0))
    tile = pl.BlockSpec((tm, d), lambda i: (i, 0))
    consts = [row(norm), mu.astype(F32), w_in.astype(BF16), row(w0), w1.astype(BF16), w2.astype(BF16),
              row(a0), a1.astype(BF16), a2.astype(BF16), row(k_k), row(k_a), seg, seg_t]
    r, lw, kf, v, neg_kk, kk_a, gate = pl.pallas_call(
        functools.partial(_rwkv_proj_kernel, tiles_per_seq=tiles_per_seq),
        grid=(m // tm,),
        in_specs=[tile, hp_spec] + [full(c) for c in consts],
        out_specs=[tile] * 7,
        out_shape=[jax.ShapeDtypeStruct((m, d), F32)] * 7,
        scratch_shapes=[pltpu.VMEM((1, d), F32)],
        compiler_params=pltpu.CompilerParams(
            dimension_semantics=("arbitrary",), vmem_limit_bytes=VMEM_LIMIT),
        name="rwkv_proj",
    )(x2, hp, *consts)

    if states is None:
        seq = lambda z: z.reshape(bsz, t, d)
        o, zt = _wkv_chunked(seq(r), seq(lw), seq(kf), seq(v), seq(neg_kk), seq(kk_a))
        zt = zt.reshape(bsz, d // LANES, 2, e, 2, e)
        s_fin = jnp.stack([zt[:, :, 0, :, 0, :], zt[:, :, 1, :, 1, :]], axis=2).reshape(bsz, nh, e, e)
    else:
        o, s_fin = _wkv_step(states, layer, r, lw, kf, v, neg_kk, kk_a)

    tmo = _pick_tile(m, (256, 128))
    tile_o = pl.BlockSpec((tmo, d), lambda i: (i, 0))
    consts_o = [row(lnx_w), row(lnx_b), row(r_k), w_out.astype(BF16), seg, seg_t]
    x_new = pl.pallas_call(
        _rwkv_out_kernel,
        grid=(m // tmo,),
        in_specs=[tile_o] * 6 + [full(c) for c in consts_o],
        out_specs=tile_o,
        out_shape=jax.ShapeDtypeStruct((m, d), F32),
        compiler_params=pltpu.CompilerParams(
            dimension_semantics=("parallel",), vmem_limit_bytes=VMEM_LIMIT),
        name="rwkv_out",
    )(o.reshape(m, d), r, kf, v, gate, x2, *consts_o)
    h_last = _rmsnorm(x[:, -1, :], norm)
    return x_new.reshape(bsz, t, d), s_fin, h_last


def _split2(x):
    hi = x.astype(BF16)
    return hi, (x - hi.astype(F32)).astype(BF16)


def _seg_sum(x, seg):
    hi, lo = _split2(x)
    return _nn(hi, seg) + _nn(lo, seg)


def _seg_bcast(y, seg_t):
    hi, lo = _split2(y)
    return _nn(hi, seg_t) + _nn(lo, seg_t)


def _seg_mats(d, nh):
    lane_head = jnp.arange(d, dtype=jnp.int32) // (d // nh)
    seg = (lane_head[:, None] == jnp.arange(nh, dtype=jnp.int32)[None, :]).astype(BF16)
    return seg, seg.T


def _fox_proj_kernel(x_ref, g_ref, w_ref, wf_ref, bf_ref, qn_ref, kn_ref, seg_ref, segt_ref,
                     q_ref, k_ref, v_ref, gate_ref, lf_ref, *attn_refs, q_scale):
    x = x_ref[...]
    d = x.shape[1]
    hb = (x * lax.rsqrt(jnp.mean(x * x, axis=-1, keepdims=True) + RMS_EPS) * g_ref[...]).astype(BF16)
    seg = seg_ref[...]
    seg_t = segt_ref[...]
    inv_e = 1.0 / (d // seg.shape[1])

    q = _nn(hb, w_ref[:, 0:d])
    k = _nn(hb, w_ref[:, d:2 * d])
    q_ms = _seg_sum(q * q, seg)
    v = _nn(hb, w_ref[:, 2 * d:3 * d])
    k_ms = _seg_sum(k * k, seg)
    q = q * _seg_bcast(lax.rsqrt(q_ms * inv_e + RMS_EPS), seg_t) * qn_ref[...]
    gate_ref[...] = _nn(hb, w_ref[:, 3 * d:4 * d])
    k = k * _seg_bcast(lax.rsqrt(k_ms * inv_e + RMS_EPS), seg_t) * kn_ref[...]
    k_ref[...] = k
    v_ref[...] = v
    if attn_refs:
        kb_ref, vb_ref = attn_refs
        q_ref[...] = (q * q_scale).astype(BF16)
        kb_ref[...] = k.astype(BF16)
        vb_ref[...] = v.astype(BF16)
    else:
        q_ref[...] = q
    f = _nn(hb, wf_ref[...]) + bf_ref[...]
    lf_ref[...] = jnp.minimum(f, 0.0) - jnp.log(1.0 + jnp.exp(-jnp.abs(f)))


def _fox_project(x, norm, w_in, b_f, qn_g, kn_g, nh, attn_q_scale=None):
    bsz, t, d = x.shape
    m = bsz * t
    tm = _pick_tile(m, (256, 128))
    seg, seg_t = _seg_mats(d, nh)
    row = lambda z: z.reshape(1, -1).astype(F32)
    full = lambda a: pl.BlockSpec(a.shape, lambda i: (0,) * a.ndim)
    consts = [row(norm), w_in[:, :4 * d].astype(BF16), w_in[:, 4 * d:].astype(BF16), row(b_f),
              row(jnp.tile(qn_g, nh)), row(jnp.tile(kn_g, nh)), seg, seg_t]
    tile = pl.BlockSpec((tm, d), lambda i: (i, 0))
    n_attn = 0 if attn_q_scale is None else 2
    q_dtype = F32 if attn_q_scale is None else BF16
    outs = pl.pallas_call(
        functools.partial(_fox_proj_kernel, q_scale=attn_q_scale),
        grid=(m // tm,),
        in_specs=[tile] + [full(a) for a in consts],
        out_specs=[tile] * 4 + [pl.BlockSpec((tm, nh), lambda i: (i, 0))] + [tile] * n_attn,
        out_shape=([jax.ShapeDtypeStruct((m, d), q_dtype)] + [jax.ShapeDtypeStruct((m, d), F32)] * 3
                   + [jax.ShapeDtypeStruct((m, nh), F32)] + [jax.ShapeDtypeStruct((m, d), BF16)] * n_attn),
        compiler_params=pltpu.CompilerParams(
            dimension_semantics=("parallel",), vmem_limit_bytes=VMEM_LIMIT),
        name="fox_proj",
    )(x.reshape(m, d), *consts)
    q, k, v, gate, logf = outs[:5]
    seq = lambda z: z.reshape(bsz, t, d)
    return seq(q), seq(k), seq(v), gate, logf.reshape(bsz, t, nh), tuple(seq(z) for z in outs[5:])


def _gate_out_kernel(o_ref, gate_ref, x_ref, w_ref, y_ref):
    g = gate_ref[...]
    y = (o_ref[...] * (g * jax.nn.sigmoid(g))).astype(BF16)
    y_ref[...] = x_ref[...] + _nn(y, w_ref[...])


def _fox_finish(x, o, gate, w_out):
    bsz, t, d = x.shape
    m = bsz * t
    tm = _pick_tile(m, (512, 256, 128))
    tile = pl.BlockSpec((tm, d), lambda i: (i, 0))
    y = pl.pallas_call(
        _gate_out_kernel,
        grid=(m // tm,),
        in_specs=[tile, tile, tile, pl.BlockSpec((d, d), lambda i: (0, 0))],
        out_specs=tile,
        out_shape=jax.ShapeDtypeStruct((m, d), F32),
        compiler_params=pltpu.CompilerParams(
            dimension_semantics=("parallel",), vmem_limit_bytes=VMEM_LIMIT),
        name="gate_out",
    )(o.reshape(m, d), gate, x.reshape(m, d), w_out.astype(BF16))
    return y.reshape(bsz, t, d)


def kernel(x_prompt, x_sample, state_wkv, state_shift, cache_k, cache_v, cache_logf, page_table,
           norm_a, mu_a, w_in_a, w0_a, w1_a, w2_a, a0_a, a1_a, a2_a, kk_a, ka_a, rk_a, lnx_w_a, lnx_b_a, w_out_a,
           norm_b, w_in_b, bf_b, qn_b, kn_b, w_out_b):
    bsz, t, d = x_prompt.shape
    nb, ts, _ = x_sample.shape
    assert ts == 1, "the sample group carries one new token per sequence"
    nh, e = rk_a.shape[1], rk_a.shape[2]
    assert d == nh * e and 2 * e == LANES and t % WKV_CHUNK == 0
    n_layers_a = norm_a.shape[0]
    n_layers_b = norm_b.shape[0]
    depth = n_layers_a + n_layers_b
    n_pool, page = cache_k.shape[1], cache_k.shape[2]
    n_pages = page_table.shape[1]
    past = n_pages * page
    scale = e ** -0.5
    ckt = jnp.transpose(cache_k, (0, 1, 3, 4, 2))
    cvt = jnp.transpose(cache_v, (0, 1, 3, 4, 2))

    xp, xs = x_prompt, x_sample
    kp_l, vp_l, fp_l, sp_l, hp_l = [], [], [], [], []
    ks_l, vs_l, fs_l, hs_l = [], [], [], []
    states_s = state_wkv.astype(F32)
    for i in range(depth):
        j = i // 2
        if i % 2 == 0:
            prm = (norm_a[j], mu_a[j], w_in_a[j], w0_a[j], w1_a[j], w2_a[j], a0_a[j], a1_a[j], a2_a[j],
                   kk_a[j], ka_a[j], rk_a[j], lnx_w_a[j], lnx_b_a[j], w_out_a[j])
            xp, s_p, l_p = _rwkv_layer(xp, jnp.zeros((bsz, d), xp.dtype), None, j, prm, nh)
            xs, states_s, l_s = _rwkv_layer(xs, state_shift[j], states_s, j, prm, nh)
            sp_l.append(s_p); hp_l.append(l_p); hs_l.append(l_s)
        else:
            qp_b, kp, vp, gp, lfp, kv_b = _fox_project(xp, norm_b[j], w_in_b[j], bf_b[j], qn_b[j], kn_b[j], nh,
                                                        attn_q_scale=scale * LOG2E)
            cp = jnp.cumsum(lfp, axis=1)
            qs, ksn, vsn, gs, lfs, _ = _fox_project(xs, norm_b[j], w_in_b[j], bf_b[j], qn_b[j], kn_b[j], nh)
            f_past = cache_logf[j][page_table].reshape(nb, past, nh).astype(F32)
            c_all = jnp.cumsum(jnp.concatenate([f_past, lfs], axis=1), axis=1)
            bias = c_all[:, past:, :] - c_all[:, :past, :]
            bias_t = bias.reshape(nb, n_pages, page, nh).transpose(0, 1, 3, 2)
            op, osm = _fox_attention(j, qp_b, *kv_b, cp, (qs * scale).reshape(nb, d), ksn.reshape(nb, d),
                                     vsn.reshape(nb, d), ckt, cvt, bias_t, page_table)
            xp = _fox_finish(xp, op, gp, w_out_b[j])
            xs = _fox_finish(xs, osm.reshape(nb, 1, d), gs, w_out_b[j])
            hd4 = lambda z, n_, t_: z.reshape(n_, t_, nh, e)
            kp_l.append(hd4(kp, bsz, t)); vp_l.append(hd4(vp, bsz, t)); fp_l.append(lfp)
            ks_l.append(hd4(ksn, nb, ts)); vs_l.append(hd4(vsn, nb, ts)); fs_l.append(lfs)
    return (xp, xs,
            jnp.stack(kp_l), jnp.stack(vp_l), jnp.stack(fp_l), jnp.stack(sp_l), jnp.stack(hp_l),
            jnp.stack(ks_l), jnp.stack(vs_l), jnp.stack(fs_l), states_s, jnp.stack(hs_l))
```

```python
import functools

import jax
import jax.numpy as jnp
from jax import lax
from jax.experimental import pallas as pl
from jax.experimental.pallas import tpu as pltpu

F32 = jnp.float32
BF16 = jnp.bfloat16

LANES = 128
RMS_EPS = 1e-6
LNX_EPS = 64e-5
NEG_BIG = -1e30
LOG2E = 1.4426950408889634
WKV_CHUNK = 64
WKV_CHUNKS_PER_STEP = (4, 2, 1)
VMEM_LIMIT = 48 * 1024 * 1024


def _nt(x, y, precision=None):
    return lax.dot_general(x, y, (((1,), (1,)), ((), ())), precision=precision,
                           preferred_element_type=F32)


def _tn(x, y, precision=None):
    return lax.dot_general(x, y, (((0,), (0,)), ((), ())), precision=precision,
                           preferred_element_type=F32)


def _nn(x, y, precision=None):
    return jnp.dot(x, y, precision=precision, preferred_element_type=F32)


def _pick_tile(n, candidates):
    for c in candidates:
        if n % c == 0:
            return c
    return n


def _wkv_chunk_kernel(r_ref, lw_ref, k_ref, v_ref, a_ref, b_ref, o_ref, zt_ref):
    c = pl.program_id(1)

    @pl.when(c == 0)
    def _():
        zt_ref[...] = jnp.zeros_like(zt_ref)

    L = WKV_CHUNK
    n_chunks = r_ref.shape[1] // L
    n = 2 * L
    half = LANES // 2
    npair = r_ref.shape[2] // LANES

    ti = lax.broadcasted_iota(jnp.int32, (L, L), 0)
    tj = lax.broadcasted_iota(jnp.int32, (L, L), 1)
    tri = (ti >= tj).astype(BF16)
    head0 = lax.broadcasted_iota(jnp.int32, (L, LANES), 1) < half
    i = lax.broadcasted_iota(jnp.int32, (n, n), 0)
    j = lax.broadcasted_iota(jnp.int32, (n, n), 1)
    strict = i > j
    incl = i >= j
    eye = jnp.where(i == j, 1.0, 0.0)
    first = (i >> 1) == (j >> 1)
    levels = []
    lvl = 1
    while (2 << lvl) <= L:
        levels.append(((i >> (lvl + 1)) == (j >> (lvl + 1))) & ((i >> lvl) != (j >> lvl)))
        lvl += 1

    def stack(x):
        return jnp.concatenate([jnp.where(head0, x, 0.0), jnp.where(head0, 0.0, x)], axis=0)

    pairs = range(npair)
    units = range(n_chunks * npair)
    sls = [(slice((q // npair) * L, (q // npair + 1) * L), slice((q % npair) * LANES, (q % npair + 1) * LANES))
           for q in units]
    ar, bk, bk_h, vs, p_last = [], [], [], [], []
    for q in units:
        r, lw, k, v, a, b = (ref[0, sls[q][0], sls[q][1]] for ref in (r_ref, lw_ref, k_ref, v_ref, a_ref, b_ref))
        lw1 = lw.astype(BF16)
        rem = lw - lw1.astype(F32)
        lw2 = rem.astype(BF16)
        lw3 = (rem - lw2.astype(F32)).astype(BF16)
        cum = _nn(tri, lw1) + (_nn(tri, lw2) + _nn(tri, lw3))
        p_inc = jnp.exp(cum)
        p_exc = jnp.exp(cum - lw)
        p_inv = jnp.exp(-cum)
        pl_ = p_inc[L - 1:L, :]
        bk_f = jnp.concatenate([stack(b * p_inv), stack(k * p_inv)], axis=0)
        ar.append(jnp.concatenate([stack(a * p_exc), stack(r * p_inc)], axis=0).astype(BF16))
        bk.append(bk_f.astype(BF16))
        bk_h.append((bk_f * pl_).astype(BF16))
        vs.append(stack(v).astype(BF16))
        p_last.append(pl_)

    g = [_nt(ar[q], bk[q]) for q in units]
    a_ab = [jnp.where(strict, g[q][:n, :n], 0.0) for q in units]
    ak_rk = [jnp.concatenate([jnp.where(strict, g[q][:n, n:], 0.0),
                              jnp.where(incl, g[q][n:, n:], 0.0)], axis=0).astype(BF16) for q in units]
    a_rb = [jnp.where(incl, g[q][n:, :n], 0.0).astype(BF16) for q in units]

    def compress(full, size):
        return functools.reduce(lambda x, y: x + y, [full[i * size:(i + 1) * size] for i in range(n // size)])

    def expand(comp, size):
        lane_blk = lax.broadcasted_iota(jnp.int32, (size, n), 1) >> (size.bit_length() - 1)
        return jnp.concatenate([jnp.where(lane_blk == i, comp, 0.0) for i in range(n // size)], axis=0)

    sub = L // 2
    t_full = [eye + jnp.where(first, a_ab[q], 0.0) for q in units]
    t_comp = [compress(t_full[q], sub) for q in units]
    for off in levels[:-1]:
        x = [_nn(t_comp[q].astype(BF16), jnp.where(off, a_ab[q], 0.0).astype(BF16)).astype(BF16) for q in units]
        t_comp = [t_comp[q] + _nn(x[q], t_full[q].astype(BF16)) for q in units]
        t_full = [expand(t_comp[q], sub) for q in units]
    t_comp = [compress(t_full[q], L) for q in units]
    x = [_nn(t_comp[q].astype(BF16), jnp.where(levels[-1], a_ab[q], 0.0).astype(BF16)).astype(BF16) for q in units]
    t_comp = [(t_comp[q] + _nn(x[q], t_full[q].astype(BF16))).astype(BF16) for q in units]

    zt = [zt_ref[0, p] for p in pairs]
    for c in range(n_chunks):
        qs = [c * npair + p for p in pairs]
        y = [_nt(ar[q], zt[p].astype(BF16)) + _nn(ak_rk[q], vs[q]) for p, q in zip(pairs, qs)]
        u_b = [stack(_nn(t_comp[q], y[p][:n].astype(BF16))).astype(BF16) for p, q in zip(pairs, qs)]
        for p, q in zip(pairs, qs):
            o_st = y[p][n:] + _nn(a_rb[q], u_b[p])
            o_ref[0, sls[q][0], sls[q][1]] = o_st[:L] + o_st[L:]
            zt[p] = zt[p] * p_last[q] + _tn(jnp.concatenate([u_b[p], vs[q]], axis=0), bk_h[q])
    for p in pairs:
        zt_ref[0, p] = zt[p]


def _wkv_chunked(r, lw, k, v, a, b):
    bsz, t, d = r.shape
    npair = d // LANES
    tb = _pick_tile(t, tuple(WKV_CHUNK * c for c in WKV_CHUNKS_PER_STEP))
    spec = pl.BlockSpec((1, tb, d), lambda bi, c: (bi, c, 0))
    return pl.pallas_call(
        _wkv_chunk_kernel,
        grid=(bsz, t // tb),
        in_specs=[spec] * 6,
        out_specs=[spec, pl.BlockSpec((1, npair, LANES, LANES), lambda bi, c: (bi, 0, 0, 0))],
        out_shape=[jax.ShapeDtypeStruct((bsz, t, d), F32),
                   jax.ShapeDtypeStruct((bsz, npair, LANES, LANES), F32)],
        compiler_params=pltpu.CompilerParams(
            dimension_semantics=("parallel", "arbitrary"), vmem_limit_bytes=VMEM_LIMIT),
        name="wkv_chunked",
    )(r, lw, k, v, a, b)


def _wkv_step_kernel(s_ref, r_ref, lw_ref, k_ref, v_ref, a_ref, b_ref, so_ref, o_ref):
    _, bb, nh, e, _ = s_ref.shape
    half = LANES // 2
    ii = lax.broadcasted_iota(jnp.int32, (e, LANES), 0)
    jj = lax.broadcasted_iota(jnp.int32, (e, LANES), 1)
    eye_lo = ii == jj
    eye_hi = ii + half == jj

    for i in range(bb):
        units = []
        for h in range(nh):
            p, par = divmod(h, 2)
            sl = slice(p * LANES, (p + 1) * LANES)
            r, lw, k, v, a, b = (ref[i:i + 1, sl] for ref in (r_ref, lw_ref, k_ref, v_ref, a_ref, b_ref))
            w = jnp.exp(lw)
            if par == 0:
                rh, wh, kh, ah, bh = (z[:, :e] for z in (r, w, k, a, b))
            else:
                rh, wh, kh, ah, bh = (pltpu.roll(z, half, 1)[:, :e] for z in (r, w, k, a, b))
            s = s_ref[0, i, h]
            v_col = jnp.sum(jnp.where(eye_hi if par else eye_lo, v, 0.0), axis=1, keepdims=True)
            sa = jnp.sum(s * ah, axis=1, keepdims=True)
            units.append((s, rh, wh, kh, bh, v_col, sa))
        o_cols = []
        for h, (s, rh, wh, kh, bh, v_col, sa) in enumerate(units):
            s_new = s * wh + sa * bh + v_col * kh
            so_ref[0, i, h] = s_new
            o_cols.append(jnp.sum(s_new * rh, axis=1, keepdims=True))
        for p in range(nh // 2):
            o_ref[i:i + 1, p * LANES:(p + 1) * LANES] = (
                jnp.sum(jnp.where(eye_lo, o_cols[2 * p], 0.0), axis=0, keepdims=True)
                + jnp.sum(jnp.where(eye_hi, o_cols[2 * p + 1], 0.0), axis=0, keepdims=True))


def _wkv_step(states, layer, r, lw, k, v, a, b):
    _, n, h, e, _ = states.shape
    d = h * e
    bb = _pick_tile(n, (8,))
    s_spec = pl.BlockSpec((1, bb, h, e, e), lambda i: (layer, i, 0, 0, 0))
    row_spec = pl.BlockSpec((bb, d), lambda i: (i, 0))
    states, o = pl.pallas_call(
        _wkv_step_kernel,
        grid=(n // bb,),
        in_specs=[s_spec] + [row_spec] * 6,
        out_specs=[s_spec, row_spec],
        out_shape=[jax.ShapeDtypeStruct(states.shape, F32), jax.ShapeDtypeStruct((n, d), F32)],
        input_output_aliases={0: 0},
        compiler_params=pltpu.CompilerParams(
            dimension_semantics=("parallel",), vmem_limit_bytes=VMEM_LIMIT),
        name="wkv_step",
    )(states, r, lw, k, v, a, b)
    return o, states


def _fox_flash_kernel(*refs, tk):
    _flash_body(pl.program_id(1), pl.program_id(2), *refs, tk=tk)


def _flash_body(p, qi, q_ref, k_ref, v_ref, cq_ref, ck_ref, o_ref, s_ref, m_ref, l_ref, acc_ref, *, tk,
                side_work=()):
    tq = q_ref.shape[1]
    half = LANES // 2
    n_piece_rows = cq_ref.shape[2]
    nh = n_piece_rows // 3
    heads = range(2)

    q_head0 = lax.broadcasted_iota(jnp.int32, (tq, LANES), 1) < half
    k_head0 = lax.broadcasted_iota(jnp.int32, (tk, LANES), 1) < half

    prow = lax.broadcasted_iota(jnp.int32, (n_piece_rows, LANES), 0)
    plane = lax.broadcasted_iota(jnp.int32, (n_piece_rows, LANES), 1)
    piece, head = prow >> (nh.bit_length() - 1), prow & (nh - 1)
    base = jnp.where(head == 2 * p, half, jnp.where(head == 2 * p + 1, 0, -LANES))
    place_q = jnp.where(plane == base + piece, 1.0, 0.0).astype(BF16)
    place_k = jnp.where(plane == base + piece + 3, -1.0, 0.0).astype(BF16)
    slot = lax.broadcasted_iota(jnp.int32, (1, LANES), 1) & (half - 1)
    ones_q = jnp.where((slot >= 3) & (slot < 6), 1.0, 0.0)
    ones_k = jnp.where(slot < 3, 1.0, 0.0)

    q = q_ref[0]
    q_aug = (_nn(cq_ref[0], place_q) + ones_q).astype(BF16)
    q_ops = (jnp.where(q_head0, q, q_aug), jnp.where(q_head0, q_aug, q))

    m_ref[...] = jnp.full_like(m_ref, NEG_BIG)
    l_ref[...] = jnp.zeros_like(l_ref)
    acc_ref[...] = jnp.zeros_like(acc_ref)

    def rows(kj):
        return slice(kj * tk, (kj + 1) * tk)

    def scores(kj, slot_idx):
        k = k_ref[0, rows(kj), :]
        k_aug = (_nn(ck_ref[0, rows(kj), :], place_k) + ones_k).astype(BF16)
        k_ops = (jnp.where(k_head0, k, k_aug), jnp.where(k_head0, k_aug, k))
        for h in heads:
            s_ref[slot_idx, h] = _nt(q_ops[h], k_ops[h])

    def consume(kj, slot_idx, on_diagonal):
        v = v_ref[0, rows(kj), :]
        zero = jnp.zeros_like(v)
        v_own = (jnp.where(k_head0, v, zero), jnp.where(k_head0, zero, v))
        if on_diagonal:
            keep = (lax.broadcasted_iota(jnp.int32, (tq, tk), 0)
                    >= lax.broadcasted_iota(jnp.int32, (tq, tk), 1))
        alpha, pv = [], []
        for h in heads:
            s = s_ref[slot_idx, h]
            if on_diagonal:
                s = jnp.where(keep, s, NEG_BIG)
            m_prev = m_ref[h]
            m_new = jnp.maximum(m_prev, jnp.max(s, axis=1, keepdims=True))
            a = jnp.exp2(m_prev - m_new)
            pr = jnp.exp2(s - jnp.concatenate([m_new] * (tk // LANES), axis=1))
            l_ref[h] = a * l_ref[h] + jnp.sum(pr, axis=1, keepdims=True)
            m_ref[h] = m_new
            alpha.append(a)
            pv.append(_nn(pr.astype(BF16), v_own[h]))
        acc_ref[...] = jnp.where(q_head0, alpha[0], alpha[1]) * acc_ref[...] + (pv[0] + pv[1])

    side = list(side_work) + [lambda: None] * (3 - len(side_work))

    def run(n_before):
        scores(0, 0)
        side[0]()
        for kj in range(n_before):
            scores(kj + 1, (kj + 1) % 2)
            consume(kj, kj % 2, False)
        side[1]()
        consume(n_before, n_before % 2, True)
        side[2]()
        o_ref[0] = acc_ref[...] / jnp.where(q_head0, l_ref[0], l_ref[1])

    for n_before in range(k_ref.shape[1] // tk):
        pl.when(qi == n_before)(functools.partial(run, n_before))


def _bias_pieces(c):
    top16 = lambda z: lax.bitcast_convert_type(
        lax.bitcast_convert_type(z, jnp.uint32) & jnp.uint32(0xFFFF0000), F32)
    c2 = c * LOG2E
    hi = top16(c2)
    mid = top16(c2 - hi)
    return jnp.concatenate([hi, mid, top16(c2 - hi - mid)], axis=-1).astype(BF16)


def _fox_flash(q_b, k_b, v_b, c, tile=512):
    bsz, t, d = q_b.shape
    nh = c.shape[-1]
    assert nh & (nh - 1) == 0, "head count must be a power of two"
    npair = d // LANES
    tq = tk = _pick_tile(t, (tile, 256, 128))
    pieces = _bias_pieces(c)
    q_map = lambda bi, p, qi: (bi, qi, p)
    seq_map = lambda bi, p, qi: (bi, 0, p)
    return pl.pallas_call(
        functools.partial(_fox_flash_kernel, tk=tk),
        grid=(bsz, npair, t // tq),
        in_specs=[pl.BlockSpec((1, tq, LANES), q_map),
                  pl.BlockSpec((1, t, LANES), seq_map),
                  pl.BlockSpec((1, t, LANES), seq_map),
                  pl.BlockSpec((1, tq, 3 * nh), lambda bi, p, qi: (bi, qi, 0)),
                  pl.BlockSpec((1, t, 3 * nh), lambda bi, p, qi: (bi, 0, 0))],
        out_specs=pl.BlockSpec((1, tq, LANES), q_map),
        out_shape=jax.ShapeDtypeStruct((bsz, t, d), F32),
        scratch_shapes=[pltpu.VMEM((2, 2, tq, tk), F32),
                        pltpu.VMEM((2, tq, LANES), F32), pltpu.VMEM((2, tq, LANES), F32),
                        pltpu.VMEM((tq, LANES), F32)],
        compiler_params=pltpu.CompilerParams(
            dimension_semantics=("parallel", "parallel", "parallel"), vmem_limit_bytes=VMEM_LIMIT),
        name="fox_flash",
    )(q_b, k_b, v_b, pieces, pieces)


def _fox_decode_kernel(pt_ref, q_ref, *refs, npg):
    k_refs = refs[:npg]
    v_refs = refs[npg:2 * npg]
    for phase in _decode_phases(pl.program_id(1), q_ref, k_refs, v_refs, *refs[2 * npg:]):
        phase()


def _decode_phases(pg, q_ref, k_refs, v_refs, bias_ref, kn_ref, vn_ref, o_ref, m_ref, l_ref, acc_ref):
    npg = len(k_refs)

    @pl.when(pg == 0)
    def _():
        m_ref[...] = jnp.full_like(m_ref, NEG_BIG)
        l_ref[...] = jnp.zeros_like(l_ref)
        acc_ref[...] = jnp.zeros_like(acc_ref)

    nh, e, page = k_refs[0].shape[2:]
    d = nh * e
    row = lax.broadcasted_iota(jnp.int32, (nh, d), 0)
    lane = lax.broadcasted_iota(jnp.int32, (nh, d), 1)
    own = (lane >= row * e) & (lane < (row + 1) * e)
    q_bd = jnp.where(own, q_ref[0], 0.0)
    q_b = q_bd.astype(BF16)
    vals = {}

    def scores():
        vals["s"] = [_nn(q_b, k_refs[t][0, 0].reshape(d, page).astype(BF16)) + bias_ref[0, t]
                     for t in range(npg)]

    def combine():
        s = vals["s"]
        m_prev = m_ref[...]
        m_new = m_prev
        for t in range(npg):
            m_new = jnp.maximum(m_new, jnp.max(s[t], axis=1, keepdims=True))
        alpha = jnp.exp(m_prev - m_new)
        l_new = alpha * l_ref[...]
        acc = alpha * acc_ref[...]
        for t in range(npg):
            pr = jnp.exp(s[t] - m_new)
            l_new = l_new + jnp.sum(pr, axis=1, keepdims=True)
            acc = acc + _nt(pr.astype(BF16), v_refs[t][0, 0].reshape(d, page).astype(BF16))
        l_ref[...] = l_new
        acc_ref[...] = acc
        m_ref[...] = m_new
        vals.update(m=m_new, l=l_new, acc=acc)

    def finish():
        m_new, l_new, acc = vals["m"], vals["l"], vals["acc"]
        s_new = jnp.sum(q_bd * kn_ref[0], axis=1, keepdims=True)
        m_fin = jnp.maximum(m_new, s_new)
        a_old = jnp.exp(m_new - m_fin)
        p_new = jnp.exp(s_new - m_fin)
        l_fin = l_new * a_old + p_new
        full = (acc * a_old + p_new * vn_ref[0]) / l_fin
        o_ref[0] = jnp.sum(jnp.where(own, full, 0.0), axis=0, keepdims=True)

    return scores, combine, finish


def _fox_decode(layer, q, k_new, v_new, cache_kt, cache_vt, bias, page_table):
    n, d = q.shape
    n_pages = page_table.shape[1]
    nh, e, page = cache_kt.shape[2:]
    npg = _pick_tile(n_pages, (8, 4, 2, 1))

    def kv_spec(t):
        return pl.BlockSpec((1, 1, nh, e, page), lambda b, pg, pt: (layer, pt[b, pg * npg + t], 0, 0, 0))

    row_spec = pl.BlockSpec((1, 1, d), lambda b, pg, pt: (b, 0, 0))
    grid_spec = pltpu.PrefetchScalarGridSpec(
        num_scalar_prefetch=1,
        grid=(n, n_pages // npg),
        in_specs=[row_spec] + [kv_spec(t) for t in range(npg)] * 2
                 + [pl.BlockSpec((1, npg, nh, page), lambda b, pg, pt: (b, pg, 0, 0)), row_spec, row_spec],
        out_specs=row_spec,
        scratch_shapes=[pltpu.VMEM((nh, 1), F32), pltpu.VMEM((nh, 1), F32), pltpu.VMEM((nh, d), F32)],
    )
    o = pl.pallas_call(
        functools.partial(_fox_decode_kernel, npg=npg),
        grid_spec=grid_spec,
        out_shape=jax.ShapeDtypeStruct((n, 1, d), F32),
        compiler_params=pltpu.CompilerParams(
            dimension_semantics=("parallel", "arbitrary"), vmem_limit_bytes=VMEM_LIMIT),
        name="fox_decode",
    )(page_table, q.reshape(n, 1, d), *([cache_kt] * npg), *([cache_vt] * npg),
      bias, k_new.reshape(n, 1, d), v_new.reshape(n, 1, d))
    return o.reshape(n, d)


def _fox_attn_fused_kernel(pt_ref, fq_ref, fk_ref, fv_ref, fcq_ref, fck_ref, dq_ref, *refs, npg, tk, n_groups):
    k_refs = refs[:npg]
    v_refs = refs[npg:2 * npg]
    (bias_ref, kn_ref, vn_ref, o_ref, do_ref,
     s_ref, m_ref, l_ref, acc_ref, dm_ref, dl_ref, dacc_ref) = refs[2 * npg:]
    p, qi = pl.program_id(1), pl.program_id(2)
    step = (pl.program_id(0) * pl.num_programs(1) + p) * pl.num_programs(2) + qi
    decode = _decode_phases(step % n_groups, dq_ref, k_refs, v_refs, bias_ref, kn_ref, vn_ref,
                            do_ref, dm_ref, dl_ref, dacc_ref)
    _flash_body(p, qi, fq_ref, fk_ref, fv_ref, fcq_ref, fck_ref, o_ref, s_ref, m_ref, l_ref, acc_ref, tk=tk,
                side_work=decode)


def _fox_attention(layer, q_b, k_b, v_b, c, dq, dk_new, dv_new, cache_kt, cache_vt, bias, page_table, tile=512):
    bsz, t, d = q_b.shape
    n = dq.shape[0]
    nh, e, page = cache_kt.shape[2:]
    n_pages = page_table.shape[1]
    npair = d // LANES
    tq = tk = _pick_tile(t, (tile, 256, 128))
    nq = t // tq
    npg = _pick_tile(n_pages, (8, 4, 2, 1))
    n_groups = n_pages // npg
    if bsz * npair * nq != n * n_groups:
        return (_fox_flash(q_b, k_b, v_b, c, tile),
                _fox_decode(layer, dq, dk_new, dv_new, cache_kt, cache_vt, bias, page_table))
    pieces = _bias_pieces(c)

    def seq_of(bi, p, qi):
        return ((bi * npair + p) * nq + qi) // n_groups

    def grp_of(bi, p, qi):
        return ((bi * npair + p) * nq + qi) % n_groups

    def kv_spec(j):
        return pl.BlockSpec((1, 1, nh, e, page),
                            lambda bi, p, qi, pt: (layer, pt[seq_of(bi, p, qi), grp_of(bi, p, qi) * npg + j], 0, 0, 0))

    q_map = lambda bi, p, qi, pt: (bi, qi, p)
    seq_map = lambda bi, p, qi, pt: (bi, 0, p)
    row_spec = pl.BlockSpec((1, 1, d), lambda bi, p, qi, pt: (seq_of(bi, p, qi), 0, 0))
    grid_spec = pltpu.PrefetchScalarGridSpec(
        num_scalar_prefetch=1,
        grid=(bsz, npair, nq),
        in_specs=[pl.BlockSpec((1, tq, LANES), q_map),
                  pl.BlockSpec((1, t, LANES), seq_map),
                  pl.BlockSpec((1, t, LANES), seq_map),
                  pl.BlockSpec((1, tq, 3 * nh), lambda bi, p, qi, pt: (bi, qi, 0)),
                  pl.BlockSpec((1, t, 3 * nh), lambda bi, p, qi, pt: (bi, 0, 0)),
                  row_spec]
                 + [kv_spec(j) for j in range(npg)] * 2
                 + [pl.BlockSpec((1, npg, nh, page),
                                 lambda bi, p, qi, pt: (seq_of(bi, p, qi), grp_of(bi, p, qi), 0, 0)),
                    row_spec, row_spec],
        out_specs=[pl.BlockSpec((1, tq, LANES), q_map), row_spec],
        scratch_shapes=[pltpu.VMEM((2, 2, tq, tk), F32),
                        pltpu.VMEM((2, tq, LANES), F32), pltpu.VMEM((2, tq, LANES), F32),
                        pltpu.VMEM((tq, LANES), F32),
                        pltpu.VMEM((nh, 1), F32), pltpu.VMEM((nh, 1), F32), pltpu.VMEM((nh, d), F32)],
    )
    o, do = pl.pallas_call(
        functools.partial(_fox_attn_fused_kernel, npg=npg, tk=tk, n_groups=n_groups),
        grid_spec=grid_spec,
        out_shape=[jax.ShapeDtypeStruct((bsz, t, d), F32), jax.ShapeDtypeStruct((n, 1, d), F32)],
        compiler_params=pltpu.CompilerParams(
            dimension_semantics=("arbitrary", "arbitrary", "arbitrary"), vmem_limit_bytes=VMEM_LIMIT),
        name="fox_attention",
    )(page_table, q_b, k_b, v_b, pieces, pieces, dq.reshape(n, 1, d),
      *([cache_kt] * npg), *([cache_vt] * npg), bias, dk_new.reshape(n, 1, d), dv_new.reshape(n, 1, d))
    return o, do.reshape(n, d)


def _rmsnorm(x, g):
    xf = x.astype(F32)
    y = xf * lax.rsqrt(jnp.mean(xf * xf, axis=-1, keepdims=True) + RMS_EPS)
    return (y * g.astype(F32)).astype(x.dtype)


def _rwkv_proj_kernel(x_ref, hp_ref, g_ref, mu_ref, w_ref, w0_ref, w1_ref, w2_ref, a0_ref, a1_ref, a2_ref,
                      kk_ref, ka_ref, seg_ref, segt_ref,
                      r_ref, lw_ref, kf_ref, v_ref, nkk_ref, kka_ref, gate_ref, carry_ref,
                      *, tiles_per_seq):
    x = x_ref[...]
    tm = x.shape[0]
    g = g_ref[...]
    h = x * lax.rsqrt(jnp.mean(x * x, axis=-1, keepdims=True) + RMS_EPS) * g
    if tiles_per_seq == 0:
        h_prev = hp_ref[...]
    else:
        first = pl.program_id(0) % tiles_per_seq == 0
        before = jnp.where(first, hp_ref[0], carry_ref[...])
        row = lax.broadcasted_iota(jnp.int32, x.shape, 0)
        h_prev = jnp.where(row == 0, before, pltpu.roll(h, 1, 0))
        carry_ref[...] = h[tm - 1:tm, :]
    xx = h_prev - h
    mix = lambda s: (h + xx * mu_ref[s:s + 1, :]).astype(BF16)
    k = _nn(mix(1), w_ref[1])
    w_mid = _nn(mix(4), w1_ref[...])
    a_mid = _nn(mix(5), a1_ref[...])
    r_ref[...] = _nn(mix(0), w_ref[0])
    kk = k * kk_ref[...]
    kk_sq = _seg_sum(kk * kk, seg_ref[...])
    z = w0_ref[...] + _nn(jnp.tanh(w_mid).astype(BF16), w2_ref[...])
    a_pre = _nn(a_mid.astype(BF16), a2_ref[...])
    v_ref[...] = _nn(mix(2), w_ref[2])
    kk = kk * _seg_bcast(1.0 / jnp.maximum(jnp.sqrt(kk_sq), 1e-12), segt_ref[...])
    gate_ref[...] = _nn(mix(3), w_ref[3])
    lw_ref[...] = -jnp.exp(jnp.minimum(z, 0.0) - jnp.log(1.0 + jnp.exp(-jnp.abs(z))) - 0.5)
    a = jax.nn.sigmoid(a0_ref[...] + a_pre)
    kf_ref[...] = k * (1.0 + (a - 1.0) * ka_ref[...])
    nkk_ref[...] = -kk
    kka_ref[...] = kk * a


def _rwkv_out_kernel(o_ref, r_ref, kf_ref, v_ref, gate_ref, x_ref, lw_ref, lb_ref, rk_ref, w_ref,
                     seg_ref, segt_ref, y_ref):
    seg = seg_ref[...]
    seg_t = segt_ref[...]
    tm, d = o_ref.shape
    inv_e = 1.0 / (d // seg.shape[1])
    n_groups = 2 if tm % 16 == 0 else 1
    rows = [slice(i * (tm // n_groups), (i + 1) * (tm // n_groups)) for i in range(n_groups)]
    o = [o_ref[rs, :] for rs in rows]
    o_sum = [_seg_sum(x, seg) for x in o]
    rk_sum = [_seg_sum(r_ref[rs, :] * kf_ref[rs, :] * rk_ref[...], seg) for rs in rows]
    cen = [x - _seg_bcast(s * inv_e, seg_t) for x, s in zip(o, o_sum)]
    var = [_seg_sum(c * c, seg) * inv_e for c in cen]
    bonus = [_seg_bcast(s, seg_t) * v_ref[rs, :] for s, rs in zip(rk_sum, rows)]
    gn = [c * _seg_bcast(lax.rsqrt(vr + LNX_EPS), seg_t) * lw_ref[...] + lb_ref[...] for c, vr in zip(cen, var)]
    for rs, gn_i, bonus_i in zip(rows, gn, bonus):
        g = gate_ref[rs, :]
        y = ((gn_i + bonus_i) * (g * jax.nn.sigmoid(g))).astype(BF16)
        y_ref[rs, :] = x_ref[rs, :] + _nn(y, w_ref[...])


def _rwkv_layer(x, h_prev0, states, layer, prm, nh):
    (norm, mu, w_in, w0, w1, w2, a0, a1, a2, k_k, k_a, r_k, lnx_w, lnx_b, w_out) = prm
    bsz, t, d = x.shape
    e = d // nh
    m = bsz * t
    seg, seg_t = _seg_mats(d, nh)
    row = lambda z: z.reshape(1, -1).astype(F32)
    full = lambda arr: pl.BlockSpec(arr.shape, lambda i: (0,) * arr.ndim)
    x2 = x.reshape(m, d)
    if t == 1:
        tm = _pick_tile(m, (256, 128))
        tiles_per_seq = 0
        hp = h_prev0.astype(F32)
        hp_spec = pl.BlockSpec((tm, d), lambda i: (i, 0))
    else:
        tm = _pick_tile(t, (256, 128))
        tiles_per_seq = t // tm
        hp = h_prev0.astype(F32).reshape(bsz, 1, d)
        hp_spec = pl.BlockSpec((1, 1, d), lambda i: (i // tiles_per_seq, 0, 0))
    tile = pl.BlockSpec((tm, d), lambda i: (i, 0))
    consts = [row(norm), mu.astype(F32), w_in.astype(BF16), row(w0), w1.astype(BF16), w2.astype(BF16),
              row(a0), a1.astype(BF16), a2.astype(BF16), row(k_k), row(k_a), seg, seg_t]
    r, lw, kf, v, neg_kk, kk_a, gate = pl.pallas_call(
        functools.partial(_rwkv_proj_kernel, tiles_per_seq=tiles_per_seq),
        grid=(m // tm,),
        in_specs=[tile, hp_spec] + [full(c) for c in consts],
        out_specs=[tile] * 7,
        out_shape=[jax.ShapeDtypeStruct((m, d), F32)] * 7,
        scratch_shapes=[pltpu.VMEM((1, d), F32)],
        compiler_params=pltpu.CompilerParams(
            dimension_semantics=("arbitrary",), vmem_limit_bytes=VMEM_LIMIT),
        name="rwkv_proj",
    )(x2, hp, *consts)

    if states is None:
        seq = lambda z: z.reshape(bsz, t, d)
        o, zt = _wkv_chunked(seq(r), seq(lw), seq(kf), seq(v), seq(neg_kk), seq(kk_a))
        zt = zt.reshape(bsz, d // LANES, 2, e, 2, e)
        s_fin = jnp.stack([zt[:, :, 0, :, 0, :], zt[:, :, 1, :, 1, :]], axis=2).reshape(bsz, nh, e, e)
    else:
        o, s_fin = _wkv_step(states, layer, r, lw, kf, v, neg_kk, kk_a)

    tmo = _pick_tile(m, (256, 128))
    tile_o = pl.BlockSpec((tmo, d), lambda i: (i, 0))
    consts_o = [row(lnx_w), row(lnx_b), row(r_k), w_out.astype(BF16), seg, seg_t]
    x_new = pl.pallas_call(
        _rwkv_out_kernel,
        grid=(m // tmo,),
        in_specs=[tile_o] * 6 + [full(c) for c in consts_o],
        out_specs=tile_o,
        out_shape=jax.ShapeDtypeStruct((m, d), F32),
        compiler_params=pltpu.CompilerParams(
            dimension_semantics=("parallel",), vmem_limit_bytes=VMEM_LIMIT),
        name="rwkv_out",
    )(o.reshape(m, d), r, kf, v, gate, x2, *consts_o)
    h_last = _rmsnorm(x[:, -1, :], norm)
    return x_new.reshape(bsz, t, d), s_fin, h_last


def _split2(x):
    hi = x.astype(BF16)
    return hi, (x - hi.astype(F32)).astype(BF16)


def _seg_sum(x, seg):
    hi, lo = _split2(x)
    return _nn(hi, seg) + _nn(lo, seg)


def _seg_bcast(y, seg_t):
    hi, lo = _split2(y)
    return _nn(hi, seg_t) + _nn(lo, seg_t)


def _seg_mats(d, nh):
    lane_head = jnp.arange(d, dtype=jnp.int32) // (d // nh)
    seg = (lane_head[:, None] == jnp.arange(nh, dtype=jnp.int32)[None, :]).astype(BF16)
    return seg, seg.T


def _fox_proj_kernel(x_ref, g_ref, w_ref, wf_ref, bf_ref, qn_ref, kn_ref, seg_ref, segt_ref,
                     q_ref, k_ref, v_ref, gate_ref, lf_ref, *attn_refs, q_scale):
    x = x_ref[...]
    d = x.shape[1]
    hb = (x * lax.rsqrt(jnp.mean(x * x, axis=-1, keepdims=True) + RMS_EPS) * g_ref[...]).astype(BF16)
    seg = seg_ref[...]
    seg_t = segt_ref[...]
    inv_e = 1.0 / (d // seg.shape[1])

    q = _nn(hb, w_ref[:, 0:d])
    k = _nn(hb, w_ref[:, d:2 * d])
    q_ms = _seg_sum(q * q, seg)
    v = _nn(hb, w_ref[:, 2 * d:3 * d])
    k_ms = _seg_sum(k * k, seg)
    q = q * _seg_bcast(lax.rsqrt(q_ms * inv_e + RMS_EPS), seg_t) * qn_ref[...]
    gate_ref[...] = _nn(hb, w_ref[:, 3 * d:4 * d])
    k = k * _seg_bcast(lax.rsqrt(k_ms * inv_e + RMS_EPS), seg_t) * kn_ref[...]
    k_ref[...] = k
    v_ref[...] = v
    if attn_refs:
        kb_ref, vb_ref = attn_refs
        q_ref[...] = (q * q_scale).astype(BF16)
        kb_ref[...] = k.astype(BF16)
        vb_ref[...] = v.astype(BF16)
    else:
        q_ref[...] = q
    f = _nn(hb, wf_ref[...]) + bf_ref[...]
    lf_ref[...] = jnp.minimum(f, 0.0) - jnp.log(1.0 + jnp.exp(-jnp.abs(f)))


def _fox_project(x, norm, w_in, b_f, qn_g, kn_g, nh, attn_q_scale=None):
    bsz, t, d = x.shape
    m = bsz * t
    tm = _pick_tile(m, (256, 128))
    seg, seg_t = _seg_mats(d, nh)
    row = lambda z: z.reshape(1, -1).astype(F32)
    full = lambda a: pl.BlockSpec(a.shape, lambda i: (0,) * a.ndim)
    consts = [row(norm), w_in[:, :4 * d].astype(BF16), w_in[:, 4 * d:].astype(BF16), row(b_f),
              row(jnp.tile(qn_g, nh)), row(jnp.tile(kn_g, nh)), seg, seg_t]
    tile = pl.BlockSpec((tm, d), lambda i: (i, 0))
    n_attn = 0 if attn_q_scale is None else 2
    q_dtype = F32 if attn_q_scale is None else BF16
    outs = pl.pallas_call(
        functools.partial(_fox_proj_kernel, q_scale=attn_q_scale),
        grid=(m // tm,),
        in_specs=[tile] + [full(a) for a in consts],
        out_specs=[tile] * 4 + [pl.BlockSpec((tm, nh), lambda i: (i, 0))] + [tile] * n_attn,
        out_shape=([jax.ShapeDtypeStruct((m, d), q_dtype)] + [jax.ShapeDtypeStruct((m, d), F32)] * 3
                   + [jax.ShapeDtypeStruct((m, nh), F32)] + [jax.ShapeDtypeStruct((m, d), BF16)] * n_attn),
        compiler_params=pltpu.CompilerParams(
            dimension_semantics=("parallel",), vmem_limit_bytes=VMEM_LIMIT),
        name="fox_proj",
    )(x.reshape(m, d), *consts)
    q, k, v, gate, logf = outs[:5]
    seq = lambda z: z.reshape(bsz, t, d)
    return seq(q), seq(k), seq(v), gate, logf.reshape(bsz, t, nh), tuple(seq(z) for z in outs[5:])


def _gate_out_kernel(o_ref, gate_ref, x_ref, w_ref, y_ref):
    g = gate_ref[...]
    y = (o_ref[...] * (g * jax.nn.sigmoid(g))).astype(BF16)
    y_ref[...] = x_ref[...] + _nn(y, w_ref[...])


def _fox_finish(x, o, gate, w_out):
    bsz, t, d = x.shape
    m = bsz * t
    tm = _pick_tile(m, (512, 256, 128))
    tile = pl.BlockSpec((tm, d), lambda i: (i, 0))
    y = pl.pallas_call(
        _gate_out_kernel,
        grid=(m // tm,),
        in_specs=[tile, tile, tile, pl.BlockSpec((d, d), lambda i: (0, 0))],
        out_specs=tile,
        out_shape=jax.ShapeDtypeStruct((m, d), F32),
        compiler_params=pltpu.CompilerParams(
            dimension_semantics=("parallel",), vmem_limit_bytes=VMEM_LIMIT),
        name="gate_out",
    )(o.reshape(m, d), gate, x.reshape(m, d), w_out.astype(BF16))
    return y.reshape(bsz, t, d)


def kernel(x_prompt, x_sample, state_wkv, state_shift, cache_k, cache_v, cache_logf, page_table,
           norm_a, mu_a, w_in_a, w0_a, w1_a, w2_a, a0_a, a1_a, a2_a, kk_a, ka_a, rk_a, lnx_w_a, lnx_b_a, w_out_a,
           norm_b, w_in_b, bf_b, qn_b, kn_b, w_out_b):
    bsz, t, d = x_prompt.shape
    nb, ts, _ = x_sample.shape
    assert ts == 1, "the sample group carries one new token per sequence"
    nh, e = rk_a.shape[1], rk_a.shape[2]
    assert d == nh * e and 2 * e == LANES and t % WKV_CHUNK == 0
    n_layers_a = norm_a.shape[0]
    n_layers_b = norm_b.shape[0]
    depth = n_layers_a + n_layers_b
    n_pool, page = cache_k.shape[1], cache_k.shape[2]
    n_pages = page_table.shape[1]
    past = n_pages * page
    scale = e ** -0.5
    ckt = jnp.transpose(cache_k, (0, 1, 3, 4, 2))
    cvt = jnp.transpose(cache_v, (0, 1, 3, 4, 2))

    xp, xs = x_prompt, x_sample
    kp_l, vp_l, fp_l, sp_l, hp_l = [], [], [], [], []
    ks_l, vs_l, fs_l, hs_l = [], [], [], []
    states_s = state_wkv.astype(F32)
    for i in range(depth):
        j = i // 2
        if i % 2 == 0:
            prm = (norm_a[j], mu_a[j], w_in_a[j], w0_a[j], w1_a[j], w2_a[j], a0_a[j], a1_a[j], a2_a[j],
                   kk_a[j], ka_a[j], rk_a[j], lnx_w_a[j], lnx_b_a[j], w_out_a[j])
            xp, s_p, l_p = _rwkv_layer(xp, jnp.zeros((bsz, d), xp.dtype), None, j, prm, nh)
            xs, states_s, l_s = _rwkv_layer(xs, state_shift[j], states_s, j, prm, nh)
            sp_l.append(s_p); hp_l.append(l_p); hs_l.append(l_s)
        else:
            qp_b, kp, vp, gp, lfp, kv_b = _fox_project(xp, norm_b[j], w_in_b[j], bf_b[j], qn_b[j], kn_b[j], nh,
                                                        attn_q_scale=scale * LOG2E)
            cp = jnp.cumsum(lfp, axis=1)
            qs, ksn, vsn, gs, lfs, _ = _fox_project(xs, norm_b[j], w_in_b[j], bf_b[j], qn_b[j], kn_b[j], nh)
            f_past = cache_logf[j][page_table].reshape(nb, past, nh).astype(F32)
            c_all = jnp.cumsum(jnp.concatenate([f_past, lfs], axis=1), axis=1)
            bias = c_all[:, past:, :] - c_all[:, :past, :]
            bias_t = bias.reshape(nb, n_pages, page, nh).transpose(0, 1, 3, 2)
            op, osm = _fox_attention(j, qp_b, *kv_b, cp, (qs * scale).reshape(nb, d), ksn.reshape(nb, d),
                                     vsn.reshape(nb, d), ckt, cvt, bias_t, page_table)
            xp = _fox_finish(xp, op, gp, w_out_b[j])
            xs = _fox_finish(xs, osm.reshape(nb, 1, d), gs, w_out_b[j])
            hd4 = lambda z, n_, t_: z.reshape(n_, t_, nh, e)
            kp_l.append(hd4(kp, bsz, t)); vp_l.append(hd4(vp, bsz, t)); fp_l.append(lfp)
            ks_l.append(hd4(ksn, nb, ts)); vs_l.append(hd4(vsn, nb, ts)); fs_l.append(lfs)
    return (xp, xs,
            jnp.stack(kp_l), jnp.stack(vp_l), jnp.stack(fp_l), jnp.stack(sp_l), jnp.stack(hp_l),
            jnp.stack(ks_l), jnp.stack(vs_l), jnp.stack(fs_l), states_s, jnp.stack(hs_l))
```

```python
import functools

import jax
import jax.numpy as jnp
from jax import lax
from jax.experimental import pallas as pl
from jax.experimental.pallas import tpu as pltpu

F32 = jnp.float32
BF16 = jnp.bfloat16

LANES = 128
RMS_EPS = 1e-6
LNX_EPS = 64e-5
NEG_BIG = -1e30
LOG2E = 1.4426950408889634
WKV_CHUNK = 64
WKV_CHUNKS_PER_STEP = (4, 2, 1)
VMEM_LIMIT = 48 * 1024 * 1024


def _nt(x, y, precision=None):
    return lax.dot_general(x, y, (((1,), (1,)), ((), ())), precision=precision,
                           preferred_element_type=F32)


def _tn(x, y, precision=None):
    return lax.dot_general(x, y, (((0,), (0,)), ((), ())), precision=precision,
                           preferred_element_type=F32)


def _nn(x, y, precision=None):
    return jnp.dot(x, y, precision=precision, preferred_element_type=F32)


def _pick_tile(n, candidates):
    for c in candidates:
        if n % c == 0:
            return c
    return n


def _wkv_chunk_kernel(r_ref, lw_ref, k_ref, v_ref, a_ref, b_ref, o_ref, zt_ref):
    c = pl.program_id(1)

    @pl.when(c == 0)
    def _():
        zt_ref[...] = jnp.zeros_like(zt_ref)

    L = WKV_CHUNK
    n_chunks = r_ref.shape[1] // L
    n = 2 * L
    half = LANES // 2
    npair = r_ref.shape[2] // LANES

    ti = lax.broadcasted_iota(jnp.int32, (L, L), 0)
    tj = lax.broadcasted_iota(jnp.int32, (L, L), 1)
    tri = (ti >= tj).astype(BF16)
    head0 = lax.broadcasted_iota(jnp.int32, (L, LANES), 1) < half
    i = lax.broadcasted_iota(jnp.int32, (n, n), 0)
    j = lax.broadcasted_iota(jnp.int32, (n, n), 1)
    strict = i > j
    incl = i >= j
    eye = jnp.where(i == j, 1.0, 0.0)
    first = (i >> 1) == (j >> 1)
    levels = []
    lvl = 1
    while (2 << lvl) <= L:
        levels.append(((i >> (lvl + 1)) == (j >> (lvl + 1))) & ((i >> lvl) != (j >> lvl)))
        lvl += 1

    def stack(x):
        return jnp.concatenate([jnp.where(head0, x, 0.0), jnp.where(head0, 0.0, x)], axis=0)

    pairs = range(npair)
    units = range(n_chunks * npair)
    sls = [(slice((q // npair) * L, (q // npair + 1) * L), slice((q % npair) * LANES, (q % npair + 1) * LANES))
           for q in units]
    ar, bk, bk_h, vs, p_last = [], [], [], [], []
    for q in units:
        r, lw, k, v, a, b = (ref[0, sls[q][0], sls[q][1]] for ref in (r_ref, lw_ref, k_ref, v_ref, a_ref, b_ref))
        lw1 = lw.astype(BF16)
        rem = lw - lw1.astype(F32)
        lw2 = rem.astype(BF16)
        lw3 = (rem - lw2.astype(F32)).astype(BF16)
        cum = _nn(tri, lw1) + (_nn(tri, lw2) + _nn(tri, lw3))
        p_inc = jnp.exp(cum)
        p_exc = jnp.exp(cum - lw)
        p_inv = jnp.exp(-cum)
        pl_ = p_inc[L - 1:L, :]
        bk_f = jnp.concatenate([stack(b * p_inv), stack(k * p_inv)], axis=0)
        ar.append(jnp.concatenate([stack(a * p_exc), stack(r * p_inc)], axis=0).astype(BF16))
        bk.append(bk_f.astype(BF16))
        bk_h.append((bk_f * pl_).astype(BF16))
        vs.append(stack(v).astype(BF16))
        p_last.append(pl_)

    g = [_nt(ar[q], bk[q]) for q in units]
    a_ab = [jnp.where(strict, g[q][:n, :n], 0.0) for q in units]
    ak_rk = [jnp.concatenate([jnp.where(strict, g[q][:n, n:], 0.0),
                              jnp.where(incl, g[q][n:, n:], 0.0)], axis=0).astype(BF16) for q in units]
    a_rb = [jnp.where(incl, g[q][n:, :n], 0.0).astype(BF16) for q in units]

    def compress(full, size):
        return functools.reduce(lambda x, y: x + y, [full[i * size:(i + 1) * size] for i in range(n // size)])

    def expand(comp, size):
        lane_blk = lax.broadcasted_iota(jnp.int32, (size, n), 1) >> (size.bit_length() - 1)
        return jnp.concatenate([jnp.where(lane_blk == i, comp, 0.0) for i in range(n // size)], axis=0)

    sub = L // 2
    t_full = [eye + jnp.where(first, a_ab[q], 0.0) for q in units]
    t_comp = [compress(t_full[q], sub) for q in units]
    for off in levels[:-1]:
        x = [_nn(t_comp[q].astype(BF16), jnp.where(off, a_ab[q], 0.0).astype(BF16)).astype(BF16) for q in units]
        t_comp = [t_comp[q] + _nn(x[q], t_full[q].astype(BF16)) for q in units]
        t_full = [expand(t_comp[q], sub) for q in units]
    t_comp = [compress(t_full[q], L) for q in units]
    x = [_nn(t_comp[q].astype(BF16), jnp.where(levels[-1], a_ab[q], 0.0).astype(BF16)).astype(BF16) for q in units]
    t_comp = [(t_comp[q] + _nn(x[q], t_full[q].astype(BF16))).astype(BF16) for q in units]

    zt = [zt_ref[0, p] for p in pairs]
    for c in range(n_chunks):
        qs = [c * npair + p for p in pairs]
        y = [_nt(ar[q], zt[p].astype(BF16)) + _nn(ak_rk[q], vs[q]) for p, q in zip(pairs, qs)]
        u_b = [stack(_nn(t_comp[q], y[p][:n].astype(BF16))).astype(BF16) for p, q in zip(pairs, qs)]
        for p, q in zip(pairs, qs):
            o_st = y[p][n:] + _nn(a_rb[q], u_b[p])
            o_ref[0, sls[q][0], sls[q][1]] = o_st[:L] + o_st[L:]
            zt[p] = zt[p] * p_last[q] + _tn(jnp.concatenate([u_b[p], vs[q]], axis=0), bk_h[q])
    for p in pairs:
        zt_ref[0, p] = zt[p]


def _wkv_chunked(r, lw, k, v, a, b):
    bsz, t, d = r.shape
    npair = d // LANES
    tb = _pick_tile(t, tuple(WKV_CHUNK * c for c in WKV_CHUNKS_PER_STEP))
    spec = pl.BlockSpec((1, tb, d), lambda bi, c: (bi, c, 0))
    return pl.pallas_call(
        _wkv_chunk_kernel,
        grid=(bsz, t // tb),
        in_specs=[spec] * 6,
        out_specs=[spec, pl.BlockSpec((1, npair, LANES, LANES), lambda bi, c: (bi, 0, 0, 0))],
        out_shape=[jax.ShapeDtypeStruct((bsz, t, d), F32),
                   jax.ShapeDtypeStruct((bsz, npair, LANES, LANES), F32)],
        compiler_params=pltpu.CompilerParams(
            dimension_semantics=("parallel", "arbitrary"), vmem_limit_bytes=VMEM_LIMIT),
        name="wkv_chunked",
    )(r, lw, k, v, a, b)


def _wkv_step_kernel(s_ref, r_ref, lw_ref, k_ref, v_ref, a_ref, b_ref, so_ref, o_ref):
    _, bb, nh, e, _ = s_ref.shape
    half = LANES // 2
    ii = lax.broadcasted_iota(jnp.int32, (e, LANES), 0)
    jj = lax.broadcasted_iota(jnp.int32, (e, LANES), 1)
    eye_lo = ii == jj
    eye_hi = ii + half == jj

    for i in range(bb):
        units = []
        for h in range(nh):
            p, par = divmod(h, 2)
            sl = slice(p * LANES, (p + 1) * LANES)
            r, lw, k, v, a, b = (ref[i:i + 1, sl] for ref in (r_ref, lw_ref, k_ref, v_ref, a_ref, b_ref))
            w = jnp.exp(lw)
            if par == 0:
                rh, wh, kh, ah, bh = (z[:, :e] for z in (r, w, k, a, b))
            else:
                rh, wh, kh, ah, bh = (pltpu.roll(z, half, 1)[:, :e] for z in (r, w, k, a, b))
            s = s_ref[0, i, h]
            v_col = jnp.sum(jnp.where(eye_hi if par else eye_lo, v, 0.0), axis=1, keepdims=True)
            sa = jnp.sum(s * ah, axis=1, keepdims=True)
            units.append((s, rh, wh, kh, bh, v_col, sa))
        o_cols = []
        for h, (s, rh, wh, kh, bh, v_col, sa) in enumerate(units):
            s_new = s * wh + sa * bh + v_col * kh
            so_ref[0, i, h] = s_new
            o_cols.append(jnp.sum(s_new * rh, axis=1, keepdims=True))
        for p in range(nh // 2):
            o_ref[i:i + 1, p * LANES:(p + 1) * LANES] = (
                jnp.sum(jnp.where(eye_lo, o_cols[2 * p], 0.0), axis=0, keepdims=True)
                + jnp.sum(jnp.where(eye_hi, o_cols[2 * p + 1], 0.0), axis=0, keepdims=True))


def _wkv_step(states, layer, r, lw, k, v, a, b):
    _, n, h, e, _ = states.shape
    d = h * e
    bb = _pick_tile(n, (8,))
    s_spec = pl.BlockSpec((1, bb, h, e, e), lambda i: (layer, i, 0, 0, 0))
    row_spec = pl.BlockSpec((bb, d), lambda i: (i, 0))
    states, o = pl.pallas_call(
        _wkv_step_kernel,
        grid=(n // bb,),
        in_specs=[s_spec] + [row_spec] * 6,
        out_specs=[s_spec, row_spec],
        out_shape=[jax.ShapeDtypeStruct(states.shape, F32), jax.ShapeDtypeStruct((n, d), F32)],
        input_output_aliases={0: 0},
        compiler_params=pltpu.CompilerParams(
            dimension_semantics=("parallel",), vmem_limit_bytes=VMEM_LIMIT),
        name="wkv_step",
    )(states, r, lw, k, v, a, b)
    return o, states


def _fox_flash_kernel(*refs, tk):
    _flash_body(pl.program_id(1), pl.program_id(2), *refs, tk=tk)


def _flash_body(p, qi, q_ref, k_ref, v_ref, cq_ref, ck_ref, o_ref, s_ref, m_ref, l_ref, acc_ref, *, tk,
                side_work=()):
    tq = q_ref.shape[1]
    half = LANES // 2
    n_piece_rows = cq_ref.shape[2]
    nh = n_piece_rows // 3
    heads = range(2)

    q_head0 = lax.broadcasted_iota(jnp.int32, (tq, LANES), 1) < half
    k_head0 = lax.broadcasted_iota(jnp.int32, (tk, LANES), 1) < half

    prow = lax.broadcasted_iota(jnp.int32, (n_piece_rows, LANES), 0)
    plane = lax.broadcasted_iota(jnp.int32, (n_piece_rows, LANES), 1)
    piece, head = prow >> (nh.bit_length() - 1), prow & (nh - 1)
    base = jnp.where(head == 2 * p, half, jnp.where(head == 2 * p + 1, 0, -LANES))
    place_q = jnp.where(plane == base + piece, 1.0, 0.0).astype(BF16)
    place_k = jnp.where(plane == base + piece + 3, -1.0, 0.0).astype(BF16)
    slot = lax.broadcasted_iota(jnp.int32, (1, LANES), 1) & (half - 1)
    ones_q = jnp.where((slot >= 3) & (slot < 6), 1.0, 0.0)
    ones_k = jnp.where(slot < 3, 1.0, 0.0)

    q = q_ref[0]
    q_aug = (_nn(cq_ref[0], place_q) + ones_q).astype(BF16)
    q_ops = (jnp.where(q_head0, q, q_aug), jnp.where(q_head0, q_aug, q))

    m_ref[...] = jnp.full_like(m_ref, NEG_BIG)
    l_ref[...] = jnp.zeros_like(l_ref)
    acc_ref[...] = jnp.zeros_like(acc_ref)

    def rows(kj):
        return slice(kj * tk, (kj + 1) * tk)

    def scores(kj, slot_idx):
        k = k_ref[0, rows(kj), :]
        k_aug = (_nn(ck_ref[0, rows(kj), :], place_k) + ones_k).astype(BF16)
        k_ops = (jnp.where(k_head0, k, k_aug), jnp.where(k_head0, k_aug, k))
        for h in heads:
            s_ref[slot_idx, h] = _nt(q_ops[h], k_ops[h])

    def consume(kj, slot_idx, on_diagonal):
        v = v_ref[0, rows(kj), :]
        zero = jnp.zeros_like(v)
        v_own = (jnp.where(k_head0, v, zero), jnp.where(k_head0, zero, v))
        if on_diagonal:
            keep = (lax.broadcasted_iota(jnp.int32, (tq, tk), 0)
                    >= lax.broadcasted_iota(jnp.int32, (tq, tk), 1))
        alpha, pv = [], []
        for h in heads:
            s = s_ref[slot_idx, h]
            if on_diagonal:
                s = jnp.where(keep, s, NEG_BIG)
            m_prev = m_ref[h]
            m_new = jnp.maximum(m_prev, jnp.max(s, axis=1, keepdims=True))
            a = jnp.exp2(m_prev - m_new)
            pr = jnp.exp2(s - jnp.concatenate([m_new] * (tk // LANES), axis=1))
            l_ref[h] = a * l_ref[h] + jnp.sum(pr, axis=1, keepdims=True)
            m_ref[h] = m_new
            alpha.append(a)
            pv.append(_nn(pr.astype(BF16), v_own[h]))
        acc_ref[...] = jnp.where(q_head0, alpha[0], alpha[1]) * acc_ref[...] + (pv[0] + pv[1])

    side = list(side_work) + [lambda: None] * (3 - len(side_work))

    def run(n_before):
        scores(0, 0)
        side[0]()
        for kj in range(n_before):
            scores(kj + 1, (kj + 1) % 2)
            consume(kj, kj % 2, False)
        side[1]()
        consume(n_before, n_before % 2, True)
        side[2]()
        o_ref[0] = acc_ref[...] / jnp.where(q_head0, l_ref[0], l_ref[1])

    for n_before in range(k_ref.shape[1] // tk):
        pl.when(qi == n_before)(functools.partial(run, n_before))


def _bias_pieces(c):
    top16 = lambda z: lax.bitcast_convert_type(
        lax.bitcast_convert_type(z, jnp.uint32) & jnp.uint32(0xFFFF0000), F32)
    c2 = c * LOG2E
    hi = top16(c2)
    mid = top16(c2 - hi)
    return jnp.concatenate([hi, mid, top16(c2 - hi - mid)], axis=-1).astype(BF16)


def _fox_flash(q_b, k_b, v_b, c, tile=512):
    bsz, t, d = q_b.shape
    nh = c.shape[-1]
    assert nh & (nh - 1) == 0, "head count must be a power of two"
    npair = d // LANES
    tq = tk = _pick_tile(t, (tile, 256, 128))
    pieces = _bias_pieces(c)
    q_map = lambda bi, p, qi: (bi, qi, p)
    seq_map = lambda bi, p, qi: (bi, 0, p)
    return pl.pallas_call(
        functools.partial(_fox_flash_kernel, tk=tk),
        grid=(bsz, npair, t // tq),
        in_specs=[pl.BlockSpec((1, tq, LANES), q_map),
                  pl.BlockSpec((1, t, LANES), seq_map),
                  pl.BlockSpec((1, t, LANES), seq_map),
                  pl.BlockSpec((1, tq, 3 * nh), lambda bi, p, qi: (bi, qi, 0)),
                  pl.BlockSpec((1, t, 3 * nh), lambda bi, p, qi: (bi, 0, 0))],
        out_specs=pl.BlockSpec((1, tq, LANES), q_map),
        out_shape=jax.ShapeDtypeStruct((bsz, t, d), F32),
        scratch_shapes=[pltpu.VMEM((2, 2, tq, tk), F32),
                        pltpu.VMEM((2, tq, LANES), F32), pltpu.VMEM((2, tq, LANES), F32),
                        pltpu.VMEM((tq, LANES), F32)],
        compiler_params=pltpu.CompilerParams(
            dimension_semantics=("parallel", "parallel", "parallel"), vmem_limit_bytes=VMEM_LIMIT),
        name="fox_flash",
    )(q_b, k_b, v_b, pieces, pieces)


def _fox_decode_kernel(pt_ref, q_ref, *refs, npg):
    k_refs, v_refs, f_refs = refs[:npg], refs[npg:2 * npg], refs[2 * npg:3 * npg]
    for phase in _decode_phases(pl.program_id(1), q_ref, k_refs, v_refs, f_refs, *refs[3 * npg:]):
        phase()


def _decode_phases(pg, q_ref, k_refs, v_refs, f_refs, tail_ref, kn_ref, vn_ref, o_ref, m_ref, l_ref, acc_ref):
    npg = len(k_refs)

    @pl.when(pg == 0)
    def _():
        m_ref[...] = jnp.full_like(m_ref, NEG_BIG)
        l_ref[...] = jnp.zeros_like(l_ref)
        acc_ref[...] = jnp.zeros_like(acc_ref)

    nh, e, page = k_refs[0].shape[2:]
    d = nh * e
    row = lax.broadcasted_iota(jnp.int32, (nh, d), 0)
    lane = lax.broadcasted_iota(jnp.int32, (nh, d), 1)
    own = (lane >= row * e) & (lane < (row + 1) * e)
    q_bd = jnp.where(own, q_ref[0], 0.0)
    q_b = q_bd.astype(BF16)
    vals = {}

    pos = lax.broadcasted_iota(jnp.int32, (nh, page), 1)
    eye_h = lax.broadcasted_iota(jnp.int32, (nh, nh), 0) == lax.broadcasted_iota(jnp.int32, (nh, nh), 1)

    def bias(t):
        lf = f_refs[t][0, 0]
        suffix = lf
        shift = 1
        while shift < page:
            suffix = suffix + jnp.where(pos < page - shift, pltpu.roll(suffix, page - shift, 1), 0.0)
            shift *= 2
        tail_col = jnp.sum(jnp.where(eye_h, tail_ref[0, t:t + 1, :], 0.0), axis=1, keepdims=True)
        return (suffix - lf) + tail_col

    def scores():
        vals["s"] = [_nn(q_b, k_refs[t][0, 0].reshape(d, page).astype(BF16)) + bias(t)
                     for t in range(npg)]

    def combine():
        s = vals["s"]
        m_prev = m_ref[...]
        m_new = m_prev
        for t in range(npg):
            m_new = jnp.maximum(m_new, jnp.max(s[t], axis=1, keepdims=True))
        alpha = jnp.exp(m_prev - m_new)
        l_new = alpha * l_ref[...]
        acc = alpha * acc_ref[...]
        for t in range(npg):
            pr = jnp.exp(s[t] - m_new)
            l_new = l_new + jnp.sum(pr, axis=1, keepdims=True)
            acc = acc + _nt(pr.astype(BF16), v_refs[t][0, 0].reshape(d, page).astype(BF16))
        l_ref[...] = l_new
        acc_ref[...] = acc
        m_ref[...] = m_new
        vals.update(m=m_new, l=l_new, acc=acc)

    def finish():
        m_new, l_new, acc = vals["m"], vals["l"], vals["acc"]
        s_new = jnp.sum(q_bd * kn_ref[0], axis=1, keepdims=True)
        m_fin = jnp.maximum(m_new, s_new)
        a_old = jnp.exp(m_new - m_fin)
        p_new = jnp.exp(s_new - m_fin)
        l_fin = l_new * a_old + p_new
        full = (acc * a_old + p_new * vn_ref[0]) / l_fin
        o_ref[0] = jnp.sum(jnp.where(own, full, 0.0), axis=0, keepdims=True)

    return scores, combine, finish


def _fox_decode(layer, q, k_new, v_new, cache_kt, cache_vt, cache_ft, tail, page_table):
    n, d = q.shape
    n_pages = page_table.shape[1]
    nh, e, page = cache_kt.shape[2:]
    npg = _pick_tile(n_pages, (8, 4, 2, 1))

    def kv_spec(t):
        return pl.BlockSpec((1, 1, nh, e, page), lambda b, pg, pt: (layer, pt[b, pg * npg + t], 0, 0, 0))

    def f_spec(t):
        return pl.BlockSpec((1, 1, nh, page), lambda b, pg, pt: (layer, pt[b, pg * npg + t], 0, 0))

    row_spec = pl.BlockSpec((1, 1, d), lambda b, pg, pt: (b, 0, 0))
    grid_spec = pltpu.PrefetchScalarGridSpec(
        num_scalar_prefetch=1,
        grid=(n, n_pages // npg),
        in_specs=[row_spec] + [kv_spec(t) for t in range(npg)] * 2 + [f_spec(t) for t in range(npg)]
                 + [pl.BlockSpec((1, npg, nh), lambda b, pg, pt: (b, pg, 0)), row_spec, row_spec],
        out_specs=row_spec,
        scratch_shapes=[pltpu.VMEM((nh, 1), F32), pltpu.VMEM((nh, 1), F32), pltpu.VMEM((nh, d), F32)],
    )
    o = pl.pallas_call(
        functools.partial(_fox_decode_kernel, npg=npg),
        grid_spec=grid_spec,
        out_shape=jax.ShapeDtypeStruct((n, 1, d), F32),
        compiler_params=pltpu.CompilerParams(
            dimension_semantics=("parallel", "arbitrary"), vmem_limit_bytes=VMEM_LIMIT),
        name="fox_decode",
    )(page_table, q.reshape(n, 1, d), *([cache_kt] * npg), *([cache_vt] * npg), *([cache_ft] * npg),
      tail, k_new.reshape(n, 1, d), v_new.reshape(n, 1, d))
    return o.reshape(n, d)


def _fox_attn_fused_kernel(pt_ref, fq_ref, fk_ref, fv_ref, fcq_ref, fck_ref, dq_ref, *refs, npg, tk, n_groups):
    k_refs, v_refs, f_refs = refs[:npg], refs[npg:2 * npg], refs[2 * npg:3 * npg]
    (tail_ref, kn_ref, vn_ref, o_ref, do_ref,
     s_ref, m_ref, l_ref, acc_ref, dm_ref, dl_ref, dacc_ref) = refs[3 * npg:]
    p, qi = pl.program_id(1), pl.program_id(2)
    step = (pl.program_id(0) * pl.num_programs(1) + p) * pl.num_programs(2) + qi
    decode = _decode_phases(step % n_groups, dq_ref, k_refs, v_refs, f_refs, tail_ref, kn_ref, vn_ref,
                            do_ref, dm_ref, dl_ref, dacc_ref)
    _flash_body(p, qi, fq_ref, fk_ref, fv_ref, fcq_ref, fck_ref, o_ref, s_ref, m_ref, l_ref, acc_ref, tk=tk,
                side_work=decode)


def _fox_attention(layer, q_b, k_b, v_b, c, dq, dk_new, dv_new, cache_kt, cache_vt, cache_ft, tail, page_table,
                   tile=512):
    bsz, t, d = q_b.shape
    n = dq.shape[0]
    nh, e, page = cache_kt.shape[2:]
    n_pages = page_table.shape[1]
    npair = d // LANES
    tq = tk = _pick_tile(t, (tile, 256, 128))
    nq = t // tq
    npg = _pick_tile(n_pages, (8, 4, 2, 1))
    n_groups = n_pages // npg
    if bsz * npair * nq != n * n_groups:
        return (_fox_flash(q_b, k_b, v_b, c, tile),
                _fox_decode(layer, dq, dk_new, dv_new, cache_kt, cache_vt, cache_ft, tail, page_table))
    pieces = _bias_pieces(c)

    def seq_of(bi, p, qi):
        return ((bi * npair + p) * nq + qi) // n_groups

    def grp_of(bi, p, qi):
        return ((bi * npair + p) * nq + qi) % n_groups

    def kv_spec(j):
        return pl.BlockSpec((1, 1, nh, e, page),
                            lambda bi, p, qi, pt: (layer, pt[seq_of(bi, p, qi), grp_of(bi, p, qi) * npg + j], 0, 0, 0))

    def f_spec(j):
        return pl.BlockSpec((1, 1, nh, page),
                            lambda bi, p, qi, pt: (layer, pt[seq_of(bi, p, qi), grp_of(bi, p, qi) * npg + j], 0, 0))

    q_map = lambda bi, p, qi, pt: (bi, qi, p)
    seq_map = lambda bi, p, qi, pt: (bi, 0, p)
    row_spec = pl.BlockSpec((1, 1, d), lambda bi, p, qi, pt: (seq_of(bi, p, qi), 0, 0))
    grid_spec = pltpu.PrefetchScalarGridSpec(
        num_scalar_prefetch=1,
        grid=(bsz, npair, nq),
        in_specs=[pl.BlockSpec((1, tq, LANES), q_map),
                  pl.BlockSpec((1, t, LANES), seq_map),
                  pl.BlockSpec((1, t, LANES), seq_map),
                  pl.BlockSpec((1, tq, 3 * nh), lambda bi, p, qi, pt: (bi, qi, 0)),
                  pl.BlockSpec((1, t, 3 * nh), lambda bi, p, qi, pt: (bi, 0, 0)),
                  row_spec]
                 + [kv_spec(j) for j in range(npg)] * 2 + [f_spec(j) for j in range(npg)]
                 + [pl.BlockSpec((1, npg, nh), lambda bi, p, qi, pt: (seq_of(bi, p, qi), grp_of(bi, p, qi), 0)),
                    row_spec, row_spec],
        out_specs=[pl.BlockSpec((1, tq, LANES), q_map), row_spec],
        scratch_shapes=[pltpu.VMEM((2, 2, tq, tk), F32),
                        pltpu.VMEM((2, tq, LANES), F32), pltpu.VMEM((2, tq, LANES), F32),
                        pltpu.VMEM((tq, LANES), F32),
                        pltpu.VMEM((nh, 1), F32), pltpu.VMEM((nh, 1), F32), pltpu.VMEM((nh, d), F32)],
    )
    o, do = pl.pallas_call(
        functools.partial(_fox_attn_fused_kernel, npg=npg, tk=tk, n_groups=n_groups),
        grid_spec=grid_spec,
        out_shape=[jax.ShapeDtypeStruct((bsz, t, d), F32), jax.ShapeDtypeStruct((n, 1, d), F32)],
        compiler_params=pltpu.CompilerParams(
            dimension_semantics=("arbitrary", "arbitrary", "arbitrary"), vmem_limit_bytes=VMEM_LIMIT),
        name="fox_attention",
    )(page_table, q_b, k_b, v_b, pieces, pieces, dq.reshape(n, 1, d),
      *([cache_kt] * npg), *([cache_vt] * npg), *([cache_ft] * npg), tail,
      dk_new.reshape(n, 1, d), dv_new.reshape(n, 1, d))
    return o, do.reshape(n, d)


def _rmsnorm(x, g):
    xf = x.astype(F32)
    y = xf * lax.rsqrt(jnp.mean(xf * xf, axis=-1, keepdims=True) + RMS_EPS)
    return (y * g.astype(F32)).astype(x.dtype)


def _rwkv_proj_kernel(x_ref, hp_ref, g_ref, mu_ref, w_ref, w0_ref, w1_ref, w2_ref, a0_ref, a1_ref, a2_ref,
                      kk_ref, ka_ref, seg_ref, segt_ref,
                      r_ref, lw_ref, kf_ref, v_ref, nkk_ref, kka_ref, gate_ref, carry_ref,
                      *, tiles_per_seq):
    x = x_ref[...]
    tm = x.shape[0]
    g = g_ref[...]
    h = x * lax.rsqrt(jnp.mean(x * x, axis=-1, keepdims=True) + RMS_EPS) * g
    if tiles_per_seq == 0:
        h_prev = hp_ref[...]
    else:
        first = pl.program_id(0) % tiles_per_seq == 0
        before = jnp.where(first, hp_ref[0], carry_ref[...])
        row = lax.broadcasted_iota(jnp.int32, x.shape, 0)
        h_prev = jnp.where(row == 0, before, pltpu.roll(h, 1, 0))
        carry_ref[...] = h[tm - 1:tm, :]
    xx = h_prev - h
    mix = lambda s: (h + xx * mu_ref[s:s + 1, :]).astype(BF16)
    k = _nn(mix(1), w_ref[1])
    w_mid = _nn(mix(4), w1_ref[...])
    a_mid = _nn(mix(5), a1_ref[...])
    r_ref[...] = _nn(mix(0), w_ref[0])
    kk = k * kk_ref[...]
    kk_sq = _seg_sum(kk * kk, seg_ref[...])
    z = w0_ref[...] + _nn(jnp.tanh(w_mid).astype(BF16), w2_ref[...])
    a_pre = _nn(a_mid.astype(BF16), a2_ref[...])
    v_ref[...] = _nn(mix(2), w_ref[2])
    kk = kk * _seg_bcast(1.0 / jnp.maximum(jnp.sqrt(kk_sq), 1e-12), segt_ref[...])
    gate_ref[...] = _nn(mix(3), w_ref[3])
    lw_ref[...] = -jnp.exp(jnp.minimum(z, 0.0) - jnp.log(1.0 + jnp.exp(-jnp.abs(z))) - 0.5)
    a = jax.nn.sigmoid(a0_ref[...] + a_pre)
    kf_ref[...] = k * (1.0 + (a - 1.0) * ka_ref[...])
    nkk_ref[...] = -kk
    kka_ref[...] = kk * a


def _rwkv_out_kernel(o_ref, r_ref, kf_ref, v_ref, gate_ref, x_ref, lw_ref, lb_ref, rk_ref, w_ref,
                     seg_ref, segt_ref, y_ref):
    seg = seg_ref[...]
    seg_t = segt_ref[...]
    tm, d = o_ref.shape
    inv_e = 1.0 / (d // seg.shape[1])
    n_groups = 2 if tm % 16 == 0 else 1
    rows = [slice(i * (tm // n_groups), (i + 1) * (tm // n_groups)) for i in range(n_groups)]
    o = [o_ref[rs, :] for rs in rows]
    o_sum = [_seg_sum(x, seg) for x in o]
    rk_sum = [_seg_sum(r_ref[rs, :] * kf_ref[rs, :] * rk_ref[...], seg) for rs in rows]
    cen = [x - _seg_bcast(s * inv_e, seg_t) for x, s in zip(o, o_sum)]
    var = [_seg_sum(c * c, seg) * inv_e for c in cen]
    bonus = [_seg_bcast(s, seg_t) * v_ref[rs, :] for s, rs in zip(rk_sum, rows)]
    gn = [c * _seg_bcast(lax.rsqrt(vr + LNX_EPS), seg_t) * lw_ref[...] + lb_ref[...] for c, vr in zip(cen, var)]
    for rs, gn_i, bonus_i in zip(rows, gn, bonus):
        g = gate_ref[rs, :]
        y = ((gn_i + bonus_i) * (g * jax.nn.sigmoid(g))).astype(BF16)
        y_ref[rs, :] = x_ref[rs, :] + _nn(y, w_ref[...])


def _rwkv_layer(x, h_prev0, states, layer, prm, nh):
    (norm, mu, w_in, w0, w1, w2, a0, a1, a2, k_k, k_a, r_k, lnx_w, lnx_b, w_out) = prm
    bsz, t, d = x.shape
    e = d // nh
    m = bsz * t
    seg, seg_t = _seg_mats(d, nh)
    row = lambda z: z.reshape(1, -1).astype(F32)
    full = lambda arr: pl.BlockSpec(arr.shape, lambda i: (0,) * arr.ndim)
    x2 = x.reshape(m, d)
    if t == 1:
        tm = _pick_tile(m, (256, 128))
        tiles_per_seq = 0
        hp = h_prev0.astype(F32)
        hp_spec = pl.BlockSpec((tm, d), lambda i: (i, 0))
    else:
        tm = _pick_tile(t, (256, 128))
        tiles_per_seq = t // tm
        hp = h_prev0.astype(F32).reshape(bsz, 1, d)
        hp_spec = pl.BlockSpec((1, 1, d), lambda i: (i // tiles_per_seq, 0, 0))
    tile = pl.BlockSpec((tm, d), lambda i: (i, 0))
    consts = [row(norm), mu.astype(F32), w_in.astype(BF16), row(w0), w1.astype(BF16), w2.astype(BF16),
              row(a0), a1.astype(BF16), a2.astype(BF16), row(k_k), row(k_a), seg, seg_t]
    r, lw, kf, v, neg_kk, kk_a, gate = pl.pallas_call(
        functools.partial(_rwkv_proj_kernel, tiles_per_seq=tiles_per_seq),
        grid=(m // tm,),
        in_specs=[tile, hp_spec] + [full(c) for c in consts],
        out_specs=[tile] * 7,
        out_shape=[jax.ShapeDtypeStruct((m, d), F32)] * 7,
        scratch_shapes=[pltpu.VMEM((1, d), F32)],
        compiler_params=pltpu.CompilerParams(
            dimension_semantics=("arbitrary",), vmem_limit_bytes=VMEM_LIMIT),
        name="rwkv_proj",
    )(x2, hp, *consts)

    if states is None:
        seq = lambda z: z.reshape(bsz, t, d)
        o, zt = _wkv_chunked(seq(r), seq(lw), seq(kf), seq(v), seq(neg_kk), seq(kk_a))
        zt = zt.reshape(bsz, d // LANES, 2, e, 2, e)
        s_fin = jnp.stack([zt[:, :, 0, :, 0, :], zt[:, :, 1, :, 1, :]], axis=2).reshape(bsz, nh, e, e)
    else:
        o, s_fin = _wkv_step(states, layer, r, lw, kf, v, neg_kk, kk_a)

    tmo = _pick_tile(m, (256, 128))
    tile_o = pl.BlockSpec((tmo, d), lambda i: (i, 0))
    consts_o = [row(lnx_w), row(lnx_b), row(r_k), w_out.astype(BF16), seg, seg_t]
    x_new = pl.pallas_call(
        _rwkv_out_kernel,
        grid=(m // tmo,),
        in_specs=[tile_o] * 6 + [full(c) for c in consts_o],
        out_specs=tile_o,
        out_shape=jax.ShapeDtypeStruct((m, d), F32),
        compiler_params=pltpu.CompilerParams(
            dimension_semantics=("parallel",), vmem_limit_bytes=VMEM_LIMIT),
        name="rwkv_out",
    )(o.reshape(m, d), r, kf, v, gate, x2, *consts_o)
    h_last = _rmsnorm(x[:, -1, :], norm)
    return x_new.reshape(bsz, t, d), s_fin, h_last


def _split2(x):
    hi = x.astype(BF16)
    return hi, (x - hi.astype(F32)).astype(BF16)


def _seg_sum(x, seg):
    hi, lo = _split2(x)
    return _nn(hi, seg) + _nn(lo, seg)


def _seg_bcast(y, seg_t):
    hi, lo = _split2(y)
    return _nn(hi, seg_t) + _nn(lo, seg_t)


def _seg_mats(d, nh):
    lane_head = jnp.arange(d, dtype=jnp.int32) // (d // nh)
    seg = (lane_head[:, None] == jnp.arange(nh, dtype=jnp.int32)[None, :]).astype(BF16)
    return seg, seg.T


def _fox_proj_kernel(x_ref, g_ref, w_ref, wf_ref, bf_ref, qn_ref, kn_ref, seg_ref, segt_ref,
                     q_ref, k_ref, v_ref, gate_ref, lf_ref, *attn_refs, q_scale):
    x = x_ref[...]
    d = x.shape[1]
    hb = (x * lax.rsqrt(jnp.mean(x * x, axis=-1, keepdims=True) + RMS_EPS) * g_ref[...]).astype(BF16)
    seg = seg_ref[...]
    seg_t = segt_ref[...]
    inv_e = 1.0 / (d // seg.shape[1])

    q = _nn(hb, w_ref[:, 0:d])
    k = _nn(hb, w_ref[:, d:2 * d])
    q_ms = _seg_sum(q * q, seg)
    v = _nn(hb, w_ref[:, 2 * d:3 * d])
    k_ms = _seg_sum(k * k, seg)
    q = q * _seg_bcast(lax.rsqrt(q_ms * inv_e + RMS_EPS), seg_t) * qn_ref[...]
    gate_ref[...] = _nn(hb, w_ref[:, 3 * d:4 * d])
    k = k * _seg_bcast(lax.rsqrt(k_ms * inv_e + RMS_EPS), seg_t) * kn_ref[...]
    k_ref[...] = k
    v_ref[...] = v
    if attn_refs:
        kb_ref, vb_ref = attn_refs
        q_ref[...] = (q * q_scale).astype(BF16)
        kb_ref[...] = k.astype(BF16)
        vb_ref[...] = v.astype(BF16)
    else:
        q_ref[...] = q
    f = _nn(hb, wf_ref[...]) + bf_ref[...]
    lf_ref[...] = jnp.minimum(f, 0.0) - jnp.log(1.0 + jnp.exp(-jnp.abs(f)))


def _fox_project(x, norm, w_in, b_f, qn_g, kn_g, nh, attn_q_scale=None):
    bsz, t, d = x.shape
    m = bsz * t
    tm = _pick_tile(m, (256, 128))
    seg, seg_t = _seg_mats(d, nh)
    row = lambda z: z.reshape(1, -1).astype(F32)
    full = lambda a: pl.BlockSpec(a.shape, lambda i: (0,) * a.ndim)
    consts = [row(norm), w_in[:, :4 * d].astype(BF16), w_in[:, 4 * d:].astype(BF16), row(b_f),
              row(jnp.tile(qn_g, nh)), row(jnp.tile(kn_g, nh)), seg, seg_t]
    tile = pl.BlockSpec((tm, d), lambda i: (i, 0))
    n_attn = 0 if attn_q_scale is None else 2
    q_dtype = F32 if attn_q_scale is None else BF16
    outs = pl.pallas_call(
        functools.partial(_fox_proj_kernel, q_scale=attn_q_scale),
        grid=(m // tm,),
        in_specs=[tile] + [full(a) for a in consts],
        out_specs=[tile] * 4 + [pl.BlockSpec((tm, nh), lambda i: (i, 0))] + [tile] * n_attn,
        out_shape=([jax.ShapeDtypeStruct((m, d), q_dtype)] + [jax.ShapeDtypeStruct((m, d), F32)] * 3
                   + [jax.ShapeDtypeStruct((m, nh), F32)] + [jax.ShapeDtypeStruct((m, d), BF16)] * n_attn),
        compiler_params=pltpu.CompilerParams(
            dimension_semantics=("parallel",), vmem_limit_bytes=VMEM_LIMIT),
        name="fox_proj",
    )(x.reshape(m, d), *consts)
    q, k, v, gate, logf = outs[:5]
    seq = lambda z: z.reshape(bsz, t, d)
    return seq(q), seq(k), seq(v), gate, logf.reshape(bsz, t, nh), tuple(seq(z) for z in outs[5:])


def _gate_out_kernel(o_ref, gate_ref, x_ref, w_ref, y_ref):
    g = gate_ref[...]
    y = (o_ref[...] * (g * jax.nn.sigmoid(g))).astype(BF16)
    y_ref[...] = x_ref[...] + _nn(y, w_ref[...])


def _fox_finish(x, o, gate, w_out):
    bsz, t, d = x.shape
    m = bsz * t
    tm = _pick_tile(m, (512, 256, 128))
    tile = pl.BlockSpec((tm, d), lambda i: (i, 0))
    y = pl.pallas_call(
        _gate_out_kernel,
        grid=(m // tm,),
        in_specs=[tile, tile, tile, pl.BlockSpec((d, d), lambda i: (0, 0))],
        out_specs=tile,
        out_shape=jax.ShapeDtypeStruct((m, d), F32),
        compiler_params=pltpu.CompilerParams(
            dimension_semantics=("parallel",), vmem_limit_bytes=VMEM_LIMIT),
        name="gate_out",
    )(o.reshape(m, d), gate, x.reshape(m, d), w_out.astype(BF16))
    return y.reshape(bsz, t, d)


def kernel(x_prompt, x_sample, state_wkv, state_shift, cache_k, cache_v, cache_logf, page_table,
           norm_a, mu_a, w_in_a, w0_a, w1_a, w2_a, a0_a, a1_a, a2_a, kk_a, ka_a, rk_a, lnx_w_a, lnx_b_a, w_out_a,
           norm_b, w_in_b, bf_b, qn_b, kn_b, w_out_b):
    bsz, t, d = x_prompt.shape
    nb, ts, _ = x_sample.shape
    assert ts == 1, "the sample group carries one new token per sequence"
    nh, e = rk_a.shape[1], rk_a.shape[2]
    assert d == nh * e and 2 * e == LANES and t % WKV_CHUNK == 0
    n_layers_a = norm_a.shape[0]
    n_layers_b = norm_b.shape[0]
    depth = n_layers_a + n_layers_b
    n_pool, page = cache_k.shape[1], cache_k.shape[2]
    n_pages = page_table.shape[1]
    past = n_pages * page
    scale = e ** -0.5
    ckt = jnp.transpose(cache_k, (0, 1, 3, 4, 2))
    cvt = jnp.transpose(cache_v, (0, 1, 3, 4, 2))
    cft = jnp.transpose(cache_logf, (0, 1, 3, 2)).astype(F32)
    page_mass = jnp.sum(cft, axis=-1)

    xp, xs = x_prompt, x_sample
    kp_l, vp_l, fp_l, sp_l, hp_l = [], [], [], [], []
    ks_l, vs_l, fs_l, hs_l = [], [], [], []
    states_s = state_wkv.astype(F32)
    for i in range(depth):
        j = i // 2
        if i % 2 == 0:
            prm = (norm_a[j], mu_a[j], w_in_a[j], w0_a[j], w1_a[j], w2_a[j], a0_a[j], a1_a[j], a2_a[j],
                   kk_a[j], ka_a[j], rk_a[j], lnx_w_a[j], lnx_b_a[j], w_out_a[j])
            xp, s_p, l_p = _rwkv_layer(xp, jnp.zeros((bsz, d), xp.dtype), None, j, prm, nh)
            xs, states_s, l_s = _rwkv_layer(xs, state_shift[j], states_s, j, prm, nh)
            sp_l.append(s_p); hp_l.append(l_p); hs_l.append(l_s)
        else:
            qp_b, kp, vp, gp, lfp, kv_b = _fox_project(xp, norm_b[j], w_in_b[j], bf_b[j], qn_b[j], kn_b[j], nh,
                                                        attn_q_scale=scale * LOG2E)
            cp = jnp.cumsum(lfp, axis=1)
            qs, ksn, vsn, gs, lfs, _ = _fox_project(xs, norm_b[j], w_in_b[j], bf_b[j], qn_b[j], kn_b[j], nh)
            seq_mass = page_mass[j][page_table]
            later = jnp.cumsum(seq_mass[:, ::-1, :], axis=1)[:, ::-1, :] - seq_mass
            tail = later + lfs.reshape(nb, 1, nh)
            op, osm = _fox_attention(j, qp_b, *kv_b, cp, (qs * scale).reshape(nb, d), ksn.reshape(nb, d),
                                     vsn.reshape(nb, d), ckt, cvt, cft, tail, page_table)
            xp = _fox_finish(xp, op, gp, w_out_b[j])
            xs = _fox_finish(xs, osm.reshape(nb, 1, d), gs, w_out_b[j])
            hd4 = lambda z, n_, t_: z.reshape(n_, t_, nh, e)
            kp_l.append(hd4(kp, bsz, t)); vp_l.append(hd4(vp, bsz, t)); fp_l.append(lfp)
            ks_l.append(hd4(ksn, nb, ts)); vs_l.append(hd4(vsn, nb, ts)); fs_l.append(lfs)
    return (xp, xs,
            jnp.stack(kp_l), jnp.stack(vp_l), jnp.stack(fp_l), jnp.stack(sp_l), jnp.stack(hp_l),
            jnp.stack(ks_l), jnp.stack(vs_l), jnp.stack(fs_l), states_s, jnp.stack(hs_l))
```

```python
import functools

import jax
import jax.numpy as jnp
from jax import lax
from jax.experimental import pallas as pl
from jax.experimental.pallas import tpu as pltpu

F32 = jnp.float32
BF16 = jnp.bfloat16

LANES = 128
RMS_EPS = 1e-6
LNX_EPS = 64e-5
NEG_BIG = -1e30
LOG2E = 1.4426950408889634
WKV_CHUNK = 64
WKV_CHUNKS_PER_STEP = (4, 2, 1)
VMEM_LIMIT = 48 * 1024 * 1024


def _nt(x, y):
    return lax.dot_general(x, y, (((1,), (1,)), ((), ())), preferred_element_type=F32)


def _tn(x, y):
    return lax.dot_general(x, y, (((0,), (0,)), ((), ())), preferred_element_type=F32)


def _nn(x, y):
    return jnp.dot(x, y, preferred_element_type=F32)


def _pick_tile(n, candidates):
    for c in candidates:
        if n % c == 0:
            return c
    return n


def _wkv_chunk_kernel(r_ref, lw_ref, k_ref, v_ref, a_ref, b_ref, o_ref, zt_ref):
    c = pl.program_id(1)

    @pl.when(c == 0)
    def _():
        zt_ref[...] = jnp.zeros_like(zt_ref)

    L = WKV_CHUNK
    n_chunks = r_ref.shape[1] // L
    n = 2 * L
    half = LANES // 2
    npair = r_ref.shape[2] // LANES

    ti = lax.broadcasted_iota(jnp.int32, (L, L), 0)
    tj = lax.broadcasted_iota(jnp.int32, (L, L), 1)
    tri = (ti >= tj).astype(BF16)
    head0 = lax.broadcasted_iota(jnp.int32, (L, LANES), 1) < half
    i = lax.broadcasted_iota(jnp.int32, (n, n), 0)
    j = lax.broadcasted_iota(jnp.int32, (n, n), 1)
    strict = i > j
    incl = i >= j
    eye = jnp.where(i == j, 1.0, 0.0)
    first = (i >> 1) == (j >> 1)
    levels = []
    lvl = 1
    while (2 << lvl) <= L:
        levels.append(((i >> (lvl + 1)) == (j >> (lvl + 1))) & ((i >> lvl) != (j >> lvl)))
        lvl += 1

    def stack(x):
        return jnp.concatenate([jnp.where(head0, x, 0.0), jnp.where(head0, 0.0, x)], axis=0)

    pairs = range(npair)
    units = range(n_chunks * npair)
    sls = [(slice((q // npair) * L, (q // npair + 1) * L), slice((q % npair) * LANES, (q % npair + 1) * LANES))
           for q in units]
    ar, bk, bk_h, vs, p_last = [], [], [], [], []
    for q in units:
        r, lw, k, v, a, b = (ref[0, sls[q][0], sls[q][1]] for ref in (r_ref, lw_ref, k_ref, v_ref, a_ref, b_ref))
        lw1 = lw.astype(BF16)
        rem = lw - lw1.astype(F32)
        lw2 = rem.astype(BF16)
        lw3 = (rem - lw2.astype(F32)).astype(BF16)
        cum = _nn(tri, lw1) + (_nn(tri, lw2) + _nn(tri, lw3))
        p_inc = jnp.exp(cum)
        p_exc = jnp.exp(cum - lw)
        p_inv = jnp.exp(-cum)
        pl_ = p_inc[L - 1:L, :]
        bk_f = jnp.concatenate([stack(b * p_inv), stack(k * p_inv)], axis=0)
        ar.append(jnp.concatenate([stack(a * p_exc), stack(r * p_inc)], axis=0).astype(BF16))
        bk.append(bk_f.astype(BF16))
        bk_h.append((bk_f * pl_).astype(BF16))
        vs.append(stack(v).astype(BF16))
        p_last.append(pl_)

    g = [_nt(ar[q], bk[q]) for q in units]
    a_ab = [jnp.where(strict, g[q][:n, :n], 0.0) for q in units]
    ak_rk = [jnp.concatenate([jnp.where(strict, g[q][:n, n:], 0.0),
                              jnp.where(incl, g[q][n:, n:], 0.0)], axis=0).astype(BF16) for q in units]
    a_rb = [jnp.where(incl, g[q][n:, :n], 0.0).astype(BF16) for q in units]

    def compress(full, size):
        return functools.reduce(lambda x, y: x + y, [full[i * size:(i + 1) * size] for i in range(n // size)])

    def expand(comp, size):
        lane_blk = lax.broadcasted_iota(jnp.int32, (size, n), 1) >> (size.bit_length() - 1)
        return jnp.concatenate([jnp.where(lane_blk == i, comp, 0.0) for i in range(n // size)], axis=0)

    sub = L // 2
    t_full = [eye + jnp.where(first, a_ab[q], 0.0) for q in units]
    t_comp = [compress(t_full[q], sub) for q in units]
    for off in levels[:-1]:
        x = [_nn(t_comp[q].astype(BF16), jnp.where(off, a_ab[q], 0.0).astype(BF16)).astype(BF16) for q in units]
        t_comp = [t_comp[q] + _nn(x[q], t_full[q].astype(BF16)) for q in units]
        t_full = [expand(t_comp[q], sub) for q in units]
    t_comp = [compress(t_full[q], L) for q in units]
    x = [_nn(t_comp[q].astype(BF16), jnp.where(levels[-1], a_ab[q], 0.0).astype(BF16)).astype(BF16) for q in units]
    t_comp = [(t_comp[q] + _nn(x[q], t_full[q].astype(BF16))).astype(BF16) for q in units]

    zt = [zt_ref[0, p] for p in pairs]
    for c in range(n_chunks):
        qs = [c * npair + p for p in pairs]
        y = [_nt(ar[q], zt[p].astype(BF16)) + _nn(ak_rk[q], vs[q]) for p, q in zip(pairs, qs)]
        u_b = [stack(_nn(t_comp[q], y[p][:n].astype(BF16))).astype(BF16) for p, q in zip(pairs, qs)]
        for p, q in zip(pairs, qs):
            o_st = y[p][n:] + _nn(a_rb[q], u_b[p])
            o_ref[0, sls[q][0], sls[q][1]] = o_st[:L] + o_st[L:]
            zt[p] = zt[p] * p_last[q] + _tn(jnp.concatenate([u_b[p], vs[q]], axis=0), bk_h[q])
    for p in pairs:
        zt_ref[0, p] = zt[p]


def _wkv_chunked(r, lw, k, v, a, b):
    bsz, t, d = r.shape
    npair = d // LANES
    tb = _pick_tile(t, tuple(WKV_CHUNK * c for c in WKV_CHUNKS_PER_STEP))
    spec = pl.BlockSpec((1, tb, d), lambda bi, c: (bi, c, 0))
    return pl.pallas_call(
        _wkv_chunk_kernel,
        grid=(bsz, t // tb),
        in_specs=[spec] * 6,
        out_specs=[spec, pl.BlockSpec((1, npair, LANES, LANES), lambda bi, c: (bi, 0, 0, 0))],
        out_shape=[jax.ShapeDtypeStruct((bsz, t, d), F32),
                   jax.ShapeDtypeStruct((bsz, npair, LANES, LANES), F32)],
        compiler_params=pltpu.CompilerParams(
            dimension_semantics=("parallel", "arbitrary"), vmem_limit_bytes=VMEM_LIMIT),
        name="wkv_chunked",
    )(r, lw, k, v, a, b)


def _wkv_step_kernel(s_ref, r_ref, lw_ref, k_ref, v_ref, a_ref, b_ref, so_ref, o_ref):
    _, bb, nh, e, _ = s_ref.shape
    half = LANES // 2
    ii = lax.broadcasted_iota(jnp.int32, (e, LANES), 0)
    jj = lax.broadcasted_iota(jnp.int32, (e, LANES), 1)
    eye_lo = ii == jj
    eye_hi = ii + half == jj

    for i in range(bb):
        units = []
        for h in range(nh):
            p, par = divmod(h, 2)
            sl = slice(p * LANES, (p + 1) * LANES)
            r, lw, k, v, a, b = (ref[i:i + 1, sl] for ref in (r_ref, lw_ref, k_ref, v_ref, a_ref, b_ref))
            w = jnp.exp(lw)
            if par == 0:
                rh, wh, kh, ah, bh = (z[:, :e] for z in (r, w, k, a, b))
            else:
                rh, wh, kh, ah, bh = (pltpu.roll(z, half, 1)[:, :e] for z in (r, w, k, a, b))
            s = s_ref[0, i, h]
            v_col = jnp.sum(jnp.where(eye_hi if par else eye_lo, v, 0.0), axis=1, keepdims=True)
            sa = jnp.sum(s * ah, axis=1, keepdims=True)
            units.append((s, rh, wh, kh, bh, v_col, sa))
        o_cols = []
        for h, (s, rh, wh, kh, bh, v_col, sa) in enumerate(units):
            s_new = s * wh + sa * bh + v_col * kh
            so_ref[0, i, h] = s_new
            o_cols.append(jnp.sum(s_new * rh, axis=1, keepdims=True))
        for p in range(nh // 2):
            o_ref[i:i + 1, p * LANES:(p + 1) * LANES] = (
                jnp.sum(jnp.where(eye_lo, o_cols[2 * p], 0.0), axis=0, keepdims=True)
                + jnp.sum(jnp.where(eye_hi, o_cols[2 * p + 1], 0.0), axis=0, keepdims=True))


def _wkv_step(states, layer, r, lw, k, v, a, b):
    _, n, h, e, _ = states.shape
    d = h * e
    bb = _pick_tile(n, (8,))
    s_spec = pl.BlockSpec((1, bb, h, e, e), lambda i: (layer, i, 0, 0, 0))
    row_spec = pl.BlockSpec((bb, d), lambda i: (i, 0))
    states, o = pl.pallas_call(
        _wkv_step_kernel,
        grid=(n // bb,),
        in_specs=[s_spec] + [row_spec] * 6,
        out_specs=[s_spec, row_spec],
        out_shape=[jax.ShapeDtypeStruct(states.shape, F32), jax.ShapeDtypeStruct((n, d), F32)],
        input_output_aliases={0: 0},
        compiler_params=pltpu.CompilerParams(
            dimension_semantics=("parallel",), vmem_limit_bytes=VMEM_LIMIT),
        name="wkv_step",
    )(states, r, lw, k, v, a, b)
    return o, states


def _fox_flash_kernel(*refs, tk):
    _flash_body(pl.program_id(1), pl.program_id(2), *refs, tk=tk)


def _flash_body(p, qi, q_ref, k_ref, v_ref, cq_ref, ck_ref, o_ref, s_ref, m_ref, l_ref, acc_ref, *, tk,
                side_work=()):
    tq = q_ref.shape[1]
    half = LANES // 2
    n_piece_rows = cq_ref.shape[2]
    nh = n_piece_rows // 3
    heads = range(2)

    q_head0 = lax.broadcasted_iota(jnp.int32, (tq, LANES), 1) < half
    k_head0 = lax.broadcasted_iota(jnp.int32, (tk, LANES), 1) < half

    prow = lax.broadcasted_iota(jnp.int32, (n_piece_rows, LANES), 0)
    plane = lax.broadcasted_iota(jnp.int32, (n_piece_rows, LANES), 1)
    piece, head = prow >> (nh.bit_length() - 1), prow & (nh - 1)
    base = jnp.where(head == 2 * p, half, jnp.where(head == 2 * p + 1, 0, -LANES))
    place_q = jnp.where(plane == base + piece, 1.0, 0.0).astype(BF16)
    place_k = jnp.where(plane == base + piece + 3, -1.0, 0.0).astype(BF16)
    slot = lax.broadcasted_iota(jnp.int32, (1, LANES), 1) & (half - 1)
    ones_q = jnp.where((slot >= 3) & (slot < 6), 1.0, 0.0)
    ones_k = jnp.where(slot < 3, 1.0, 0.0)

    q = q_ref[0]
    q_aug = (_nn(cq_ref[0], place_q) + ones_q).astype(BF16)
    q_ops = (jnp.where(q_head0, q, q_aug), jnp.where(q_head0, q_aug, q))

    m_ref[...] = jnp.full_like(m_ref, NEG_BIG)
    l_ref[...] = jnp.zeros_like(l_ref)
    acc_ref[...] = jnp.zeros_like(acc_ref)

    def rows(kj):
        return slice(kj * tk, (kj + 1) * tk)

    def scores(kj, slot_idx):
        k = k_ref[0, rows(kj), :]
        k_aug = (_nn(ck_ref[0, rows(kj), :], place_k) + ones_k).astype(BF16)
        k_ops = (jnp.where(k_head0, k, k_aug), jnp.where(k_head0, k_aug, k))
        for h in heads:
            s_ref[slot_idx, h] = _nt(q_ops[h], k_ops[h])

    def consume(kj, slot_idx, on_diagonal):
        v = v_ref[0, rows(kj), :]
        zero = jnp.zeros_like(v)
        v_own = (jnp.where(k_head0, v, zero), jnp.where(k_head0, zero, v))
        if on_diagonal:
            keep = (lax.broadcasted_iota(jnp.int32, (tq, tk), 0)
                    >= lax.broadcasted_iota(jnp.int32, (tq, tk), 1))
        alpha, pv = [], []
        for h in heads:
            s = s_ref[slot_idx, h]
            if on_diagonal:
                s = jnp.where(keep, s, NEG_BIG)
            m_prev = m_ref[h]
            m_new = jnp.maximum(m_prev, jnp.max(s, axis=1, keepdims=True))
            a = jnp.exp2(m_prev - m_new)
            pr = jnp.exp2(s - jnp.concatenate([m_new] * (tk // LANES), axis=1))
            l_ref[h] = a * l_ref[h] + jnp.sum(pr, axis=1, keepdims=True)
            m_ref[h] = m_new
            alpha.append(a)
            pv.append(_nn(pr.astype(BF16), v_own[h]))
        acc_ref[...] = jnp.where(q_head0, alpha[0], alpha[1]) * acc_ref[...] + (pv[0] + pv[1])

    side = list(side_work) + [lambda: None] * (3 - len(side_work))

    def run(n_before):
        scores(0, 0)
        side[0]()
        for kj in range(n_before):
            scores(kj + 1, (kj + 1) % 2)
            consume(kj, kj % 2, False)
        side[1]()
        consume(n_before, n_before % 2, True)
        side[2]()
        o_ref[0] = acc_ref[...] / jnp.where(q_head0, l_ref[0], l_ref[1])

    for n_before in range(k_ref.shape[1] // tk):
        pl.when(qi == n_before)(functools.partial(run, n_before))


def _fox_flash(q_b, k_b, v_b, pieces, tile=512):
    bsz, t, d = q_b.shape
    nh = pieces.shape[-1] // 3
    assert nh & (nh - 1) == 0, "head count must be a power of two"
    npair = d // LANES
    tq = tk = _pick_tile(t, (tile, 256, 128))
    q_map = lambda bi, p, qi: (bi, qi, p)
    seq_map = lambda bi, p, qi: (bi, 0, p)
    return pl.pallas_call(
        functools.partial(_fox_flash_kernel, tk=tk),
        grid=(bsz, npair, t // tq),
        in_specs=[pl.BlockSpec((1, tq, LANES), q_map),
                  pl.BlockSpec((1, t, LANES), seq_map),
                  pl.BlockSpec((1, t, LANES), seq_map),
                  pl.BlockSpec((1, tq, 3 * nh), lambda bi, p, qi: (bi, qi, 0)),
                  pl.BlockSpec((1, t, 3 * nh), lambda bi, p, qi: (bi, 0, 0))],
        out_specs=pl.BlockSpec((1, tq, LANES), q_map),
        out_shape=jax.ShapeDtypeStruct((bsz, t, d), F32),
        scratch_shapes=[pltpu.VMEM((2, 2, tq, tk), F32),
                        pltpu.VMEM((2, tq, LANES), F32), pltpu.VMEM((2, tq, LANES), F32),
                        pltpu.VMEM((tq, LANES), F32)],
        compiler_params=pltpu.CompilerParams(
            dimension_semantics=("parallel", "parallel", "parallel"), vmem_limit_bytes=VMEM_LIMIT),
        name="fox_flash",
    )(q_b, k_b, v_b, pieces, pieces)


def _fox_decode_kernel(pt_ref, q_ref, *refs, npg):
    k_refs, v_refs, f_refs = refs[:npg], refs[npg:2 * npg], refs[2 * npg:3 * npg]
    for phase in _decode_phases(pl.program_id(1), q_ref, k_refs, v_refs, f_refs, *refs[3 * npg:]):
        phase()


def _decode_phases(pg, q_ref, k_refs, v_refs, f_refs, tail_ref, kn_ref, vn_ref, o_ref, m_ref, l_ref, acc_ref):
    npg = len(k_refs)

    @pl.when(pg == 0)
    def _():
        m_ref[...] = jnp.full_like(m_ref, NEG_BIG)
        l_ref[...] = jnp.zeros_like(l_ref)
        acc_ref[...] = jnp.zeros_like(acc_ref)

    nh, e, page = k_refs[0].shape[2:]
    d = nh * e
    row = lax.broadcasted_iota(jnp.int32, (nh, d), 0)
    lane = lax.broadcasted_iota(jnp.int32, (nh, d), 1)
    own = (lane >= row * e) & (lane < (row + 1) * e)
    q_bd = jnp.where(own, q_ref[0], 0.0)
    q_b = q_bd.astype(BF16)
    vals = {}

    pos = lax.broadcasted_iota(jnp.int32, (nh, page), 1)
    eye_h = lax.broadcasted_iota(jnp.int32, (nh, nh), 0) == lax.broadcasted_iota(jnp.int32, (nh, nh), 1)

    def bias(t):
        lf = f_refs[t][0, 0]
        suffix = lf
        shift = 1
        while shift < page:
            suffix = suffix + jnp.where(pos < page - shift, pltpu.roll(suffix, page - shift, 1), 0.0)
            shift *= 2
        tail_col = jnp.sum(jnp.where(eye_h, tail_ref[0, t:t + 1, :], 0.0), axis=1, keepdims=True)
        return (suffix - lf) + tail_col

    def scores():
        vals["s"] = [_nn(q_b, k_refs[t][0, 0].reshape(d, page).astype(BF16)) + bias(t)
                     for t in range(npg)]

    def combine():
        s = vals["s"]
        m_prev = m_ref[...]
        m_new = m_prev
        for t in range(npg):
            m_new = jnp.maximum(m_new, jnp.max(s[t], axis=1, keepdims=True))
        alpha = jnp.exp(m_prev - m_new)
        l_new = alpha * l_ref[...]
        acc = alpha * acc_ref[...]
        for t in range(npg):
            pr = jnp.exp(s[t] - m_new)
            l_new = l_new + jnp.sum(pr, axis=1, keepdims=True)
            acc = acc + _nt(pr.astype(BF16), v_refs[t][0, 0].reshape(d, page).astype(BF16))
        l_ref[...] = l_new
        acc_ref[...] = acc
        m_ref[...] = m_new
        vals.update(m=m_new, l=l_new, acc=acc)

    def finish():
        m_new, l_new, acc = vals["m"], vals["l"], vals["acc"]
        s_new = jnp.sum(q_bd * kn_ref[0], axis=1, keepdims=True)
        m_fin = jnp.maximum(m_new, s_new)
        a_old = jnp.exp(m_new - m_fin)
        p_new = jnp.exp(s_new - m_fin)
        l_fin = l_new * a_old + p_new
        full = (acc * a_old + p_new * vn_ref[0]) / l_fin
        o_ref[0] = jnp.sum(jnp.where(own, full, 0.0), axis=0, keepdims=True)

    return scores, combine, finish


def _fox_decode(layer, q, k_new, v_new, cache_kt, cache_vt, cache_ft, tail, page_table):
    n, d = q.shape
    n_pages = page_table.shape[1]
    nh, e, page = cache_kt.shape[2:]
    npg = _pick_tile(n_pages, (8, 4, 2, 1))

    def kv_spec(t):
        return pl.BlockSpec((1, 1, nh, e, page), lambda b, pg, pt: (layer, pt[b, pg * npg + t], 0, 0, 0))

    def f_spec(t):
        return pl.BlockSpec((1, 1, nh, page), lambda b, pg, pt: (layer, pt[b, pg * npg + t], 0, 0))

    row_spec = pl.BlockSpec((1, 1, d), lambda b, pg, pt: (b, 0, 0))
    grid_spec = pltpu.PrefetchScalarGridSpec(
        num_scalar_prefetch=1,
        grid=(n, n_pages // npg),
        in_specs=[row_spec] + [kv_spec(t) for t in range(npg)] * 2 + [f_spec(t) for t in range(npg)]
                 + [pl.BlockSpec((1, npg, nh), lambda b, pg, pt: (b, pg, 0)), row_spec, row_spec],
        out_specs=row_spec,
        scratch_shapes=[pltpu.VMEM((nh, 1), F32), pltpu.VMEM((nh, 1), F32), pltpu.VMEM((nh, d), F32)],
    )
    o = pl.pallas_call(
        functools.partial(_fox_decode_kernel, npg=npg),
        grid_spec=grid_spec,
        out_shape=jax.ShapeDtypeStruct((n, 1, d), F32),
        compiler_params=pltpu.CompilerParams(
            dimension_semantics=("parallel", "arbitrary"), vmem_limit_bytes=VMEM_LIMIT),
        name="fox_decode",
    )(page_table, q.reshape(n, 1, d), *([cache_kt] * npg), *([cache_vt] * npg), *([cache_ft] * npg),
      tail, k_new.reshape(n, 1, d), v_new.reshape(n, 1, d))
    return o.reshape(n, d)


def _fox_attn_fused_kernel(pt_ref, fq_ref, fk_ref, fv_ref, fcq_ref, fck_ref, dq_ref, *refs, npg, tk, n_groups):
    k_refs, v_refs, f_refs = refs[:npg], refs[npg:2 * npg], refs[2 * npg:3 * npg]
    (tail_ref, kn_ref, vn_ref, o_ref, do_ref,
     s_ref, m_ref, l_ref, acc_ref, dm_ref, dl_ref, dacc_ref) = refs[3 * npg:]
    p, qi = pl.program_id(1), pl.program_id(2)
    step = (pl.program_id(0) * pl.num_programs(1) + p) * pl.num_programs(2) + qi
    decode = _decode_phases(step % n_groups, dq_ref, k_refs, v_refs, f_refs, tail_ref, kn_ref, vn_ref,
                            do_ref, dm_ref, dl_ref, dacc_ref)
    _flash_body(p, qi, fq_ref, fk_ref, fv_ref, fcq_ref, fck_ref, o_ref, s_ref, m_ref, l_ref, acc_ref, tk=tk,
                side_work=decode)


def _fox_attention(layer, q_b, k_b, v_b, pieces, dq, dk_new, dv_new, cache_kt, cache_vt, cache_ft, tail,
                   page_table, tile=512):
    bsz, t, d = q_b.shape
    n = dq.shape[0]
    nh, e, page = cache_kt.shape[2:]
    n_pages = page_table.shape[1]
    npair = d // LANES
    tq = tk = _pick_tile(t, (tile, 256, 128))
    nq = t // tq
    npg = _pick_tile(n_pages, (8, 4, 2, 1))
    n_groups = n_pages // npg
    if bsz * npair * nq != n * n_groups:
        return (_fox_flash(q_b, k_b, v_b, pieces, tile),
                _fox_decode(layer, dq, dk_new, dv_new, cache_kt, cache_vt, cache_ft, tail, page_table))

    def seq_of(bi, p, qi):
        return ((bi * npair + p) * nq + qi) // n_groups

    def grp_of(bi, p, qi):
        return ((bi * npair + p) * nq + qi) % n_groups

    def kv_spec(j):
        return pl.BlockSpec((1, 1, nh, e, page),
                            lambda bi, p, qi, pt: (layer, pt[seq_of(bi, p, qi), grp_of(bi, p, qi) * npg + j], 0, 0, 0))

    def f_spec(j):
        return pl.BlockSpec((1, 1, nh, page),
                            lambda bi, p, qi, pt: (layer, pt[seq_of(bi, p, qi), grp_of(bi, p, qi) * npg + j], 0, 0))

    q_map = lambda bi, p, qi, pt: (bi, qi, p)
    seq_map = lambda bi, p, qi, pt: (bi, 0, p)
    row_spec = pl.BlockSpec((1, 1, d), lambda bi, p, qi, pt: (seq_of(bi, p, qi), 0, 0))
    grid_spec = pltpu.PrefetchScalarGridSpec(
        num_scalar_prefetch=1,
        grid=(bsz, npair, nq),
        in_specs=[pl.BlockSpec((1, tq, LANES), q_map),
                  pl.BlockSpec((1, t, LANES), seq_map),
                  pl.BlockSpec((1, t, LANES), seq_map),
                  pl.BlockSpec((1, tq, 3 * nh), lambda bi, p, qi, pt: (bi, qi, 0)),
                  pl.BlockSpec((1, t, 3 * nh), lambda bi, p, qi, pt: (bi, 0, 0)),
                  row_spec]
                 + [kv_spec(j) for j in range(npg)] * 2 + [f_spec(j) for j in range(npg)]
                 + [pl.BlockSpec((1, npg, nh), lambda bi, p, qi, pt: (seq_of(bi, p, qi), grp_of(bi, p, qi), 0)),
                    row_spec, row_spec],
        out_specs=[pl.BlockSpec((1, tq, LANES), q_map), row_spec],
        scratch_shapes=[pltpu.VMEM((2, 2, tq, tk), F32),
                        pltpu.VMEM((2, tq, LANES), F32), pltpu.VMEM((2, tq, LANES), F32),
                        pltpu.VMEM((tq, LANES), F32),
                        pltpu.VMEM((nh, 1), F32), pltpu.VMEM((nh, 1), F32), pltpu.VMEM((nh, d), F32)],
    )
    o, do = pl.pallas_call(
        functools.partial(_fox_attn_fused_kernel, npg=npg, tk=tk, n_groups=n_groups),
        grid_spec=grid_spec,
        out_shape=[jax.ShapeDtypeStruct((bsz, t, d), F32), jax.ShapeDtypeStruct((n, 1, d), F32)],
        compiler_params=pltpu.CompilerParams(
            dimension_semantics=("arbitrary", "arbitrary", "arbitrary"), vmem_limit_bytes=VMEM_LIMIT),
        name="fox_attention",
    )(page_table, q_b, k_b, v_b, pieces, pieces, dq.reshape(n, 1, d),
      *([cache_kt] * npg), *([cache_vt] * npg), *([cache_ft] * npg), tail,
      dk_new.reshape(n, 1, d), dv_new.reshape(n, 1, d))
    return o, do.reshape(n, d)


def _rmsnorm(x, g):
    xf = x.astype(F32)
    y = xf * lax.rsqrt(jnp.mean(xf * xf, axis=-1, keepdims=True) + RMS_EPS)
    return (y * g.astype(F32)).astype(x.dtype)


def _rwkv_proj_kernel(x_ref, hp_ref, g_ref, mu_ref, w_ref, w0_ref, w1_ref, w2_ref, a0_ref, a1_ref, a2_ref,
                      kk_ref, ka_ref, seg_ref, segt_ref,
                      r_ref, lw_ref, kf_ref, v_ref, nkk_ref, kka_ref, gate_ref, carry_ref,
                      *, tiles_per_seq):
    x = x_ref[...]
    tm = x.shape[0]
    g = g_ref[...]
    h = x * lax.rsqrt(jnp.mean(x * x, axis=-1, keepdims=True) + RMS_EPS) * g
    if tiles_per_seq == 0:
        h_prev = hp_ref[...]
    else:
        first = pl.program_id(0) % tiles_per_seq == 0
        before = jnp.where(first, hp_ref[0], carry_ref[...])
        row = lax.broadcasted_iota(jnp.int32, x.shape, 0)
        h_prev = jnp.where(row == 0, before, pltpu.roll(h, 1, 0))
        carry_ref[...] = h[tm - 1:tm, :]
    xx = h_prev - h
    mix = lambda s: (h + xx * mu_ref[s:s + 1, :]).astype(BF16)
    k = _nn(mix(1), w_ref[1])
    w_mid = _nn(mix(4), w1_ref[...])
    a_mid = _nn(mix(5), a1_ref[...])
    r_ref[...] = _nn(mix(0), w_ref[0])
    kk = k * kk_ref[...]
    kk_sq = _seg_sum(kk * kk, seg_ref[...])
    z = w0_ref[...] + _nn(jnp.tanh(w_mid).astype(BF16), w2_ref[...])
    a_pre = _nn(a_mid.astype(BF16), a2_ref[...])
    v_ref[...] = _nn(mix(2), w_ref[2])
    kk = kk * _seg_bcast(1.0 / jnp.maximum(jnp.sqrt(kk_sq), 1e-12), segt_ref[...])
    gate_ref[...] = _nn(mix(3), w_ref[3])
    lw_ref[...] = -jnp.exp(jnp.minimum(z, 0.0) - jnp.log(1.0 + jnp.exp(-jnp.abs(z))) - 0.5)
    a = jax.nn.sigmoid(a0_ref[...] + a_pre)
    kf_ref[...] = k * (1.0 + (a - 1.0) * ka_ref[...])
    nkk_ref[...] = -kk
    kka_ref[...] = kk * a


def _rwkv_out_kernel(o_ref, r_ref, kf_ref, v_ref, gate_ref, x_ref, lw_ref, lb_ref, rk_ref, w_ref,
                     seg_ref, segt_ref, y_ref):
    seg = seg_ref[...]
    seg_t = segt_ref[...]
    tm, d = o_ref.shape
    inv_e = 1.0 / (d // seg.shape[1])
    n_groups = 2 if tm % 16 == 0 else 1
    rows = [slice(i * (tm // n_groups), (i + 1) * (tm // n_groups)) for i in range(n_groups)]
    o = [o_ref[rs, :] for rs in rows]
    o_sum = [_seg_sum(x, seg) for x in o]
    rk_sum = [_seg_sum(r_ref[rs, :] * kf_ref[rs, :] * rk_ref[...], seg) for rs in rows]
    cen = [x - _seg_bcast(s * inv_e, seg_t) for x, s in zip(o, o_sum)]
    var = [_seg_sum(c * c, seg) * inv_e for c in cen]
    bonus = [_seg_bcast(s, seg_t) * v_ref[rs, :] for s, rs in zip(rk_sum, rows)]
    gn = [c * _seg_bcast(lax.rsqrt(vr + LNX_EPS), seg_t) * lw_ref[...] + lb_ref[...] for c, vr in zip(cen, var)]
    for rs, gn_i, bonus_i in zip(rows, gn, bonus):
        g = gate_ref[rs, :]
        y = ((gn_i + bonus_i) * (g * jax.nn.sigmoid(g))).astype(BF16)
        y_ref[rs, :] = x_ref[rs, :] + _nn(y, w_ref[...])


def _rwkv_layer(x, h_prev0, states, layer, prm, nh):
    (norm, mu, w_in, w0, w1, w2, a0, a1, a2, k_k, k_a, r_k, lnx_w, lnx_b, w_out) = prm
    bsz, t, d = x.shape
    e = d // nh
    m = bsz * t
    seg, seg_t = _seg_mats(d, nh)
    row = lambda z: z.reshape(1, -1).astype(F32)
    full = lambda arr: pl.BlockSpec(arr.shape, lambda i: (0,) * arr.ndim)
    x2 = x.reshape(m, d)
    if t == 1:
        tm = _pick_tile(m, (256, 128))
        tiles_per_seq = 0
        hp = h_prev0.astype(F32)
        hp_spec = pl.BlockSpec((tm, d), lambda i: (i, 0))
    else:
        tm = _pick_tile(t, (256, 128))
        tiles_per_seq = t // tm
        hp = h_prev0.astype(F32).reshape(bsz, 1, d)
        hp_spec = pl.BlockSpec((1, 1, d), lambda i: (i // tiles_per_seq, 0, 0))
    tile = pl.BlockSpec((tm, d), lambda i: (i, 0))
    consts = [row(norm), mu.astype(F32), w_in.astype(BF16), row(w0), w1.astype(BF16), w2.astype(BF16),
              row(a0), a1.astype(BF16), a2.astype(BF16), row(k_k), row(k_a), seg, seg_t]
    r, lw, kf, v, neg_kk, kk_a, gate = pl.pallas_call(
        functools.partial(_rwkv_proj_kernel, tiles_per_seq=tiles_per_seq),
        grid=(m // tm,),
        in_specs=[tile, hp_spec] + [full(c) for c in consts],
        out_specs=[tile] * 7,
        out_shape=[jax.ShapeDtypeStruct((m, d), F32)] * 7,
        scratch_shapes=[pltpu.VMEM((1, d), F32)],
        compiler_params=pltpu.CompilerParams(
            dimension_semantics=("arbitrary",), vmem_limit_bytes=VMEM_LIMIT),
        name="rwkv_proj",
    )(x2, hp, *consts)

    if states is None:
        seq = lambda z: z.reshape(bsz, t, d)
        o, zt = _wkv_chunked(seq(r), seq(lw), seq(kf), seq(v), seq(neg_kk), seq(kk_a))
        zt = zt.reshape(bsz, d // LANES, 2, e, 2, e)
        s_fin = jnp.stack([zt[:, :, 0, :, 0, :], zt[:, :, 1, :, 1, :]], axis=2).reshape(bsz, nh, e, e)
    else:
        o, s_fin = _wkv_step(states, layer, r, lw, kf, v, neg_kk, kk_a)

    tmo = _pick_tile(m, (256, 128))
    tile_o = pl.BlockSpec((tmo, d), lambda i: (i, 0))
    consts_o = [row(lnx_w), row(lnx_b), row(r_k), w_out.astype(BF16), seg, seg_t]
    x_new = pl.pallas_call(
        _rwkv_out_kernel,
        grid=(m // tmo,),
        in_specs=[tile_o] * 6 + [full(c) for c in consts_o],
        out_specs=tile_o,
        out_shape=jax.ShapeDtypeStruct((m, d), F32),
        compiler_params=pltpu.CompilerParams(
            dimension_semantics=("parallel",), vmem_limit_bytes=VMEM_LIMIT),
        name="rwkv_out",
    )(o.reshape(m, d), r, kf, v, gate, x2, *consts_o)
    h_last = _rmsnorm(x[:, -1, :], norm)
    return x_new.reshape(bsz, t, d), s_fin, h_last


def _split2(x):
    hi = x.astype(BF16)
    return hi, (x - hi.astype(F32)).astype(BF16)


def _seg_sum(x, seg):
    hi, lo = _split2(x)
    return _nn(hi, seg) + _nn(lo, seg)


def _seg_bcast(y, seg_t):
    hi, lo = _split2(y)
    return _nn(hi, seg_t) + _nn(lo, seg_t)


def _seg_mats(d, nh):
    lane_head = jnp.arange(d, dtype=jnp.int32) // (d // nh)
    seg = (lane_head[:, None] == jnp.arange(nh, dtype=jnp.int32)[None, :]).astype(BF16)
    return seg, seg.T


def _fox_proj_kernel(x_ref, g_ref, w_ref, wf_ref, bf_ref, qn_ref, kn_ref, seg_ref, segt_ref,
                     q_ref, k_ref, v_ref, gate_ref, lf_ref, *attn_refs, q_scale, tiles_per_seq):
    x = x_ref[...]
    tm = x.shape[0]
    nh = lf_ref.shape[1]
    d = x.shape[1]
    hb = (x * lax.rsqrt(jnp.mean(x * x, axis=-1, keepdims=True) + RMS_EPS) * g_ref[...]).astype(BF16)
    seg = seg_ref[...]
    seg_t = segt_ref[...]
    inv_e = 1.0 / (d // seg.shape[1])

    q = _nn(hb, w_ref[:, 0:d])
    k = _nn(hb, w_ref[:, d:2 * d])
    q_ms = _seg_sum(q * q, seg)
    v = _nn(hb, w_ref[:, 2 * d:3 * d])
    k_ms = _seg_sum(k * k, seg)
    q = q * _seg_bcast(lax.rsqrt(q_ms * inv_e + RMS_EPS), seg_t) * qn_ref[...]
    gate_ref[...] = _nn(hb, w_ref[:, 3 * d:4 * d])
    k = k * _seg_bcast(lax.rsqrt(k_ms * inv_e + RMS_EPS), seg_t) * kn_ref[...]
    k_ref[...] = k
    v_ref[...] = v
    f = _nn(hb, wf_ref[...]) + bf_ref[...]
    logf = jnp.minimum(f, 0.0) - jnp.log(1.0 + jnp.exp(-jnp.abs(f)))
    lf_ref[...] = logf[:, :nh]
    if not attn_refs:
        q_ref[...] = q
        return
    kb_ref, vb_ref, pieces_ref, carry_ref = attn_refs
    q_ref[...] = (q * q_scale).astype(BF16)
    kb_ref[...] = k.astype(BF16)
    vb_ref[...] = v.astype(BF16)
    tri = (lax.broadcasted_iota(jnp.int32, (tm, tm), 0)
           >= lax.broadcasted_iota(jnp.int32, (tm, tm), 1)).astype(BF16)
    f1 = logf.astype(BF16)
    rem = logf - f1.astype(F32)
    f2 = rem.astype(BF16)
    f3 = (rem - f2.astype(F32)).astype(BF16)
    first = pl.program_id(0) % tiles_per_seq == 0
    c = _nn(tri, f1) + (_nn(tri, f2) + _nn(tri, f3)) + jnp.where(first, 0.0, carry_ref[...])
    carry_ref[...] = c[tm - 1:tm, :]
    top16 = lambda z: lax.bitcast_convert_type(
        lax.bitcast_convert_type(z, jnp.uint32) & jnp.uint32(0xFFFF0000), F32)
    c2 = c * LOG2E
    hi = top16(c2)
    mid = top16(c2 - hi)
    lane = lax.broadcasted_iota(jnp.int32, c2.shape, 1)
    pieces_ref[...] = jnp.where(lane < nh, hi, jnp.where(lane < 2 * nh, mid, top16(c2 - hi - mid))).astype(BF16)


def _fox_project(x, norm, w_in, b_f, qn_g, kn_g, nh, attn_q_scale=None):
    bsz, t, d = x.shape
    m = bsz * t
    for_attn = attn_q_scale is not None
    tm = _pick_tile(t if for_attn else m, (256, 128))
    n_f = 3 * nh if for_attn else nh
    seg, seg_t = _seg_mats(d, nh)
    row = lambda z: z.reshape(1, -1).astype(F32)
    full = lambda a: pl.BlockSpec(a.shape, lambda i: (0,) * a.ndim)
    w_f = w_in[:, 4 * d:].astype(BF16)
    consts = [row(norm), w_in[:, :4 * d].astype(BF16), jnp.tile(w_f, (1, n_f // nh)), row(jnp.tile(b_f, n_f // nh)),
              row(jnp.tile(qn_g, nh)), row(jnp.tile(kn_g, nh)), seg, seg_t]
    tile = pl.BlockSpec((tm, d), lambda i: (i, 0))
    narrow = lambda w: pl.BlockSpec((tm, w), lambda i: (i, 0))
    attn_specs = [tile, tile, narrow(n_f)] if for_attn else []
    attn_shapes = ([jax.ShapeDtypeStruct((m, d), BF16)] * 2 + [jax.ShapeDtypeStruct((m, n_f), BF16)]) if for_attn else []
    outs = pl.pallas_call(
        functools.partial(_fox_proj_kernel, q_scale=attn_q_scale, tiles_per_seq=t // tm),
        grid=(m // tm,),
        in_specs=[tile] + [full(a) for a in consts],
        out_specs=[tile] * 4 + [narrow(nh)] + attn_specs,
        out_shape=([jax.ShapeDtypeStruct((m, d), BF16 if for_attn else F32)]
                   + [jax.ShapeDtypeStruct((m, d), F32)] * 3 + [jax.ShapeDtypeStruct((m, nh), F32)] + attn_shapes),
        scratch_shapes=[pltpu.VMEM((1, n_f), F32)] if for_attn else [],
        compiler_params=pltpu.CompilerParams(
            dimension_semantics=("arbitrary" if for_attn else "parallel",), vmem_limit_bytes=VMEM_LIMIT),
        name="fox_proj",
    )(x.reshape(m, d), *consts)
    q, k, v, gate, logf = outs[:5]
    seq = lambda z: z.reshape(bsz, t, z.shape[-1])
    return seq(q), seq(k), seq(v), gate, logf.reshape(bsz, t, nh), tuple(seq(z) for z in outs[5:])


def _gate_out_kernel(o_ref, gate_ref, x_ref, w_ref, y_ref):
    g = gate_ref[...]
    y = (o_ref[...] * (g * jax.nn.sigmoid(g))).astype(BF16)
    y_ref[...] = x_ref[...] + _nn(y, w_ref[...])


def _fox_finish(x, o, gate, w_out):
    bsz, t, d = x.shape
    m = bsz * t
    tm = _pick_tile(m, (512, 256, 128))
    tile = pl.BlockSpec((tm, d), lambda i: (i, 0))
    y = pl.pallas_call(
        _gate_out_kernel,
        grid=(m // tm,),
        in_specs=[tile, tile, tile, pl.BlockSpec((d, d), lambda i: (0, 0))],
        out_specs=tile,
        out_shape=jax.ShapeDtypeStruct((m, d), F32),
        compiler_params=pltpu.CompilerParams(
            dimension_semantics=("parallel",), vmem_limit_bytes=VMEM_LIMIT),
        name="gate_out",
    )(o.reshape(m, d), gate, x.reshape(m, d), w_out.astype(BF16))
    return y.reshape(bsz, t, d)


def kernel(x_prompt, x_sample, state_wkv, state_shift, cache_k, cache_v, cache_logf, page_table,
           norm_a, mu_a, w_in_a, w0_a, w1_a, w2_a, a0_a, a1_a, a2_a, kk_a, ka_a, rk_a, lnx_w_a, lnx_b_a, w_out_a,
           norm_b, w_in_b, bf_b, qn_b, kn_b, w_out_b):
    bsz, t, d = x_prompt.shape
    nb, ts, _ = x_sample.shape
    assert ts == 1, "the sample group carries one new token per sequence"
    nh, e = rk_a.shape[1], rk_a.shape[2]
    assert d == nh * e and 2 * e == LANES and t % WKV_CHUNK == 0
    depth = norm_a.shape[0] + norm_b.shape[0]
    scale = e ** -0.5
    ckt = jnp.transpose(cache_k, (0, 1, 3, 4, 2))
    cvt = jnp.transpose(cache_v, (0, 1, 3, 4, 2))
    cft = jnp.transpose(cache_logf, (0, 1, 3, 2)).astype(F32)
    page_mass = jnp.sum(cft, axis=-1)

    xp, xs = x_prompt, x_sample
    kp_l, vp_l, fp_l, sp_l, hp_l = [], [], [], [], []
    ks_l, vs_l, fs_l, hs_l = [], [], [], []
    states_s = state_wkv.astype(F32)
    for i in range(depth):
        j = i // 2
        if i % 2 == 0:
            prm = (norm_a[j], mu_a[j], w_in_a[j], w0_a[j], w1_a[j], w2_a[j], a0_a[j], a1_a[j], a2_a[j],
                   kk_a[j], ka_a[j], rk_a[j], lnx_w_a[j], lnx_b_a[j], w_out_a[j])
            xp, s_p, l_p = _rwkv_layer(xp, jnp.zeros((bsz, d), xp.dtype), None, j, prm, nh)
            xs, states_s, l_s = _rwkv_layer(xs, state_shift[j], states_s, j, prm, nh)
            sp_l.append(s_p); hp_l.append(l_p); hs_l.append(l_s)
        else:
            qp_b, kp, vp, gp, lfp, attn_ops = _fox_project(xp, norm_b[j], w_in_b[j], bf_b[j], qn_b[j], kn_b[j], nh,
                                                        attn_q_scale=scale * LOG2E)
            qs, ksn, vsn, gs, lfs, _ = _fox_project(xs, norm_b[j], w_in_b[j], bf_b[j], qn_b[j], kn_b[j], nh)
            seq_mass = page_mass[j][page_table]
            later = jnp.cumsum(seq_mass[:, ::-1, :], axis=1)[:, ::-1, :] - seq_mass
            tail = later + lfs.reshape(nb, 1, nh)
            op, osm = _fox_attention(j, qp_b, *attn_ops, (qs * scale).reshape(nb, d), ksn.reshape(nb, d),
                                     vsn.reshape(nb, d), ckt, cvt, cft, tail, page_table)
            xp = _fox_finish(xp, op, gp, w_out_b[j])
            xs = _fox_finish(xs, osm.reshape(nb, 1, d), gs, w_out_b[j])
            hd4 = lambda z, n_, t_: z.reshape(n_, t_, nh, e)
            kp_l.append(hd4(kp, bsz, t)); vp_l.append(hd4(vp, bsz, t)); fp_l.append(lfp)
            ks_l.append(hd4(ksn, nb, ts)); vs_l.append(hd4(vsn, nb, ts)); fs_l.append(lfs)
    return (xp, xs,
            jnp.stack(kp_l), jnp.stack(vp_l), jnp.stack(fp_l), jnp.stack(sp_l), jnp.stack(hp_l),
            jnp.stack(ks_l), jnp.stack(vs_l), jnp.stack(fs_l), states_s, jnp.stack(hs_l))
```

```python
import functools

import jax
import jax.numpy as jnp
from jax import lax
from jax.experimental import pallas as pl
from jax.experimental.pallas import tpu as pltpu

F32 = jnp.float32
BF16 = jnp.bfloat16

LANES = 128
RMS_EPS = 1e-6
LNX_EPS = 64e-5
NEG_BIG = -1e30
LOG2E = 1.4426950408889634
WKV_CHUNK = 64
WKV_CHUNKS_PER_STEP = (4, 2, 1)
VMEM_LIMIT = 48 * 1024 * 1024


def _nt(x, y):
    return lax.dot_general(x, y, (((1,), (1,)), ((), ())), preferred_element_type=F32)


def _tn(x, y):
    return lax.dot_general(x, y, (((0,), (0,)), ((), ())), preferred_element_type=F32)


def _nn(x, y):
    return jnp.dot(x, y, preferred_element_type=F32)


def _pick_tile(n, candidates):
    for c in candidates:
        if n % c == 0:
            return c
    return n


def _wkv_chunk_kernel(r_ref, lw_ref, k_ref, v_ref, a_ref, b_ref, o_ref, zt_ref):
    c = pl.program_id(1)

    @pl.when(c == 0)
    def _():
        zt_ref[...] = jnp.zeros_like(zt_ref)

    L = WKV_CHUNK
    n_chunks = r_ref.shape[1] // L
    n = 2 * L
    half = LANES // 2
    npair = r_ref.shape[2] // LANES

    ti = lax.broadcasted_iota(jnp.int32, (L, L), 0)
    tj = lax.broadcasted_iota(jnp.int32, (L, L), 1)
    tri = (ti >= tj).astype(BF16)
    head0 = lax.broadcasted_iota(jnp.int32, (L, LANES), 1) < half
    i = lax.broadcasted_iota(jnp.int32, (n, n), 0)
    j = lax.broadcasted_iota(jnp.int32, (n, n), 1)
    strict = i > j
    incl = i >= j
    eye = jnp.where(i == j, 1.0, 0.0)
    first = (i >> 1) == (j >> 1)
    levels = []
    lvl = 1
    while (2 << lvl) <= L:
        levels.append(((i >> (lvl + 1)) == (j >> (lvl + 1))) & ((i >> lvl) != (j >> lvl)))
        lvl += 1

    def stack(x):
        return jnp.concatenate([jnp.where(head0, x, 0.0), jnp.where(head0, 0.0, x)], axis=0)

    pairs = range(npair)
    units = range(n_chunks * npair)
    sls = [(slice((q // npair) * L, (q // npair + 1) * L), slice((q % npair) * LANES, (q % npair + 1) * LANES))
           for q in units]
    ar, bk, bk_h, vs, p_last = [], [], [], [], []
    for q in units:
        r, lw, k, v, a, b = (ref[0, sls[q][0], sls[q][1]] for ref in (r_ref, lw_ref, k_ref, v_ref, a_ref, b_ref))
        lw1 = lw.astype(BF16)
        rem = lw - lw1.astype(F32)
        lw2 = rem.astype(BF16)
        lw3 = (rem - lw2.astype(F32)).astype(BF16)
        cum = _nn(tri, lw1) + (_nn(tri, lw2) + _nn(tri, lw3))
        p_inc = jnp.exp(cum)
        p_exc = jnp.exp(cum - lw)
        p_inv = jnp.exp(-cum)
        pl_ = p_inc[L - 1:L, :]
        bk_f = jnp.concatenate([stack(b * p_inv), stack(k * p_inv)], axis=0)
        ar.append(jnp.concatenate([stack(a * p_exc), stack(r * p_inc)], axis=0).astype(BF16))
        bk.append(bk_f.astype(BF16))
        bk_h.append((bk_f * pl_).astype(BF16))
        vs.append(stack(v).astype(BF16))
        p_last.append(pl_)

    g = [_nt(ar[q], bk[q]) for q in units]
    a_ab = [jnp.where(strict, g[q][:n, :n], 0.0) for q in units]
    ak_rk = [jnp.concatenate([jnp.where(strict, g[q][:n, n:], 0.0),
                              jnp.where(incl, g[q][n:, n:], 0.0)], axis=0).astype(BF16) for q in units]
    a_rb = [jnp.where(incl, g[q][n:, :n], 0.0).astype(BF16) for q in units]

    def compress(full, size):
        return functools.reduce(lambda x, y: x + y, [full[i * size:(i + 1) * size] for i in range(n // size)])

    def expand(comp, size):
        lane_blk = lax.broadcasted_iota(jnp.int32, (size, n), 1) >> (size.bit_length() - 1)
        return jnp.concatenate([jnp.where(lane_blk == i, comp, 0.0) for i in range(n // size)], axis=0)

    sub = L // 2
    t_full = [eye + jnp.where(first, a_ab[q], 0.0) for q in units]
    t_comp = [compress(t_full[q], sub) for q in units]
    for off in levels[:-1]:
        x = [_nn(t_comp[q].astype(BF16), jnp.where(off, a_ab[q], 0.0).astype(BF16)).astype(BF16) for q in units]
        t_comp = [t_comp[q] + _nn(x[q], t_full[q].astype(BF16)) for q in units]
        t_full = [expand(t_comp[q], sub) for q in units]
    t_comp = [compress(t_full[q], L) for q in units]
    x = [_nn(t_comp[q].astype(BF16), jnp.where(levels[-1], a_ab[q], 0.0).astype(BF16)).astype(BF16) for q in units]
    t_comp = [(t_comp[q] + _nn(x[q], t_full[q].astype(BF16))).astype(BF16) for q in units]

    zt = [zt_ref[0, p] for p in pairs]
    for c in range(n_chunks):
        qs = [c * npair + p for p in pairs]
        y = [_nt(ar[q], zt[p].astype(BF16)) + _nn(ak_rk[q], vs[q]) for p, q in zip(pairs, qs)]
        u_b = [stack(_nn(t_comp[q], y[p][:n].astype(BF16))).astype(BF16) for p, q in zip(pairs, qs)]
        for p, q in zip(pairs, qs):
            o_st = y[p][n:] + _nn(a_rb[q], u_b[p])
            o_ref[0, sls[q][0], sls[q][1]] = o_st[:L] + o_st[L:]
            zt[p] = zt[p] * p_last[q] + _tn(jnp.concatenate([u_b[p], vs[q]], axis=0), bk_h[q])
    for p in pairs:
        zt_ref[0, p] = zt[p]


def _wkv_chunked(r, lw, k, v, a, b):
    bsz, t, d = r.shape
    npair = d // LANES
    tb = _pick_tile(t, tuple(WKV_CHUNK * c for c in WKV_CHUNKS_PER_STEP))
    spec = pl.BlockSpec((1, tb, d), lambda bi, c: (bi, c, 0))
    return pl.pallas_call(
        _wkv_chunk_kernel,
        grid=(bsz, t // tb),
        in_specs=[spec] * 6,
        out_specs=[spec, pl.BlockSpec((1, npair, LANES, LANES), lambda bi, c: (bi, 0, 0, 0))],
        out_shape=[jax.ShapeDtypeStruct((bsz, t, d), F32),
                   jax.ShapeDtypeStruct((bsz, npair, LANES, LANES), F32)],
        compiler_params=pltpu.CompilerParams(
            dimension_semantics=("parallel", "arbitrary"), vmem_limit_bytes=VMEM_LIMIT),
        name="wkv_chunked",
    )(r, lw, k, v, a, b)


def _wkv_step_kernel(s_ref, r_ref, lw_ref, k_ref, v_ref, a_ref, b_ref, so_ref, o_ref):
    _, bb, nh, e, _ = s_ref.shape
    half = LANES // 2
    ii = lax.broadcasted_iota(jnp.int32, (e, LANES), 0)
    jj = lax.broadcasted_iota(jnp.int32, (e, LANES), 1)
    eye_lo = ii == jj
    eye_hi = ii + half == jj

    for i in range(bb):
        units = []
        for h in range(nh):
            p, par = divmod(h, 2)
            sl = slice(p * LANES, (p + 1) * LANES)
            r, lw, k, v, a, b = (ref[i:i + 1, sl] for ref in (r_ref, lw_ref, k_ref, v_ref, a_ref, b_ref))
            w = jnp.exp(lw)
            if par == 0:
                rh, wh, kh, ah, bh = (z[:, :e] for z in (r, w, k, a, b))
            else:
                rh, wh, kh, ah, bh = (pltpu.roll(z, half, 1)[:, :e] for z in (r, w, k, a, b))
            s = s_ref[0, i, h]
            v_col = jnp.sum(jnp.where(eye_hi if par else eye_lo, v, 0.0), axis=1, keepdims=True)
            sa = jnp.sum(s * ah, axis=1, keepdims=True)
            units.append((s, rh, wh, kh, bh, v_col, sa))
        o_cols = []
        for h, (s, rh, wh, kh, bh, v_col, sa) in enumerate(units):
            s_new = s * wh + sa * bh + v_col * kh
            so_ref[0, i, h] = s_new
            o_cols.append(jnp.sum(s_new * rh, axis=1, keepdims=True))
        for p in range(nh // 2):
            o_ref[i:i + 1, p * LANES:(p + 1) * LANES] = (
                jnp.sum(jnp.where(eye_lo, o_cols[2 * p], 0.0), axis=0, keepdims=True)
                + jnp.sum(jnp.where(eye_hi, o_cols[2 * p + 1], 0.0), axis=0, keepdims=True))


def _wkv_step(states, layer, r, lw, k, v, a, b):
    _, n, h, e, _ = states.shape
    d = h * e
    bb = _pick_tile(n, (8,))
    s_spec = pl.BlockSpec((1, bb, h, e, e), lambda i: (layer, i, 0, 0, 0))
    row_spec = pl.BlockSpec((bb, d), lambda i: (i, 0))
    states, o = pl.pallas_call(
        _wkv_step_kernel,
        grid=(n // bb,),
        in_specs=[s_spec] + [row_spec] * 6,
        out_specs=[s_spec, row_spec],
        out_shape=[jax.ShapeDtypeStruct(states.shape, F32), jax.ShapeDtypeStruct((n, d), F32)],
        input_output_aliases={0: 0},
        compiler_params=pltpu.CompilerParams(
            dimension_semantics=("parallel",), vmem_limit_bytes=VMEM_LIMIT),
        name="wkv_step",
    )(states, r, lw, k, v, a, b)
    return o, states


def _fox_flash_kernel(*refs, tk):
    _flash_body(pl.program_id(1), pl.program_id(2), *refs, tk=tk)


def _flash_body(p, qi, q_ref, k_ref, v_ref, cq_ref, ck_ref, o_ref, s_ref, m_ref, l_ref, acc_ref, *, tk,
                side_work=()):
    tq = q_ref.shape[1]
    half = LANES // 2
    n_piece_rows = cq_ref.shape[2]
    nh = n_piece_rows // 3
    heads = range(2)

    q_head0 = lax.broadcasted_iota(jnp.int32, (tq, LANES), 1) < half
    k_head0 = lax.broadcasted_iota(jnp.int32, (tk, LANES), 1) < half

    prow = lax.broadcasted_iota(jnp.int32, (n_piece_rows, LANES), 0)
    plane = lax.broadcasted_iota(jnp.int32, (n_piece_rows, LANES), 1)
    piece, head = prow >> (nh.bit_length() - 1), prow & (nh - 1)
    base = jnp.where(head == 2 * p, half, jnp.where(head == 2 * p + 1, 0, -LANES))
    place_q = jnp.where(plane == base + piece, 1.0, 0.0).astype(BF16)
    place_k = jnp.where(plane == base + piece + 3, -1.0, 0.0).astype(BF16)
    slot = lax.broadcasted_iota(jnp.int32, (1, LANES), 1) & (half - 1)
    ones_q = jnp.where((slot >= 3) & (slot < 6), 1.0, 0.0)
    ones_k = jnp.where(slot < 3, 1.0, 0.0)

    q = q_ref[0]
    q_aug = (_nn(cq_ref[0], place_q) + ones_q).astype(BF16)
    q_ops = (jnp.where(q_head0, q, q_aug), jnp.where(q_head0, q_aug, q))

    m_ref[...] = jnp.full_like(m_ref, NEG_BIG)
    l_ref[...] = jnp.zeros_like(l_ref)
    acc_ref[...] = jnp.zeros_like(acc_ref)

    def rows(kj):
        return slice(kj * tk, (kj + 1) * tk)

    def scores(kj, slot_idx):
        k = k_ref[0, rows(kj), :]
        k_aug = (_nn(ck_ref[0, rows(kj), :], place_k) + ones_k).astype(BF16)
        k_ops = (jnp.where(k_head0, k, k_aug), jnp.where(k_head0, k_aug, k))
        for h in heads:
            s_ref[slot_idx, h] = _nt(q_ops[h], k_ops[h])

    def consume(kj, slot_idx, on_diagonal):
        v = v_ref[0, rows(kj), :]
        zero = jnp.zeros_like(v)
        v_own = (jnp.where(k_head0, v, zero), jnp.where(k_head0, zero, v))
        if on_diagonal:
            keep = (lax.broadcasted_iota(jnp.int32, (tq, tk), 0)
                    >= lax.broadcasted_iota(jnp.int32, (tq, tk), 1))
        alpha, pv = [], []
        for h in heads:
            s = s_ref[slot_idx, h]
            if on_diagonal:
                s = jnp.where(keep, s, NEG_BIG)
            m_prev = m_ref[h]
            m_new = jnp.maximum(m_prev, jnp.max(s, axis=1, keepdims=True))
            a = jnp.exp2(m_prev - m_new)
            pr = jnp.exp2(s - jnp.concatenate([m_new] * (tk // LANES), axis=1))
            l_ref[h] = a * l_ref[h] + jnp.sum(pr, axis=1, keepdims=True)
            m_ref[h] = m_new
            alpha.append(a)
            pv.append(_nn(pr.astype(BF16), v_own[h]))
        acc_ref[...] = jnp.where(q_head0, alpha[0], alpha[1]) * acc_ref[...] + (pv[0] + pv[1])

    side = list(side_work) + [lambda: None] * (3 - len(side_work))

    def run(n_before):
        scores(0, 0)
        side[0]()
        for kj in range(n_before):
            scores(kj + 1, (kj + 1) % 2)
            consume(kj, kj % 2, False)
        side[1]()
        consume(n_before, n_before % 2, True)
        side[2]()
        o_ref[0] = acc_ref[...] / jnp.where(q_head0, l_ref[0], l_ref[1])

    for n_before in range(k_ref.shape[1] // tk):
        pl.when(qi == n_before)(functools.partial(run, n_before))


def _fox_flash(q_b, k_b, v_b, pieces, tile=512):
    bsz, t, d = q_b.shape
    nh = pieces.shape[-1] // 3
    assert nh & (nh - 1) == 0, "head count must be a power of two"
    npair = d // LANES
    tq = tk = _pick_tile(t, (tile, 256, 128))
    q_map = lambda bi, p, qi: (bi, qi, p)
    seq_map = lambda bi, p, qi: (bi, 0, p)
    return pl.pallas_call(
        functools.partial(_fox_flash_kernel, tk=tk),
        grid=(bsz, npair, t // tq),
        in_specs=[pl.BlockSpec((1, tq, LANES), q_map),
                  pl.BlockSpec((1, t, LANES), seq_map),
                  pl.BlockSpec((1, t, LANES), seq_map),
                  pl.BlockSpec((1, tq, 3 * nh), lambda bi, p, qi: (bi, qi, 0)),
                  pl.BlockSpec((1, t, 3 * nh), lambda bi, p, qi: (bi, 0, 0))],
        out_specs=pl.BlockSpec((1, tq, LANES), q_map),
        out_shape=jax.ShapeDtypeStruct((bsz, t, d), F32),
        scratch_shapes=[pltpu.VMEM((2, 2, tq, tk), F32),
                        pltpu.VMEM((2, tq, LANES), F32), pltpu.VMEM((2, tq, LANES), F32),
                        pltpu.VMEM((tq, LANES), F32)],
        compiler_params=pltpu.CompilerParams(
            dimension_semantics=("parallel", "parallel", "parallel"), vmem_limit_bytes=VMEM_LIMIT),
        name="fox_flash",
    )(q_b, k_b, v_b, pieces, pieces)


def _fox_decode_kernel(pt_ref, q_ref, *refs, npg):
    k_refs, v_refs, f_refs = refs[:npg], refs[npg:2 * npg], refs[2 * npg:3 * npg]
    for phase in _decode_phases(pl.program_id(1), q_ref, k_refs, v_refs, f_refs, *refs[3 * npg:]):
        phase()


def _decode_phases(pg, q_ref, k_refs, v_refs, f_refs, tail_ref, kn_ref, vn_ref, o_ref, m_ref, l_ref, acc_ref):
    npg = len(k_refs)

    @pl.when(pg == 0)
    def _():
        m_ref[...] = jnp.full_like(m_ref, NEG_BIG)
        l_ref[...] = jnp.zeros_like(l_ref)
        acc_ref[...] = jnp.zeros_like(acc_ref)

    nh, e, page = k_refs[0].shape[2:]
    d = nh * e
    row = lax.broadcasted_iota(jnp.int32, (nh, d), 0)
    lane = lax.broadcasted_iota(jnp.int32, (nh, d), 1)
    own = (lane >= row * e) & (lane < (row + 1) * e)
    q_bd = jnp.where(own, q_ref[0], 0.0)
    q_b = q_bd.astype(BF16)
    vals = {}

    pos = lax.broadcasted_iota(jnp.int32, (nh, page), 1)
    eye_h = lax.broadcasted_iota(jnp.int32, (nh, nh), 0) == lax.broadcasted_iota(jnp.int32, (nh, nh), 1)

    def bias(t):
        lf = f_refs[t][0, 0]
        suffix = lf
        shift = 1
        while shift < page:
            suffix = suffix + jnp.where(pos < page - shift, pltpu.roll(suffix, page - shift, 1), 0.0)
            shift *= 2
        tail_col = jnp.sum(jnp.where(eye_h, tail_ref[0, t:t + 1, :], 0.0), axis=1, keepdims=True)
        return (suffix - lf) + tail_col

    def scores():
        vals["s"] = [_nn(q_b, k_refs[t][0, 0].reshape(d, page).astype(BF16)) + bias(t)
                     for t in range(npg)]

    def combine():
        s = vals["s"]
        m_prev = m_ref[...]
        m_new = m_prev
        for t in range(npg):
            m_new = jnp.maximum(m_new, jnp.max(s[t], axis=1, keepdims=True))
        alpha = jnp.exp(m_prev - m_new)
        l_new = alpha * l_ref[...]
        acc = alpha * acc_ref[...]
        for t in range(npg):
            pr = jnp.exp(s[t] - m_new)
            l_new = l_new + jnp.sum(pr, axis=1, keepdims=True)
            acc = acc + _nt(pr.astype(BF16), v_refs[t][0, 0].reshape(d, page).astype(BF16))
        l_ref[...] = l_new
        acc_ref[...] = acc
        m_ref[...] = m_new
        vals.update(m=m_new, l=l_new, acc=acc)

    def finish():
        m_new, l_new, acc = vals["m"], vals["l"], vals["acc"]
        s_new = jnp.sum(q_bd * kn_ref[0], axis=1, keepdims=True)
        m_fin = jnp.maximum(m_new, s_new)
        a_old = jnp.exp(m_new - m_fin)
        p_new = jnp.exp(s_new - m_fin)
        l_fin = l_new * a_old + p_new
        full = (acc * a_old + p_new * vn_ref[0]) / l_fin
        o_ref[0] = jnp.sum(jnp.where(own, full, 0.0), axis=0, keepdims=True)

    return scores, combine, finish


def _fox_decode(layer, q, k_new, v_new, cache_kt, cache_vt, cache_ft, tail, page_table):
    n, d = q.shape
    n_pages = page_table.shape[1]
    nh, e, page = cache_kt.shape[2:]
    npg = _pick_tile(n_pages, (8, 4, 2, 1))

    def kv_spec(t):
        return pl.BlockSpec((1, 1, nh, e, page), lambda b, pg, pt: (layer, pt[b, pg * npg + t], 0, 0, 0))

    def f_spec(t):
        return pl.BlockSpec((1, 1, nh, page), lambda b, pg, pt: (layer, pt[b, pg * npg + t], 0, 0))

    row_spec = pl.BlockSpec((1, 1, d), lambda b, pg, pt: (b, 0, 0))
    grid_spec = pltpu.PrefetchScalarGridSpec(
        num_scalar_prefetch=1,
        grid=(n, n_pages // npg),
        in_specs=[row_spec] + [kv_spec(t) for t in range(npg)] * 2 + [f_spec(t) for t in range(npg)]
                 + [pl.BlockSpec((1, npg, nh), lambda b, pg, pt: (b, pg, 0)), row_spec, row_spec],
        out_specs=row_spec,
        scratch_shapes=[pltpu.VMEM((nh, 1), F32), pltpu.VMEM((nh, 1), F32), pltpu.VMEM((nh, d), F32)],
    )
    o = pl.pallas_call(
        functools.partial(_fox_decode_kernel, npg=npg),
        grid_spec=grid_spec,
        out_shape=jax.ShapeDtypeStruct((n, 1, d), F32),
        compiler_params=pltpu.CompilerParams(
            dimension_semantics=("parallel", "arbitrary"), vmem_limit_bytes=VMEM_LIMIT),
        name="fox_decode",
    )(page_table, q.reshape(n, 1, d), *([cache_kt] * npg), *([cache_vt] * npg), *([cache_ft] * npg),
      tail, k_new.reshape(n, 1, d), v_new.reshape(n, 1, d))
    return o.reshape(n, d)


def _fox_attn_fused_kernel(pt_ref, fq_ref, fk_ref, fv_ref, fcq_ref, fck_ref, dq_ref, *refs, npg, tk, n_groups):
    k_refs, v_refs, f_refs = refs[:npg], refs[npg:2 * npg], refs[2 * npg:3 * npg]
    (tail_ref, kn_ref, vn_ref, o_ref, do_ref,
     s_ref, m_ref, l_ref, acc_ref, dm_ref, dl_ref, dacc_ref) = refs[3 * npg:]
    p, qi = pl.program_id(1), pl.program_id(2)
    step = (pl.program_id(0) * pl.num_programs(1) + p) * pl.num_programs(2) + qi
    decode = _decode_phases(step % n_groups, dq_ref, k_refs, v_refs, f_refs, tail_ref, kn_ref, vn_ref,
                            do_ref, dm_ref, dl_ref, dacc_ref)
    _flash_body(p, qi, fq_ref, fk_ref, fv_ref, fcq_ref, fck_ref, o_ref, s_ref, m_ref, l_ref, acc_ref, tk=tk,
                side_work=decode)


def _fox_attention(layer, q_b, k_b, v_b, pieces, dq, dk_new, dv_new, cache_kt, cache_vt, cache_ft, tail,
                   page_table, tile=512):
    bsz, t, d = q_b.shape
    n = dq.shape[0]
    nh, e, page = cache_kt.shape[2:]
    n_pages = page_table.shape[1]
    npair = d // LANES
    tq = tk = _pick_tile(t, (tile, 256, 128))
    nq = t // tq
    npg = _pick_tile(n_pages, (8, 4, 2, 1))
    n_groups = n_pages // npg
    if bsz * npair * nq != n * n_groups:
        return (_fox_flash(q_b, k_b, v_b, pieces, tile),
                _fox_decode(layer, dq, dk_new, dv_new, cache_kt, cache_vt, cache_ft, tail, page_table))

    def seq_of(bi, p, qi):
        return ((bi * npair + p) * nq + qi) // n_groups

    def grp_of(bi, p, qi):
        return ((bi * npair + p) * nq + qi) % n_groups

    def kv_spec(j):
        return pl.BlockSpec((1, 1, nh, e, page),
                            lambda bi, p, qi, pt: (layer, pt[seq_of(bi, p, qi), grp_of(bi, p, qi) * npg + j], 0, 0, 0))

    def f_spec(j):
        return pl.BlockSpec((1, 1, nh, page),
                            lambda bi, p, qi, pt: (layer, pt[seq_of(bi, p, qi), grp_of(bi, p, qi) * npg + j], 0, 0))

    q_map = lambda bi, p, qi, pt: (bi, qi, p)
    seq_map = lambda bi, p, qi, pt: (bi, 0, p)
    row_spec = pl.BlockSpec((1, 1, d), lambda bi, p, qi, pt: (seq_of(bi, p, qi), 0, 0))
    grid_spec = pltpu.PrefetchScalarGridSpec(
        num_scalar_prefetch=1,
        grid=(bsz, npair, nq),
        in_specs=[pl.BlockSpec((1, tq, LANES), q_map),
                  pl.BlockSpec((1, t, LANES), seq_map),
                  pl.BlockSpec((1, t, LANES), seq_map),
                  pl.BlockSpec((1, tq, 3 * nh), lambda bi, p, qi, pt: (bi, qi, 0)),
                  pl.BlockSpec((1, t, 3 * nh), lambda bi, p, qi, pt: (bi, 0, 0)),
                  row_spec]
                 + [kv_spec(j) for j in range(npg)] * 2 + [f_spec(j) for j in range(npg)]
                 + [pl.BlockSpec((1, npg, nh), lambda bi, p, qi, pt: (seq_of(bi, p, qi), grp_of(bi, p, qi), 0)),
                    row_spec, row_spec],
        out_specs=[pl.BlockSpec((1, tq, LANES), q_map), row_spec],
        scratch_shapes=[pltpu.VMEM((2, 2, tq, tk), F32),
                        pltpu.VMEM((2, tq, LANES), F32), pltpu.VMEM((2, tq, LANES), F32),
                        pltpu.VMEM((tq, LANES), F32),
                        pltpu.VMEM((nh, 1), F32), pltpu.VMEM((nh, 1), F32), pltpu.VMEM((nh, d), F32)],
    )
    o, do = pl.pallas_call(
        functools.partial(_fox_attn_fused_kernel, npg=npg, tk=tk, n_groups=n_groups),
        grid_spec=grid_spec,
        out_shape=[jax.ShapeDtypeStruct((bsz, t, d), F32), jax.ShapeDtypeStruct((n, 1, d), F32)],
        compiler_params=pltpu.CompilerParams(
            dimension_semantics=("arbitrary", "arbitrary", "arbitrary"), vmem_limit_bytes=VMEM_LIMIT),
        name="fox_attention",
    )(page_table, q_b, k_b, v_b, pieces, pieces, dq.reshape(n, 1, d),
      *([cache_kt] * npg), *([cache_vt] * npg), *([cache_ft] * npg), tail,
      dk_new.reshape(n, 1, d), dv_new.reshape(n, 1, d))
    return o, do.reshape(n, d)


def _rmsnorm(x, g):
    xf = x.astype(F32)
    y = xf * lax.rsqrt(jnp.mean(xf * xf, axis=-1, keepdims=True) + RMS_EPS)
    return (y * g.astype(F32)).astype(x.dtype)


def _rwkv_proj_kernel(x_ref, hp_ref, g_ref, mu_ref, w_ref, w0_ref, w1_ref, w2_ref, a0_ref, a1_ref, a2_ref,
                      kk_ref, ka_ref, seg_ref, segt_ref,
                      r_ref, lw_ref, kf_ref, v_ref, nkk_ref, kka_ref, gate_ref, carry_ref,
                      *, tiles_per_seq):
    x = x_ref[...]
    tm = x.shape[0]
    g = g_ref[...]
    h = x * lax.rsqrt(jnp.mean(x * x, axis=-1, keepdims=True) + RMS_EPS) * g
    if tiles_per_seq == 0:
        h_prev = hp_ref[...]
    else:
        first = pl.program_id(0) % tiles_per_seq == 0
        before = jnp.where(first, hp_ref[0], carry_ref[...])
        row = lax.broadcasted_iota(jnp.int32, x.shape, 0)
        h_prev = jnp.where(row == 0, before, pltpu.roll(h, 1, 0))
        carry_ref[...] = h[tm - 1:tm, :]
    xx = h_prev - h
    mix = lambda s: (h + xx * mu_ref[s:s + 1, :]).astype(BF16)
    k = _nn(mix(1), w_ref[1])
    w_mid = _nn(mix(4), w1_ref[...])
    a_mid = _nn(mix(5), a1_ref[...])
    r_ref[...] = _nn(mix(0), w_ref[0])
    kk = k * kk_ref[...]
    kk_sq = _seg_sum(kk * kk, seg_ref[...])
    z = w0_ref[...] + _nn(jnp.tanh(w_mid).astype(BF16), w2_ref[...])
    a_pre = _nn(a_mid.astype(BF16), a2_ref[...])
    v_ref[...] = _nn(mix(2), w_ref[2])
    kk = kk * _seg_bcast(1.0 / jnp.maximum(jnp.sqrt(kk_sq), 1e-12), segt_ref[...])
    gate_ref[...] = _nn(mix(3), w_ref[3])
    lw_ref[...] = -jnp.exp(jnp.minimum(z, 0.0) - jnp.log(1.0 + jnp.exp(-jnp.abs(z))) - 0.5)
    a = jax.nn.sigmoid(a0_ref[...] + a_pre)
    kf_ref[...] = k * (1.0 + (a - 1.0) * ka_ref[...])
    nkk_ref[...] = -kk
    kka_ref[...] = kk * a


def _rwkv_out_kernel(o_ref, r_ref, kf_ref, v_ref, gate_ref, x_ref, lw_ref, lb_ref, rk_ref, w_ref,
                     seg_ref, segt_ref, y_ref):
    seg = seg_ref[...]
    seg_t = segt_ref[...]
    tm, d = o_ref.shape
    inv_e = 1.0 / (d // seg.shape[1])
    n_groups = 2 if tm % 16 == 0 else 1
    rows = [slice(i * (tm // n_groups), (i + 1) * (tm // n_groups)) for i in range(n_groups)]
    o = [o_ref[rs, :] for rs in rows]
    o_sum = [_seg_sum(x, seg) for x in o]
    rk_sum = [_seg_sum(r_ref[rs, :] * kf_ref[rs, :] * rk_ref[...], seg) for rs in rows]
    cen = [x - _seg_bcast(s * inv_e, seg_t) for x, s in zip(o, o_sum)]
    var = [_seg_sum(c * c, seg) * inv_e for c in cen]
    bonus = [_seg_bcast(s, seg_t) * v_ref[rs, :] for s, rs in zip(rk_sum, rows)]
    gn = [c * _seg_bcast(lax.rsqrt(vr + LNX_EPS), seg_t) * lw_ref[...] + lb_ref[...] for c, vr in zip(cen, var)]
    for rs, gn_i, bonus_i in zip(rows, gn, bonus):
        g = gate_ref[rs, :]
        y = ((gn_i + bonus_i) * (g * jax.nn.sigmoid(g))).astype(BF16)
        y_ref[rs, :] = x_ref[rs, :] + _nn(y, w_ref[...])


def _rwkv_layer(x, h_prev0, states, layer, prm, nh):
    (norm, mu, w_in, w0, w1, w2, a0, a1, a2, k_k, k_a, r_k, lnx_w, lnx_b, w_out) = prm
    bsz, t, d = x.shape
    e = d // nh
    m = bsz * t
    seg, seg_t = _seg_mats(d, nh)
    row = lambda z: z.reshape(1, -1).astype(F32)
    full = lambda arr: pl.BlockSpec(arr.shape, lambda i: (0,) * arr.ndim)
    x2 = x.reshape(m, d)
    if t == 1:
        tm = _pick_tile(m, (256, 128))
        tiles_per_seq = 0
        hp = h_prev0.astype(F32)
        hp_spec = pl.BlockSpec((tm, d), lambda i: (i, 0))
    else:
        tm = _pick_tile(t, (256, 128))
        tiles_per_seq = t // tm
        hp = h_prev0.astype(F32).reshape(bsz, 1, d)
        hp_spec = pl.BlockSpec((1, 1, d), lambda i: (i // tiles_per_seq, 0, 0))
    tile = pl.BlockSpec((tm, d), lambda i: (i, 0))
    consts = [row(norm), mu.astype(F32), w_in.astype(BF16), row(w0), w1.astype(BF16), w2.astype(BF16),
              row(a0), a1.astype(BF16), a2.astype(BF16), row(k_k), row(k_a), seg, seg_t]
    r, lw, kf, v, neg_kk, kk_a, gate = pl.pallas_call(
        functools.partial(_rwkv_proj_kernel, tiles_per_seq=tiles_per_seq),
        grid=(m // tm,),
        in_specs=[tile, hp_spec] + [full(c) for c in consts],
        out_specs=[tile] * 7,
        out_shape=[jax.ShapeDtypeStruct((m, d), F32)] * 7,
        scratch_shapes=[pltpu.VMEM((1, d), F32)],
        compiler_params=pltpu.CompilerParams(
            dimension_semantics=("arbitrary",), vmem_limit_bytes=VMEM_LIMIT),
        name="rwkv_proj",
    )(x2, hp, *consts)

    if states is None:
        seq = lambda z: z.reshape(bsz, t, d)
        o, zt = _wkv_chunked(seq(r), seq(lw), seq(kf), seq(v), seq(neg_kk), seq(kk_a))
        zt = zt.reshape(bsz, d // LANES, 2, e, 2, e)
        s_fin = jnp.stack([zt[:, :, 0, :, 0, :], zt[:, :, 1, :, 1, :]], axis=2).reshape(bsz, nh, e, e)
    else:
        o, s_fin = _wkv_step(states, layer, r, lw, kf, v, neg_kk, kk_a)

    tmo = _pick_tile(m, (256, 128))
    tile_o = pl.BlockSpec((tmo, d), lambda i: (i, 0))
    consts_o = [row(lnx_w), row(lnx_b), row(r_k), w_out.astype(BF16), seg, seg_t]
    x_new = pl.pallas_call(
        _rwkv_out_kernel,
        grid=(m // tmo,),
        in_specs=[tile_o] * 6 + [full(c) for c in consts_o],
        out_specs=tile_o,
        out_shape=jax.ShapeDtypeStruct((m, d), F32),
        compiler_params=pltpu.CompilerParams(
            dimension_semantics=("parallel",), vmem_limit_bytes=VMEM_LIMIT),
        name="rwkv_out",
    )(o.reshape(m, d), r, kf, v, gate, x2, *consts_o)
    h_last = _rmsnorm(x[:, -1, :], norm)
    return x_new.reshape(bsz, t, d), s_fin, h_last


def _split2(x):
    hi = x.astype(BF16)
    return hi, (x - hi.astype(F32)).astype(BF16)


def _seg_sum(x, seg):
    hi, lo = _split2(x)
    return _nn(hi, seg) + _nn(lo, seg)


def _seg_bcast(y, seg_t):
    hi, lo = _split2(y)
    return _nn(hi, seg_t) + _nn(lo, seg_t)


def _seg_mats(d, nh):
    lane_head = jnp.arange(d, dtype=jnp.int32) // (d // nh)
    seg = (lane_head[:, None] == jnp.arange(nh, dtype=jnp.int32)[None, :]).astype(BF16)
    return seg, seg.T


def _fox_proj_kernel(x_ref, g_ref, w_ref, wf_ref, bf_ref, qn_ref, kn_ref, seg_ref, segt_ref, *refs,
                     q_scale, tiles_per_seq, n_aliased):
    q_ref, k_ref, v_ref, gate_ref, lf_ref = refs[n_aliased:n_aliased + 5]
    attn_refs = refs[n_aliased + 5:]
    x = x_ref[...]
    tm = x.shape[0]
    nh = lf_ref.shape[1]
    d = x.shape[1]
    hb = (x * lax.rsqrt(jnp.mean(x * x, axis=-1, keepdims=True) + RMS_EPS) * g_ref[...]).astype(BF16)
    seg = seg_ref[...]
    seg_t = segt_ref[...]
    inv_e = 1.0 / (d // seg.shape[1])

    q = _nn(hb, w_ref[:, 0:d])
    k = _nn(hb, w_ref[:, d:2 * d])
    q_ms = _seg_sum(q * q, seg)
    v = _nn(hb, w_ref[:, 2 * d:3 * d])
    k_ms = _seg_sum(k * k, seg)
    q = q * _seg_bcast(lax.rsqrt(q_ms * inv_e + RMS_EPS), seg_t) * qn_ref[...]
    gate_ref[...] = _nn(hb, w_ref[:, 3 * d:4 * d])
    k = k * _seg_bcast(lax.rsqrt(k_ms * inv_e + RMS_EPS), seg_t) * kn_ref[...]
    k_ref[...] = k.reshape(k_ref.shape)
    v_ref[...] = v.reshape(v_ref.shape)
    f = _nn(hb, wf_ref[...]) + bf_ref[...]
    logf = jnp.minimum(f, 0.0) - jnp.log(1.0 + jnp.exp(-jnp.abs(f)))
    lf_ref[...] = logf[:, :nh]
    if not attn_refs:
        q_ref[...] = q
        return
    kb_ref, vb_ref, pieces_ref, carry_ref = attn_refs
    q_ref[...] = (q * q_scale).astype(BF16)
    kb_ref[...] = k.astype(BF16)
    vb_ref[...] = v.astype(BF16)
    tri = (lax.broadcasted_iota(jnp.int32, (tm, tm), 0)
           >= lax.broadcasted_iota(jnp.int32, (tm, tm), 1)).astype(BF16)
    f1 = logf.astype(BF16)
    rem = logf - f1.astype(F32)
    f2 = rem.astype(BF16)
    f3 = (rem - f2.astype(F32)).astype(BF16)
    first = pl.program_id(0) % tiles_per_seq == 0
    c = _nn(tri, f1) + (_nn(tri, f2) + _nn(tri, f3)) + jnp.where(first, 0.0, carry_ref[...])
    carry_ref[...] = c[tm - 1:tm, :]
    top16 = lambda z: lax.bitcast_convert_type(
        lax.bitcast_convert_type(z, jnp.uint32) & jnp.uint32(0xFFFF0000), F32)
    c2 = c * LOG2E
    hi = top16(c2)
    mid = top16(c2 - hi)
    lane = lax.broadcasted_iota(jnp.int32, c2.shape, 1)
    pieces_ref[...] = jnp.where(lane < nh, hi, jnp.where(lane < 2 * nh, mid, top16(c2 - hi - mid))).astype(BF16)


def _fox_project(x, norm, w_in, b_f, qn_g, kn_g, nh, attn_q_scale=None, stacks=None, layer=0):
    bsz, t, d = x.shape
    m = bsz * t
    for_attn = attn_q_scale is not None
    tm = _pick_tile(t if for_attn else m, (256, 128))
    n_f = 3 * nh if for_attn else nh
    seg, seg_t = _seg_mats(d, nh)
    row = lambda z: z.reshape(1, -1).astype(F32)
    full = lambda a: pl.BlockSpec(a.shape, lambda i: (0,) * a.ndim)
    w_f = w_in[:, 4 * d:].astype(BF16)
    consts = [row(norm), w_in[:, :4 * d].astype(BF16), jnp.tile(w_f, (1, n_f // nh)), row(jnp.tile(b_f, n_f // nh)),
              row(jnp.tile(qn_g, nh)), row(jnp.tile(kn_g, nh)), seg, seg_t]
    tile = pl.BlockSpec((tm, d), lambda i: (i, 0))
    narrow = lambda w: pl.BlockSpec((tm, w), lambda i: (i, 0))
    attn_specs = [tile, tile, narrow(n_f)] if for_attn else []
    attn_shapes = ([jax.ShapeDtypeStruct((m, d), BF16)] * 2 + [jax.ShapeDtypeStruct((m, n_f), BF16)]) if for_attn else []
    kv_spec, kv_shape = tile, jax.ShapeDtypeStruct((m, d), F32)
    stack_args, stack_specs, aliases = [], [], {}
    if stacks is not None:
        kv_spec = pl.BlockSpec((1, tm, d), lambda i: (layer, i, 0))
        kv_shape = jax.ShapeDtypeStruct(stacks[0].shape, F32)
        stack_args = list(stacks)
        stack_specs = [pl.BlockSpec(memory_space=pl.ANY)] * 2
        aliases = {1 + len(consts): 1, 2 + len(consts): 2}
    outs = pl.pallas_call(
        functools.partial(_fox_proj_kernel, q_scale=attn_q_scale, tiles_per_seq=t // tm,
                          n_aliased=len(stack_args)),
        grid=(m // tm,),
        in_specs=[tile] + [full(a) for a in consts] + stack_specs,
        out_specs=[tile, kv_spec, kv_spec, tile, narrow(nh)] + attn_specs,
        out_shape=([jax.ShapeDtypeStruct((m, d), BF16 if for_attn else F32), kv_shape, kv_shape,
                    jax.ShapeDtypeStruct((m, d), F32), jax.ShapeDtypeStruct((m, nh), F32)] + attn_shapes),
        scratch_shapes=[pltpu.VMEM((1, n_f), F32)] if for_attn else [],
        input_output_aliases=aliases,
        compiler_params=pltpu.CompilerParams(
            dimension_semantics=("arbitrary" if for_attn else "parallel",), vmem_limit_bytes=VMEM_LIMIT),
        name="fox_proj",
    )(x.reshape(m, d), *consts, *stack_args)
    q, k, v, gate, logf = outs[:5]
    seq = lambda z: z.reshape(bsz, t, z.shape[-1])
    if stacks is None:
        k, v = seq(k), seq(v)
    return seq(q), k, v, gate, logf.reshape(bsz, t, nh), tuple(seq(z) for z in outs[5:])


def _gate_out_kernel(o_ref, gate_ref, x_ref, w_ref, y_ref):
    g = gate_ref[...]
    y = (o_ref[...] * (g * jax.nn.sigmoid(g))).astype(BF16)
    y_ref[...] = x_ref[...] + _nn(y, w_ref[...])


def _fox_finish(x, o, gate, w_out):
    bsz, t, d = x.shape
    m = bsz * t
    tm = _pick_tile(m, (512, 256, 128))
    tile = pl.BlockSpec((tm, d), lambda i: (i, 0))
    y = pl.pallas_call(
        _gate_out_kernel,
        grid=(m // tm,),
        in_specs=[tile, tile, tile, pl.BlockSpec((d, d), lambda i: (0, 0))],
        out_specs=tile,
        out_shape=jax.ShapeDtypeStruct((m, d), F32),
        compiler_params=pltpu.CompilerParams(
            dimension_semantics=("parallel",), vmem_limit_bytes=VMEM_LIMIT),
        name="gate_out",
    )(o.reshape(m, d), gate, x.reshape(m, d), w_out.astype(BF16))
    return y.reshape(bsz, t, d)


def kernel(x_prompt, x_sample, state_wkv, state_shift, cache_k, cache_v, cache_logf, page_table,
           norm_a, mu_a, w_in_a, w0_a, w1_a, w2_a, a0_a, a1_a, a2_a, kk_a, ka_a, rk_a, lnx_w_a, lnx_b_a, w_out_a,
           norm_b, w_in_b, bf_b, qn_b, kn_b, w_out_b):
    bsz, t, d = x_prompt.shape
    nb, ts, _ = x_sample.shape
    assert ts == 1, "the sample group carries one new token per sequence"
    nh, e = rk_a.shape[1], rk_a.shape[2]
    assert d == nh * e and 2 * e == LANES and t % WKV_CHUNK == 0
    depth = norm_a.shape[0] + norm_b.shape[0]
    scale = e ** -0.5
    ckt = jnp.transpose(cache_k, (0, 1, 3, 4, 2))
    cvt = jnp.transpose(cache_v, (0, 1, 3, 4, 2))
    cft = jnp.transpose(cache_logf, (0, 1, 3, 2)).astype(F32)
    page_mass = jnp.sum(cft, axis=-1)

    xp, xs = x_prompt, x_sample
    fp_l, sp_l, hp_l = [], [], []
    ks_l, vs_l, fs_l, hs_l = [], [], [], []
    states_s = state_wkv.astype(F32)
    n_fox = norm_b.shape[0]
    kv_stacks = (jnp.zeros((n_fox, bsz * t, d), F32), jnp.zeros((n_fox, bsz * t, d), F32))
    for i in range(depth):
        j = i // 2
        if i % 2 == 0:
            prm = (norm_a[j], mu_a[j], w_in_a[j], w0_a[j], w1_a[j], w2_a[j], a0_a[j], a1_a[j], a2_a[j],
                   kk_a[j], ka_a[j], rk_a[j], lnx_w_a[j], lnx_b_a[j], w_out_a[j])
            xp, s_p, l_p = _rwkv_layer(xp, jnp.zeros((bsz, d), xp.dtype), None, j, prm, nh)
            xs, states_s, l_s = _rwkv_layer(xs, state_shift[j], states_s, j, prm, nh)
            sp_l.append(s_p); hp_l.append(l_p); hs_l.append(l_s)
        else:
            qp_b, *kv_stacks, gp, lfp, attn_ops = _fox_project(
                xp, norm_b[j], w_in_b[j], bf_b[j], qn_b[j], kn_b[j], nh,
                attn_q_scale=scale * LOG2E, stacks=kv_stacks, layer=j)
            qs, ksn, vsn, gs, lfs, _ = _fox_project(xs, norm_b[j], w_in_b[j], bf_b[j], qn_b[j], kn_b[j], nh)
            seq_mass = page_mass[j][page_table]
            later = jnp.cumsum(seq_mass[:, ::-1, :], axis=1)[:, ::-1, :] - seq_mass
            tail = later + lfs.reshape(nb, 1, nh)
            op, osm = _fox_attention(j, qp_b, *attn_ops, (qs * scale).reshape(nb, d), ksn.reshape(nb, d),
                                     vsn.reshape(nb, d), ckt, cvt, cft, tail, page_table)
            xp = _fox_finish(xp, op, gp, w_out_b[j])
            xs = _fox_finish(xs, osm.reshape(nb, 1, d), gs, w_out_b[j])
            hd4 = lambda z, n_, t_: z.reshape(n_, t_, nh, e)
            fp_l.append(lfp)
            ks_l.append(hd4(ksn, nb, ts)); vs_l.append(hd4(vsn, nb, ts)); fs_l.append(lfs)
    k_prompt, v_prompt = (z.reshape(n_fox, bsz, t, nh, e) for z in kv_stacks)
    return (xp, xs,
            k_prompt, v_prompt, jnp.stack(fp_l), jnp.stack(sp_l), jnp.stack(hp_l),
            jnp.stack(ks_l), jnp.stack(vs_l), jnp.stack(fs_l), states_s, jnp.stack(hs_l))
```

```python
import functools

import jax
import jax.numpy as jnp
from jax import lax
from jax.experimental import pallas as pl
from jax.experimental.pallas import tpu as pltpu

F32 = jnp.float32
BF16 = jnp.bfloat16

LANES = 128
RMS_EPS = 1e-6
LNX_EPS = 64e-5
NEG_BIG = -1e30
LOG2E = 1.4426950408889634
WKV_CHUNK = 64
WKV_CHUNKS_PER_STEP = (4, 2, 1)
VMEM_LIMIT = 48 * 1024 * 1024


def _nt(x, y):
    return lax.dot_general(x, y, (((1,), (1,)), ((), ())), preferred_element_type=F32)


def _tn(x, y):
    return lax.dot_general(x, y, (((0,), (0,)), ((), ())), preferred_element_type=F32)


def _nn(x, y):
    return jnp.dot(x, y, preferred_element_type=F32)


def _pick_tile(n, candidates):
    for c in candidates:
        if n % c == 0:
            return c
    return n


def _wkv_chunk_kernel(r_ref, lw_ref, k_ref, v_ref, a_ref, b_ref, o_ref, zt_ref):
    c = pl.program_id(1)

    @pl.when(c == 0)
    def _():
        zt_ref[...] = jnp.zeros_like(zt_ref)

    L = WKV_CHUNK
    n_chunks = r_ref.shape[1] // L
    n = 2 * L
    half = LANES // 2
    npair = r_ref.shape[2] // LANES

    ti = lax.broadcasted_iota(jnp.int32, (L, L), 0)
    tj = lax.broadcasted_iota(jnp.int32, (L, L), 1)
    tri = (ti >= tj).astype(BF16)
    head0 = lax.broadcasted_iota(jnp.int32, (L, LANES), 1) < half
    i = lax.broadcasted_iota(jnp.int32, (n, n), 0)
    j = lax.broadcasted_iota(jnp.int32, (n, n), 1)
    strict = i > j
    incl = i >= j
    eye = jnp.where(i == j, 1.0, 0.0)
    first = (i >> 1) == (j >> 1)
    levels = []
    lvl = 1
    while (2 << lvl) <= L:
        levels.append(((i >> (lvl + 1)) == (j >> (lvl + 1))) & ((i >> lvl) != (j >> lvl)))
        lvl += 1

    def stack(x):
        return jnp.concatenate([jnp.where(head0, x, 0.0), jnp.where(head0, 0.0, x)], axis=0)

    pairs = range(npair)
    units = range(n_chunks * npair)
    sls = [(slice((q // npair) * L, (q // npair + 1) * L), slice((q % npair) * LANES, (q % npair + 1) * LANES))
           for q in units]
    ar, bk, bk_h, vs, p_last = [], [], [], [], []
    for q in units:
        r, lw, k, v, a, b = (ref[0, sls[q][0], sls[q][1]] for ref in (r_ref, lw_ref, k_ref, v_ref, a_ref, b_ref))
        lw1 = lw.astype(BF16)
        rem = lw - lw1.astype(F32)
        lw2 = rem.astype(BF16)
        lw3 = (rem - lw2.astype(F32)).astype(BF16)
        cum = _nn(tri, lw1) + (_nn(tri, lw2) + _nn(tri, lw3))
        p_inc = jnp.exp(cum)
        p_exc = jnp.exp(cum - lw)
        p_inv = jnp.exp(-cum)
        pl_ = p_inc[L - 1:L, :]
        bk_f = jnp.concatenate([stack(b * p_inv), stack(k * p_inv)], axis=0)
        ar.append(jnp.concatenate([stack(a * p_exc), stack(r * p_inc)], axis=0).astype(BF16))
        bk.append(bk_f.astype(BF16))
        bk_h.append((bk_f * pl_).astype(BF16))
        vs.append(stack(v).astype(BF16))
        p_last.append(pl_)

    g = [_nt(ar[q], bk[q]) for q in units]
    a_ab = [jnp.where(strict, g[q][:n, :n], 0.0) for q in units]
    ak_rk = [jnp.concatenate([jnp.where(strict, g[q][:n, n:], 0.0),
                              jnp.where(incl, g[q][n:, n:], 0.0)], axis=0).astype(BF16) for q in units]
    a_rb = [jnp.where(incl, g[q][n:, :n], 0.0).astype(BF16) for q in units]

    def compress(full, size):
        return functools.reduce(lambda x, y: x + y, [full[i * size:(i + 1) * size] for i in range(n // size)])

    def expand(comp, size):
        lane_blk = lax.broadcasted_iota(jnp.int32, (size, n), 1) >> (size.bit_length() - 1)
        return jnp.concatenate([jnp.where(lane_blk == i, comp, 0.0) for i in range(n // size)], axis=0)

    sub = L // 2
    t_full = [eye + jnp.where(first, a_ab[q], 0.0) for q in units]
    t_comp = [compress(t_full[q], sub) for q in units]
    for off in levels[:-1]:
        x = [_nn(t_comp[q].astype(BF16), jnp.where(off, a_ab[q], 0.0).astype(BF16)).astype(BF16) for q in units]
        t_comp = [t_comp[q] + _nn(x[q], t_full[q].astype(BF16)) for q in units]
        t_full = [expand(t_comp[q], sub) for q in units]
    t_comp = [compress(t_full[q], L) for q in units]
    x = [_nn(t_comp[q].astype(BF16), jnp.where(levels[-1], a_ab[q], 0.0).astype(BF16)).astype(BF16) for q in units]
    t_comp = [(t_comp[q] + _nn(x[q], t_full[q].astype(BF16))).astype(BF16) for q in units]

    zt = [zt_ref[0, p] for p in pairs]
    for c in range(n_chunks):
        qs = [c * npair + p for p in pairs]
        y = [_nt(ar[q], zt[p].astype(BF16)) + _nn(ak_rk[q], vs[q]) for p, q in zip(pairs, qs)]
        u_b = [stack(_nn(t_comp[q], y[p][:n].astype(BF16))).astype(BF16) for p, q in zip(pairs, qs)]
        for p, q in zip(pairs, qs):
            o_st = y[p][n:] + _nn(a_rb[q], u_b[p])
            o_ref[0, sls[q][0], sls[q][1]] = o_st[:L] + o_st[L:]
            zt[p] = zt[p] * p_last[q] + _tn(jnp.concatenate([u_b[p], vs[q]], axis=0), bk_h[q])
    for p in pairs:
        zt_ref[0, p] = zt[p]


def _wkv_chunked(r, lw, k, v, a, b):
    bsz, t, d = r.shape
    npair = d // LANES
    tb = _pick_tile(t, tuple(WKV_CHUNK * c for c in WKV_CHUNKS_PER_STEP))
    spec = pl.BlockSpec((1, tb, d), lambda bi, c: (bi, c, 0))
    return pl.pallas_call(
        _wkv_chunk_kernel,
        grid=(bsz, t // tb),
        in_specs=[spec] * 6,
        out_specs=[spec, pl.BlockSpec((1, npair, LANES, LANES), lambda bi, c: (bi, 0, 0, 0))],
        out_shape=[jax.ShapeDtypeStruct((bsz, t, d), F32),
                   jax.ShapeDtypeStruct((bsz, npair, LANES, LANES), F32)],
        compiler_params=pltpu.CompilerParams(
            dimension_semantics=("parallel", "arbitrary"), vmem_limit_bytes=VMEM_LIMIT),
        name="wkv_chunked",
    )(r, lw, k, v, a, b)


def _wkv_step_kernel(s_ref, r_ref, lw_ref, k_ref, v_ref, a_ref, b_ref, so_ref, o_ref):
    _, bb, nh, e, _ = s_ref.shape
    half = LANES // 2
    ii = lax.broadcasted_iota(jnp.int32, (e, LANES), 0)
    jj = lax.broadcasted_iota(jnp.int32, (e, LANES), 1)
    eye_lo = ii == jj
    eye_hi = ii + half == jj

    for i in range(bb):
        units = []
        for h in range(nh):
            p, par = divmod(h, 2)
            sl = slice(p * LANES, (p + 1) * LANES)
            r, lw, k, v, a, b = (ref[i:i + 1, sl] for ref in (r_ref, lw_ref, k_ref, v_ref, a_ref, b_ref))
            w = jnp.exp(lw)
            if par == 0:
                rh, wh, kh, ah, bh = (z[:, :e] for z in (r, w, k, a, b))
            else:
                rh, wh, kh, ah, bh = (pltpu.roll(z, half, 1)[:, :e] for z in (r, w, k, a, b))
            s = s_ref[0, i, h]
            v_col = jnp.sum(jnp.where(eye_hi if par else eye_lo, v, 0.0), axis=1, keepdims=True)
            sa = jnp.sum(s * ah, axis=1, keepdims=True)
            units.append((s, rh, wh, kh, bh, v_col, sa))
        o_cols = []
        for h, (s, rh, wh, kh, bh, v_col, sa) in enumerate(units):
            s_new = s * wh + sa * bh + v_col * kh
            so_ref[0, i, h] = s_new
            o_cols.append(jnp.sum(s_new * rh, axis=1, keepdims=True))
        for p in range(nh // 2):
            o_ref[i:i + 1, p * LANES:(p + 1) * LANES] = (
                jnp.sum(jnp.where(eye_lo, o_cols[2 * p], 0.0), axis=0, keepdims=True)
                + jnp.sum(jnp.where(eye_hi, o_cols[2 * p + 1], 0.0), axis=0, keepdims=True))


def _wkv_step(states, layer, r, lw, k, v, a, b):
    _, n, h, e, _ = states.shape
    d = h * e
    bb = _pick_tile(n, (8,))
    s_spec = pl.BlockSpec((1, bb, h, e, e), lambda i: (layer, i, 0, 0, 0))
    row_spec = pl.BlockSpec((bb, d), lambda i: (i, 0))
    states, o = pl.pallas_call(
        _wkv_step_kernel,
        grid=(n // bb,),
        in_specs=[s_spec] + [row_spec] * 6,
        out_specs=[s_spec, row_spec],
        out_shape=[jax.ShapeDtypeStruct(states.shape, F32), jax.ShapeDtypeStruct((n, d), F32)],
        input_output_aliases={0: 0},
        compiler_params=pltpu.CompilerParams(
            dimension_semantics=("parallel",), vmem_limit_bytes=VMEM_LIMIT),
        name="wkv_step",
    )(states, r, lw, k, v, a, b)
    return o, states


def _fox_flash_kernel(*refs, tk):
    _flash_body(pl.program_id(1), pl.program_id(2), *refs, tk=tk)


def _flash_body(p, qi, q_ref, k_ref, v_ref, cq_ref, ck_ref, o_ref, s_ref, m_ref, l_ref, acc_ref, *, tk,
                side_work=()):
    tq = q_ref.shape[1]
    half = LANES // 2
    n_piece_rows = cq_ref.shape[2]
    nh = n_piece_rows // 3
    heads = range(2)

    q_head0 = lax.broadcasted_iota(jnp.int32, (tq, LANES), 1) < half
    k_head0 = lax.broadcasted_iota(jnp.int32, (tk, LANES), 1) < half

    prow = lax.broadcasted_iota(jnp.int32, (n_piece_rows, LANES), 0)
    plane = lax.broadcasted_iota(jnp.int32, (n_piece_rows, LANES), 1)
    piece, head = prow >> (nh.bit_length() - 1), prow & (nh - 1)
    base = jnp.where(head == 2 * p, half, jnp.where(head == 2 * p + 1, 0, -LANES))
    place_q = jnp.where(plane == base + piece, 1.0, 0.0).astype(BF16)
    place_k = jnp.where(plane == base + piece + 3, -1.0, 0.0).astype(BF16)
    slot = lax.broadcasted_iota(jnp.int32, (1, LANES), 1) & (half - 1)
    ones_q = jnp.where((slot >= 3) & (slot < 6), 1.0, 0.0)
    ones_k = jnp.where(slot < 3, 1.0, 0.0)

    q = q_ref[0]
    q_aug = (_nn(cq_ref[0], place_q) + ones_q).astype(BF16)
    q_ops = (jnp.where(q_head0, q, q_aug), jnp.where(q_head0, q_aug, q))

    m_ref[...] = jnp.full_like(m_ref, NEG_BIG)
    l_ref[...] = jnp.zeros_like(l_ref)
    acc_ref[...] = jnp.zeros_like(acc_ref)

    def rows(kj):
        return slice(kj * tk, (kj + 1) * tk)

    def scores(kj, slot_idx):
        k = k_ref[0, rows(kj), :]
        k_aug = (_nn(ck_ref[0, rows(kj), :], place_k) + ones_k).astype(BF16)
        k_ops = (jnp.where(k_head0, k, k_aug), jnp.where(k_head0, k_aug, k))
        for h in heads:
            s_ref[slot_idx, h] = _nt(q_ops[h], k_ops[h])

    def consume(kj, slot_idx, on_diagonal):
        v = v_ref[0, rows(kj), :]
        zero = jnp.zeros_like(v)
        v_own = (jnp.where(k_head0, v, zero), jnp.where(k_head0, zero, v))
        if on_diagonal:
            keep = (lax.broadcasted_iota(jnp.int32, (tq, tk), 0)
                    >= lax.broadcasted_iota(jnp.int32, (tq, tk), 1))
        alpha, pv = [], []
        for h in heads:
            s = s_ref[slot_idx, h]
            if on_diagonal:
                s = jnp.where(keep, s, NEG_BIG)
            m_prev = m_ref[h]
            m_new = jnp.maximum(m_prev, jnp.max(s, axis=1, keepdims=True))
            a = jnp.exp2(m_prev - m_new)
            pr = jnp.exp2(s - jnp.concatenate([m_new] * (tk // LANES), axis=1))
            l_ref[h] = a * l_ref[h] + jnp.sum(pr, axis=1, keepdims=True)
            m_ref[h] = m_new
            alpha.append(a)
            pv.append(_nn(pr.astype(BF16), v_own[h]))
        acc_ref[...] = jnp.where(q_head0, alpha[0], alpha[1]) * acc_ref[...] + (pv[0] + pv[1])

    side = list(side_work) + [lambda: None] * (3 - len(side_work))

    def run(n_before):
        side[0]()
        scores(0, 0)
        for kj in range(n_before):
            scores(kj + 1, (kj + 1) % 2)
            consume(kj, kj % 2, False)
            if kj == 0:
                side[1]()
        if n_before == 0:
            side[1]()
        consume(n_before, n_before % 2, True)
        side[2]()
        o_ref[0] = acc_ref[...] / jnp.where(q_head0, l_ref[0], l_ref[1])

    for n_before in range(k_ref.shape[1] // tk):
        pl.when(qi == n_before)(functools.partial(run, n_before))


def _fox_flash(q_b, k_b, v_b, pieces, tile=512):
    bsz, t, d = q_b.shape
    nh = pieces.shape[-1] // 3
    assert nh & (nh - 1) == 0, "head count must be a power of two"
    npair = d // LANES
    tq = tk = _pick_tile(t, (tile, 256, 128))
    q_map = lambda bi, p, qi: (bi, qi, p)
    seq_map = lambda bi, p, qi: (bi, 0, p)
    return pl.pallas_call(
        functools.partial(_fox_flash_kernel, tk=tk),
        grid=(bsz, npair, t // tq),
        in_specs=[pl.BlockSpec((1, tq, LANES), q_map),
                  pl.BlockSpec((1, t, LANES), seq_map),
                  pl.BlockSpec((1, t, LANES), seq_map),
                  pl.BlockSpec((1, tq, 3 * nh), lambda bi, p, qi: (bi, qi, 0)),
                  pl.BlockSpec((1, t, 3 * nh), lambda bi, p, qi: (bi, 0, 0))],
        out_specs=pl.BlockSpec((1, tq, LANES), q_map),
        out_shape=jax.ShapeDtypeStruct((bsz, t, d), F32),
        scratch_shapes=[pltpu.VMEM((2, 2, tq, tk), F32),
                        pltpu.VMEM((2, tq, LANES), F32), pltpu.VMEM((2, tq, LANES), F32),
                        pltpu.VMEM((tq, LANES), F32)],
        compiler_params=pltpu.CompilerParams(
            dimension_semantics=("parallel", "parallel", "parallel"), vmem_limit_bytes=VMEM_LIMIT),
        name="fox_flash",
    )(q_b, k_b, v_b, pieces, pieces)


def _fox_decode_kernel(pt_ref, q_ref, *refs, npg):
    k_refs, v_refs, f_refs = refs[:npg], refs[npg:2 * npg], refs[2 * npg:3 * npg]
    for phase in _decode_phases(pl.program_id(1), q_ref, k_refs, v_refs, f_refs, *refs[3 * npg:]):
        phase()


def _decode_phases(pg, q_ref, k_refs, v_refs, f_refs, tail_ref, kn_ref, vn_ref, o_ref, m_ref, l_ref, acc_ref):
    npg = len(k_refs)

    @pl.when(pg == 0)
    def _():
        m_ref[...] = jnp.full_like(m_ref, NEG_BIG)
        l_ref[...] = jnp.zeros_like(l_ref)
        acc_ref[...] = jnp.zeros_like(acc_ref)

    nh, e, page = k_refs[0].shape[2:]
    d = nh * e
    row = lax.broadcasted_iota(jnp.int32, (nh, d), 0)
    lane = lax.broadcasted_iota(jnp.int32, (nh, d), 1)
    own = (lane >= row * e) & (lane < (row + 1) * e)
    q_bd = jnp.where(own, q_ref[0], 0.0)
    q_b = q_bd.astype(BF16)
    vals = {}

    pos = lax.broadcasted_iota(jnp.int32, (nh, page), 1)
    eye_h = lax.broadcasted_iota(jnp.int32, (nh, nh), 0) == lax.broadcasted_iota(jnp.int32, (nh, nh), 1)

    def bias(t):
        lf = f_refs[t][0, 0]
        suffix = lf
        shift = 1
        while shift < page:
            suffix = suffix + jnp.where(pos < page - shift, pltpu.roll(suffix, page - shift, 1), 0.0)
            shift *= 2
        tail_col = jnp.sum(jnp.where(eye_h, tail_ref[0, t:t + 1, :], 0.0), axis=1, keepdims=True)
        return (suffix - lf) + tail_col

    groups = [list(range(t, min(t + 2, npg))) for t in range(0, npg, 2)]

    def side_by_side(refs, g):
        return jnp.concatenate([refs[t][0, 0].reshape(d, page).astype(BF16) for t in g], axis=1)

    def scores():
        vals["s"] = [_nn(q_b, side_by_side(k_refs, g)) + jnp.concatenate([bias(t) for t in g], axis=1)
                     for g in groups]

    def combine():
        s = vals["s"]
        m_prev = m_ref[...]
        m_new = m_prev
        for s_g in s:
            m_new = jnp.maximum(m_new, jnp.max(s_g, axis=1, keepdims=True))
        alpha = jnp.exp(m_prev - m_new)
        l_new = alpha * l_ref[...]
        acc = alpha * acc_ref[...]
        for s_g, g in zip(s, groups):
            pr = jnp.exp(s_g - m_new)
            l_new = l_new + jnp.sum(pr, axis=1, keepdims=True)
            acc = acc + _nt(pr.astype(BF16), side_by_side(v_refs, g))
        l_ref[...] = l_new
        acc_ref[...] = acc
        m_ref[...] = m_new
        vals.update(m=m_new, l=l_new, acc=acc)

    def finish():
        m_new, l_new, acc = vals["m"], vals["l"], vals["acc"]
        s_new = jnp.sum(q_bd * kn_ref[0], axis=1, keepdims=True)
        m_fin = jnp.maximum(m_new, s_new)
        a_old = jnp.exp(m_new - m_fin)
        p_new = jnp.exp(s_new - m_fin)
        l_fin = l_new * a_old + p_new
        full = (acc * a_old + p_new * vn_ref[0]) / l_fin
        o_ref[0] = jnp.sum(jnp.where(own, full, 0.0), axis=0, keepdims=True)

    return scores, combine, finish


def _fox_decode(layer, q, k_new, v_new, cache_kt, cache_vt, cache_ft, tail, page_table):
    n, d = q.shape
    n_pages = page_table.shape[1]
    nh, e, page = cache_kt.shape[2:]
    npg = _pick_tile(n_pages, (8, 4, 2, 1))

    def kv_spec(t):
        return pl.BlockSpec((1, 1, nh, e, page), lambda b, pg, pt: (layer, pt[b, pg * npg + t], 0, 0, 0))

    def f_spec(t):
        return pl.BlockSpec((1, 1, nh, page), lambda b, pg, pt: (layer, pt[b, pg * npg + t], 0, 0))

    row_spec = pl.BlockSpec((1, 1, d), lambda b, pg, pt: (b, 0, 0))
    grid_spec = pltpu.PrefetchScalarGridSpec(
        num_scalar_prefetch=1,
        grid=(n, n_pages // npg),
        in_specs=[row_spec] + [kv_spec(t) for t in range(npg)] * 2 + [f_spec(t) for t in range(npg)]
                 + [pl.BlockSpec((1, npg, nh), lambda b, pg, pt: (b, pg, 0)), row_spec, row_spec],
        out_specs=row_spec,
        scratch_shapes=[pltpu.VMEM((nh, 1), F32), pltpu.VMEM((nh, 1), F32), pltpu.VMEM((nh, d), F32)],
    )
    o = pl.pallas_call(
        functools.partial(_fox_decode_kernel, npg=npg),
        grid_spec=grid_spec,
        out_shape=jax.ShapeDtypeStruct((n, 1, d), F32),
        compiler_params=pltpu.CompilerParams(
            dimension_semantics=("parallel", "arbitrary"), vmem_limit_bytes=VMEM_LIMIT),
        name="fox_decode",
    )(page_table, q.reshape(n, 1, d), *([cache_kt] * npg), *([cache_vt] * npg), *([cache_ft] * npg),
      tail, k_new.reshape(n, 1, d), v_new.reshape(n, 1, d))
    return o.reshape(n, d)


def _fox_attn_fused_kernel(pt_ref, fq_ref, fk_ref, fv_ref, fcq_ref, fck_ref, dq_ref, *refs, npg, tk, n_groups):
    k_refs, v_refs, f_refs = refs[:npg], refs[npg:2 * npg], refs[2 * npg:3 * npg]
    (tail_ref, kn_ref, vn_ref, o_ref, do_ref,
     s_ref, m_ref, l_ref, acc_ref, dm_ref, dl_ref, dacc_ref) = refs[3 * npg:]
    p, qi = pl.program_id(1), pl.program_id(2)
    step = (pl.program_id(0) * pl.num_programs(1) + p) * pl.num_programs(2) + qi
    decode = _decode_phases(step % n_groups, dq_ref, k_refs, v_refs, f_refs, tail_ref, kn_ref, vn_ref,
                            do_ref, dm_ref, dl_ref, dacc_ref)
    _flash_body(p, qi, fq_ref, fk_ref, fv_ref, fcq_ref, fck_ref, o_ref, s_ref, m_ref, l_ref, acc_ref, tk=tk,
                side_work=decode)


def _fox_attention(layer, q_b, k_b, v_b, pieces, dq, dk_new, dv_new, cache_kt, cache_vt, cache_ft, tail,
                   page_table, tile=512):
    bsz, t, d = q_b.shape
    n = dq.shape[0]
    nh, e, page = cache_kt.shape[2:]
    n_pages = page_table.shape[1]
    npair = d // LANES
    tq = tk = _pick_tile(t, (tile, 256, 128))
    nq = t // tq
    npg = _pick_tile(n_pages, (8, 4, 2, 1))
    n_groups = n_pages // npg
    if bsz * npair * nq != n * n_groups:
        return (_fox_flash(q_b, k_b, v_b, pieces, tile),
                _fox_decode(layer, dq, dk_new, dv_new, cache_kt, cache_vt, cache_ft, tail, page_table))

    def seq_of(bi, p, qi):
        return ((bi * npair + p) * nq + qi) // n_groups

    def grp_of(bi, p, qi):
        return ((bi * npair + p) * nq + qi) % n_groups

    def kv_spec(j):
        return pl.BlockSpec((1, 1, nh, e, page),
                            lambda bi, p, qi, pt: (layer, pt[seq_of(bi, p, qi), grp_of(bi, p, qi) * npg + j], 0, 0, 0))

    def f_spec(j):
        return pl.BlockSpec((1, 1, nh, page),
                            lambda bi, p, qi, pt: (layer, pt[seq_of(bi, p, qi), grp_of(bi, p, qi) * npg + j], 0, 0))

    q_map = lambda bi, p, qi, pt: (bi, qi, p)
    seq_map = lambda bi, p, qi, pt: (bi, 0, p)
    row_spec = pl.BlockSpec((1, 1, d), lambda bi, p, qi, pt: (seq_of(bi, p, qi), 0, 0))
    grid_spec = pltpu.PrefetchScalarGridSpec(
        num_scalar_prefetch=1,
        grid=(bsz, npair, nq),
        in_specs=[pl.BlockSpec((1, tq, LANES), q_map),
                  pl.BlockSpec((1, t, LANES), seq_map),
                  pl.BlockSpec((1, t, LANES), seq_map),
                  pl.BlockSpec((1, tq, 3 * nh), lambda bi, p, qi, pt: (bi, qi, 0)),
                  pl.BlockSpec((1, t, 3 * nh), lambda bi, p, qi, pt: (bi, 0, 0)),
                  row_spec]
                 + [kv_spec(j) for j in range(npg)] * 2 + [f_spec(j) for j in range(npg)]
                 + [pl.BlockSpec((1, npg, nh), lambda bi, p, qi, pt: (seq_of(bi, p, qi), grp_of(bi, p, qi), 0)),
                    row_spec, row_spec],
        out_specs=[pl.BlockSpec((1, tq, LANES), q_map), row_spec],
        scratch_shapes=[pltpu.VMEM((2, 2, tq, tk), F32),
                        pltpu.VMEM((2, tq, LANES), F32), pltpu.VMEM((2, tq, LANES), F32),
                        pltpu.VMEM((tq, LANES), F32),
                        pltpu.VMEM((nh, 1), F32), pltpu.VMEM((nh, 1), F32), pltpu.VMEM((nh, d), F32)],
    )
    o, do = pl.pallas_call(
        functools.partial(_fox_attn_fused_kernel, npg=npg, tk=tk, n_groups=n_groups),
        grid_spec=grid_spec,
        out_shape=[jax.ShapeDtypeStruct((bsz, t, d), F32), jax.ShapeDtypeStruct((n, 1, d), F32)],
        compiler_params=pltpu.CompilerParams(
            dimension_semantics=("arbitrary", "arbitrary", "arbitrary"), vmem_limit_bytes=VMEM_LIMIT),
        name="fox_attention",
    )(page_table, q_b, k_b, v_b, pieces, pieces, dq.reshape(n, 1, d),
      *([cache_kt] * npg), *([cache_vt] * npg), *([cache_ft] * npg), tail,
      dk_new.reshape(n, 1, d), dv_new.reshape(n, 1, d))
    return o, do.reshape(n, d)


def _rmsnorm(x, g):
    xf = x.astype(F32)
    y = xf * lax.rsqrt(jnp.mean(xf * xf, axis=-1, keepdims=True) + RMS_EPS)
    return (y * g.astype(F32)).astype(x.dtype)


def _rwkv_proj_kernel(x_ref, hp_ref, g_ref, mu_ref, w_ref, w0_ref, w1_ref, w2_ref, a0_ref, a1_ref, a2_ref,
                      kk_ref, ka_ref, seg_ref, segt_ref,
                      r_ref, lw_ref, kf_ref, v_ref, nkk_ref, kka_ref, gate_ref, carry_ref,
                      *, tiles_per_seq):
    x = x_ref[...]
    tm = x.shape[0]
    g = g_ref[...]
    h = x * lax.rsqrt(jnp.mean(x * x, axis=-1, keepdims=True) + RMS_EPS) * g
    if tiles_per_seq == 0:
        h_prev = hp_ref[...]
    else:
        first = pl.program_id(0) % tiles_per_seq == 0
        before = jnp.where(first, hp_ref[0], carry_ref[...])
        row = lax.broadcasted_iota(jnp.int32, x.shape, 0)
        h_prev = jnp.where(row == 0, before, pltpu.roll(h, 1, 0))
        carry_ref[...] = h[tm - 1:tm, :]
    xx = h_prev - h
    mix = lambda s: (h + xx * mu_ref[s:s + 1, :]).astype(BF16)
    k = _nn(mix(1), w_ref[1])
    w_mid = _nn(mix(4), w1_ref[...])
    a_mid = _nn(mix(5), a1_ref[...])
    r_ref[...] = _nn(mix(0), w_ref[0])
    kk = k * kk_ref[...]
    kk_sq = _seg_sum(kk * kk, seg_ref[...])
    z = w0_ref[...] + _nn(jnp.tanh(w_mid).astype(BF16), w2_ref[...])
    a_pre = _nn(a_mid.astype(BF16), a2_ref[...])
    v_ref[...] = _nn(mix(2), w_ref[2])
    kk = kk * _seg_bcast(1.0 / jnp.maximum(jnp.sqrt(kk_sq), 1e-12), segt_ref[...])
    gate_ref[...] = _nn(mix(3), w_ref[3])
    lw_ref[...] = -jnp.exp(jnp.minimum(z, 0.0) - jnp.log(1.0 + jnp.exp(-jnp.abs(z))) - 0.5)
    a = jax.nn.sigmoid(a0_ref[...] + a_pre)
    kf_ref[...] = k * (1.0 + (a - 1.0) * ka_ref[...])
    nkk_ref[...] = -kk
    kka_ref[...] = kk * a


def _rwkv_out_kernel(o_ref, r_ref, kf_ref, v_ref, gate_ref, x_ref, lw_ref, lb_ref, rk_ref, w_ref,
                     seg_ref, segt_ref, y_ref):
    seg = seg_ref[...]
    seg_t = segt_ref[...]
    tm, d = o_ref.shape
    inv_e = 1.0 / (d // seg.shape[1])
    n_groups = 2 if tm % 16 == 0 else 1
    rows = [slice(i * (tm // n_groups), (i + 1) * (tm // n_groups)) for i in range(n_groups)]
    o = [o_ref[rs, :] for rs in rows]
    o_sum = [_seg_sum(x, seg) for x in o]
    rk_sum = [_seg_sum(r_ref[rs, :] * kf_ref[rs, :] * rk_ref[...], seg) for rs in rows]
    cen = [x - _seg_bcast(s * inv_e, seg_t) for x, s in zip(o, o_sum)]
    var = [_seg_sum(c * c, seg) * inv_e for c in cen]
    bonus = [_seg_bcast(s, seg_t) * v_ref[rs, :] for s, rs in zip(rk_sum, rows)]
    gn = [c * _seg_bcast(lax.rsqrt(vr + LNX_EPS), seg_t) * lw_ref[...] + lb_ref[...] for c, vr in zip(cen, var)]
    for rs, gn_i, bonus_i in zip(rows, gn, bonus):
        g = gate_ref[rs, :]
        y = ((gn_i + bonus_i) * (g * jax.nn.sigmoid(g))).astype(BF16)
        y_ref[rs, :] = x_ref[rs, :] + _nn(y, w_ref[...])


def _rwkv_layer(x, h_prev0, states, layer, prm, nh):
    (norm, mu, w_in, w0, w1, w2, a0, a1, a2, k_k, k_a, r_k, lnx_w, lnx_b, w_out) = prm
    bsz, t, d = x.shape
    e = d // nh
    m = bsz * t
    seg, seg_t = _seg_mats(d, nh)
    row = lambda z: z.reshape(1, -1).astype(F32)
    full = lambda arr: pl.BlockSpec(arr.shape, lambda i: (0,) * arr.ndim)
    x2 = x.reshape(m, d)
    if t == 1:
        tm = _pick_tile(m, (256, 128))
        tiles_per_seq = 0
        hp = h_prev0.astype(F32)
        hp_spec = pl.BlockSpec((tm, d), lambda i: (i, 0))
    else:
        tm = _pick_tile(t, (256, 128))
        tiles_per_seq = t // tm
        hp = h_prev0.astype(F32).reshape(bsz, 1, d)
        hp_spec = pl.BlockSpec((1, 1, d), lambda i: (i // tiles_per_seq, 0, 0))
    tile = pl.BlockSpec((tm, d), lambda i: (i, 0))
    consts = [row(norm), mu.astype(F32), w_in.astype(BF16), row(w0), w1.astype(BF16), w2.astype(BF16),
              row(a0), a1.astype(BF16), a2.astype(BF16), row(k_k), row(k_a), seg, seg_t]
    r, lw, kf, v, neg_kk, kk_a, gate = pl.pallas_call(
        functools.partial(_rwkv_proj_kernel, tiles_per_seq=tiles_per_seq),
        grid=(m // tm,),
        in_specs=[tile, hp_spec] + [full(c) for c in consts],
        out_specs=[tile] * 7,
        out_shape=[jax.ShapeDtypeStruct((m, d), F32)] * 7,
        scratch_shapes=[pltpu.VMEM((1, d), F32)],
        compiler_params=pltpu.CompilerParams(
            dimension_semantics=("arbitrary",), vmem_limit_bytes=VMEM_LIMIT),
        name="rwkv_proj",
    )(x2, hp, *consts)

    if states is None:
        seq = lambda z: z.reshape(bsz, t, d)
        o, zt = _wkv_chunked(seq(r), seq(lw), seq(kf), seq(v), seq(neg_kk), seq(kk_a))
        zt = zt.reshape(bsz, d // LANES, 2, e, 2, e)
        s_fin = jnp.stack([zt[:, :, 0, :, 0, :], zt[:, :, 1, :, 1, :]], axis=2).reshape(bsz, nh, e, e)
    else:
        o, s_fin = _wkv_step(states, layer, r, lw, kf, v, neg_kk, kk_a)

    tmo = _pick_tile(m, (256, 128))
    tile_o = pl.BlockSpec((tmo, d), lambda i: (i, 0))
    consts_o = [row(lnx_w), row(lnx_b), row(r_k), w_out.astype(BF16), seg, seg_t]
    x_new = pl.pallas_call(
        _rwkv_out_kernel,
        grid=(m // tmo,),
        in_specs=[tile_o] * 6 + [full(c) for c in consts_o],
        out_specs=tile_o,
        out_shape=jax.ShapeDtypeStruct((m, d), F32),
        compiler_params=pltpu.CompilerParams(
            dimension_semantics=("parallel",), vmem_limit_bytes=VMEM_LIMIT),
        name="rwkv_out",
    )(o.reshape(m, d), r, kf, v, gate, x2, *consts_o)
    h_last = _rmsnorm(x[:, -1, :], norm)
    return x_new.reshape(bsz, t, d), s_fin, h_last


def _split2(x):
    hi = x.astype(BF16)
    return hi, (x - hi.astype(F32)).astype(BF16)


def _seg_sum(x, seg):
    hi, lo = _split2(x)
    return _nn(hi, seg) + _nn(lo, seg)


def _seg_bcast(y, seg_t):
    hi, lo = _split2(y)
    return _nn(hi, seg_t) + _nn(lo, seg_t)


def _seg_mats(d, nh):
    lane_head = jnp.arange(d, dtype=jnp.int32) // (d // nh)
    seg = (lane_head[:, None] == jnp.arange(nh, dtype=jnp.int32)[None, :]).astype(BF16)
    return seg, seg.T


def _fox_proj_kernel(x_ref, g_ref, w_ref, wf_ref, bf_ref, qn_ref, kn_ref, seg_ref, segt_ref,
                     q_ref, k_ref, v_ref, gate_ref, lf_ref, *attn_refs, q_scale, tiles_per_seq):
    x = x_ref[...]
    tm = x.shape[0]
    nh = lf_ref.shape[1]
    d = x.shape[1]
    hb = (x * lax.rsqrt(jnp.mean(x * x, axis=-1, keepdims=True) + RMS_EPS) * g_ref[...]).astype(BF16)
    seg = seg_ref[...]
    seg_t = segt_ref[...]
    inv_e = 1.0 / (d // seg.shape[1])

    q = _nn(hb, w_ref[:, 0:d])
    k = _nn(hb, w_ref[:, d:2 * d])
    q_ms = _seg_sum(q * q, seg)
    v = _nn(hb, w_ref[:, 2 * d:3 * d])
    k_ms = _seg_sum(k * k, seg)
    q = q * _seg_bcast(lax.rsqrt(q_ms * inv_e + RMS_EPS), seg_t) * qn_ref[...]
    gate_ref[...] = _nn(hb, w_ref[:, 3 * d:4 * d])
    k = k * _seg_bcast(lax.rsqrt(k_ms * inv_e + RMS_EPS), seg_t) * kn_ref[...]
    k_ref[...] = k
    v_ref[...] = v
    f = _nn(hb, wf_ref[...]) + bf_ref[...]
    logf = jnp.minimum(f, 0.0) - jnp.log(1.0 + jnp.exp(-jnp.abs(f)))
    lf_ref[...] = logf[:, :nh]
    if not attn_refs:
        q_ref[...] = q
        return
    kb_ref, vb_ref, pieces_ref, carry_ref = attn_refs
    q_ref[...] = (q * q_scale).astype(BF16)
    kb_ref[...] = k.astype(BF16)
    vb_ref[...] = v.astype(BF16)
    tri = (lax.broadcasted_iota(jnp.int32, (tm, tm), 0)
           >= lax.broadcasted_iota(jnp.int32, (tm, tm), 1)).astype(BF16)
    f1 = logf.astype(BF16)
    rem = logf - f1.astype(F32)
    f2 = rem.astype(BF16)
    f3 = (rem - f2.astype(F32)).astype(BF16)
    first = pl.program_id(0) % tiles_per_seq == 0
    c = _nn(tri, f1) + (_nn(tri, f2) + _nn(tri, f3)) + jnp.where(first, 0.0, carry_ref[...])
    carry_ref[...] = c[tm - 1:tm, :]
    top16 = lambda z: lax.bitcast_convert_type(
        lax.bitcast_convert_type(z, jnp.uint32) & jnp.uint32(0xFFFF0000), F32)
    c2 = c * LOG2E
    hi = top16(c2)
    mid = top16(c2 - hi)
    lane = lax.broadcasted_iota(jnp.int32, c2.shape, 1)
    pieces_ref[...] = jnp.where(lane < nh, hi, jnp.where(lane < 2 * nh, mid, top16(c2 - hi - mid))).astype(BF16)


def _fox_project(x, norm, w_in, b_f, qn_g, kn_g, nh, attn_q_scale=None):
    bsz, t, d = x.shape
    m = bsz * t
    for_attn = attn_q_scale is not None
    tm = _pick_tile(t if for_attn else m, (256, 128))
    n_f = 3 * nh if for_attn else nh
    seg, seg_t = _seg_mats(d, nh)
    row = lambda z: z.reshape(1, -1).astype(F32)
    full = lambda a: pl.BlockSpec(a.shape, lambda i: (0,) * a.ndim)
    w_f = w_in[:, 4 * d:].astype(BF16)
    consts = [row(norm), w_in[:, :4 * d].astype(BF16), jnp.tile(w_f, (1, n_f // nh)), row(jnp.tile(b_f, n_f // nh)),
              row(jnp.tile(qn_g, nh)), row(jnp.tile(kn_g, nh)), seg, seg_t]
    tile = pl.BlockSpec((tm, d), lambda i: (i, 0))
    narrow = lambda w: pl.BlockSpec((tm, w), lambda i: (i, 0))
    attn_specs = [tile, tile, narrow(n_f)] if for_attn else []
    attn_shapes = ([jax.ShapeDtypeStruct((m, d), BF16)] * 2 + [jax.ShapeDtypeStruct((m, n_f), BF16)]) if for_attn else []
    outs = pl.pallas_call(
        functools.partial(_fox_proj_kernel, q_scale=attn_q_scale, tiles_per_seq=t // tm),
        grid=(m // tm,),
        in_specs=[tile] + [full(a) for a in consts],
        out_specs=[tile] * 4 + [narrow(nh)] + attn_specs,
        out_shape=([jax.ShapeDtypeStruct((m, d), BF16 if for_attn else F32)]
                   + [jax.ShapeDtypeStruct((m, d), F32)] * 3 + [jax.ShapeDtypeStruct((m, nh), F32)] + attn_shapes),
        scratch_shapes=[pltpu.VMEM((1, n_f), F32)] if for_attn else [],
        compiler_params=pltpu.CompilerParams(
            dimension_semantics=("arbitrary" if for_attn else "parallel",), vmem_limit_bytes=VMEM_LIMIT),
        name="fox_proj",
    )(x.reshape(m, d), *consts)
    q, k, v, gate, logf = outs[:5]
    seq = lambda z: z.reshape(bsz, t, z.shape[-1])
    return seq(q), seq(k), seq(v), gate, logf.reshape(bsz, t, nh), tuple(seq(z) for z in outs[5:])


def _gate_out_kernel(o_ref, gate_ref, x_ref, w_ref, y_ref):
    g = gate_ref[...]
    y = (o_ref[...] * (g * jax.nn.sigmoid(g))).astype(BF16)
    y_ref[...] = x_ref[...] + _nn(y, w_ref[...])


def _fox_finish(x, o, gate, w_out):
    bsz, t, d = x.shape
    m = bsz * t
    tm = _pick_tile(m, (512, 256, 128))
    tile = pl.BlockSpec((tm, d), lambda i: (i, 0))
    y = pl.pallas_call(
        _gate_out_kernel,
        grid=(m // tm,),
        in_specs=[tile, tile, tile, pl.BlockSpec((d, d), lambda i: (0, 0))],
        out_specs=tile,
        out_shape=jax.ShapeDtypeStruct((m, d), F32),
        compiler_params=pltpu.CompilerParams(
            dimension_semantics=("parallel",), vmem_limit_bytes=VMEM_LIMIT),
        name="gate_out",
    )(o.reshape(m, d), gate, x.reshape(m, d), w_out.astype(BF16))
    return y.reshape(bsz, t, d)


def kernel(x_prompt, x_sample, state_wkv, state_shift, cache_k, cache_v, cache_logf, page_table,
           norm_a, mu_a, w_in_a, w0_a, w1_a, w2_a, a0_a, a1_a, a2_a, kk_a, ka_a, rk_a, lnx_w_a, lnx_b_a, w_out_a,
           norm_b, w_in_b, bf_b, qn_b, kn_b, w_out_b):
    bsz, t, d = x_prompt.shape
    nb, ts, _ = x_sample.shape
    assert ts == 1, "the sample group carries one new token per sequence"
    nh, e = rk_a.shape[1], rk_a.shape[2]
    assert d == nh * e and 2 * e == LANES and t % WKV_CHUNK == 0
    depth = norm_a.shape[0] + norm_b.shape[0]
    scale = e ** -0.5
    ckt = jnp.transpose(cache_k, (0, 1, 3, 4, 2))
    cvt = jnp.transpose(cache_v, (0, 1, 3, 4, 2))
    cft = jnp.transpose(cache_logf, (0, 1, 3, 2)).astype(F32)
    page_mass = jnp.sum(cft, axis=-1)

    xp, xs = x_prompt, x_sample
    kp_l, vp_l, fp_l, sp_l, hp_l = [], [], [], [], []
    ks_l, vs_l, fs_l, hs_l = [], [], [], []
    states_s = state_wkv.astype(F32)
    for i in range(depth):
        j = i // 2
        if i % 2 == 0:
            prm = (norm_a[j], mu_a[j], w_in_a[j], w0_a[j], w1_a[j], w2_a[j], a0_a[j], a1_a[j], a2_a[j],
                   kk_a[j], ka_a[j], rk_a[j], lnx_w_a[j], lnx_b_a[j], w_out_a[j])
            xp, s_p, l_p = _rwkv_layer(xp, jnp.zeros((bsz, d), xp.dtype), None, j, prm, nh)
            xs, states_s, l_s = _rwkv_layer(xs, state_shift[j], states_s, j, prm, nh)
            sp_l.append(s_p); hp_l.append(l_p); hs_l.append(l_s)
        else:
            qp_b, kp, vp, gp, lfp, attn_ops = _fox_project(xp, norm_b[j], w_in_b[j], bf_b[j], qn_b[j], kn_b[j], nh,
                                                        attn_q_scale=scale * LOG2E)
            qs, ksn, vsn, gs, lfs, _ = _fox_project(xs, norm_b[j], w_in_b[j], bf_b[j], qn_b[j], kn_b[j], nh)
            seq_mass = page_mass[j][page_table]
            later = jnp.cumsum(seq_mass[:, ::-1, :], axis=1)[:, ::-1, :] - seq_mass
            tail = later + lfs.reshape(nb, 1, nh)
            op, osm = _fox_attention(j, qp_b, *attn_ops, (qs * scale).reshape(nb, d), ksn.reshape(nb, d),
                                     vsn.reshape(nb, d), ckt, cvt, cft, tail, page_table)
            xp = _fox_finish(xp, op, gp, w_out_b[j])
            xs = _fox_finish(xs, osm.reshape(nb, 1, d), gs, w_out_b[j])
            hd4 = lambda z, n_, t_: z.reshape(n_, t_, nh, e)
            kp_l.append(hd4(kp, bsz, t)); vp_l.append(hd4(vp, bsz, t)); fp_l.append(lfp)
            ks_l.append(hd4(ksn, nb, ts)); vs_l.append(hd4(vsn, nb, ts)); fs_l.append(lfs)
    return (xp, xs,
            jnp.stack(kp_l), jnp.stack(vp_l), jnp.stack(fp_l), jnp.stack(sp_l), jnp.stack(hp_l),
            jnp.stack(ks_l), jnp.stack(vs_l), jnp.stack(fs_l), states_s, jnp.stack(hs_l))
```

```python
import functools

import jax
import jax.numpy as jnp
from jax import lax
from jax.experimental import pallas as pl
from jax.experimental.pallas import tpu as pltpu

F32 = jnp.float32
BF16 = jnp.bfloat16

LANES = 128
RMS_EPS = 1e-6
LNX_EPS = 64e-5
NEG_BIG = -1e30
LOG2E = 1.4426950408889634
WKV_CHUNK = 64
WKV_CHUNKS_PER_STEP = (4, 2, 1)
VMEM_LIMIT = 48 * 1024 * 1024


def _nt(x, y):
    return lax.dot_general(x, y, (((1,), (1,)), ((), ())), preferred_element_type=F32)


def _tn(x, y):
    return lax.dot_general(x, y, (((0,), (0,)), ((), ())), preferred_element_type=F32)


def _nn(x, y):
    return jnp.dot(x, y, preferred_element_type=F32)


def _pick_tile(n, candidates):
    for c in candidates:
        if n % c == 0:
            return c
    return n


def _wkv_chunk_kernel(r_ref, lw_ref, k_ref, v_ref, a_ref, b_ref, o_ref, zt_ref):
    c = pl.program_id(1)

    @pl.when(c == 0)
    def _():
        zt_ref[...] = jnp.zeros_like(zt_ref)

    L = WKV_CHUNK
    n_chunks = r_ref.shape[1] // L
    n = 2 * L
    half = LANES // 2
    npair = r_ref.shape[2] // LANES

    ti = lax.broadcasted_iota(jnp.int32, (L, L), 0)
    tj = lax.broadcasted_iota(jnp.int32, (L, L), 1)
    tri = (ti >= tj).astype(BF16)
    head0 = lax.broadcasted_iota(jnp.int32, (L, LANES), 1) < half
    i = lax.broadcasted_iota(jnp.int32, (n, n), 0)
    j = lax.broadcasted_iota(jnp.int32, (n, n), 1)
    strict = i > j
    incl = i >= j
    eye = jnp.where(i == j, 1.0, 0.0)
    first = (i >> 1) == (j >> 1)
    levels = []
    lvl = 1
    while (2 << lvl) <= L:
        levels.append(((i >> (lvl + 1)) == (j >> (lvl + 1))) & ((i >> lvl) != (j >> lvl)))
        lvl += 1

    def stack(x):
        return jnp.concatenate([jnp.where(head0, x, 0.0), jnp.where(head0, 0.0, x)], axis=0)

    pairs = range(npair)
    units = range(n_chunks * npair)
    sls = [(slice((q // npair) * L, (q // npair + 1) * L), slice((q % npair) * LANES, (q % npair + 1) * LANES))
           for q in units]
    ar, bk, bk_h, vs, p_last = [], [], [], [], []
    for q in units:
        r, lw, k, v, a, b = (ref[0, sls[q][0], sls[q][1]] for ref in (r_ref, lw_ref, k_ref, v_ref, a_ref, b_ref))
        lw1 = lw.astype(BF16)
        rem = lw - lw1.astype(F32)
        lw2 = rem.astype(BF16)
        lw3 = (rem - lw2.astype(F32)).astype(BF16)
        cum = _nn(tri, lw1) + (_nn(tri, lw2) + _nn(tri, lw3))
        p_inc = jnp.exp(cum)
        p_exc = jnp.exp(cum - lw)
        p_inv = jnp.exp(-cum)
        pl_ = p_inc[L - 1:L, :]
        bk_f = jnp.concatenate([stack(b * p_inv), stack(k * p_inv)], axis=0)
        ar.append(jnp.concatenate([stack(a * p_exc), stack(r * p_inc)], axis=0).astype(BF16))
        bk.append(bk_f.astype(BF16))
        bk_h.append((bk_f * pl_).astype(BF16))
        vs.append(stack(v).astype(BF16))
        p_last.append(pl_)

    g = [_nt(ar[q], bk[q]) for q in units]
    a_ab = [jnp.where(strict, g[q][:n, :n], 0.0) for q in units]
    ak_rk = [jnp.concatenate([jnp.where(strict, g[q][:n, n:], 0.0),
                              jnp.where(incl, g[q][n:, n:], 0.0)], axis=0).astype(BF16) for q in units]
    a_rb = [jnp.where(incl, g[q][n:, :n], 0.0).astype(BF16) for q in units]

    def compress(full, size):
        return functools.reduce(lambda x, y: x + y, [full[i * size:(i + 1) * size] for i in range(n // size)])

    def expand(comp, size):
        lane_blk = lax.broadcasted_iota(jnp.int32, (size, n), 1) >> (size.bit_length() - 1)
        return jnp.concatenate([jnp.where(lane_blk == i, comp, 0.0) for i in range(n // size)], axis=0)

    sub = L // 2
    t_full = [eye + jnp.where(first, a_ab[q], 0.0) for q in units]
    t_comp = [compress(t_full[q], sub) for q in units]
    for off in levels[:-1]:
        x = [_nn(t_comp[q].astype(BF16), jnp.where(off, a_ab[q], 0.0).astype(BF16)).astype(BF16) for q in units]
        t_comp = [t_comp[q] + _nn(x[q], t_full[q].astype(BF16)) for q in units]
        t_full = [expand(t_comp[q], sub) for q in units]
    t_comp = [compress(t_full[q], L) for q in units]
    x = [_nn(t_comp[q].astype(BF16), jnp.where(levels[-1], a_ab[q], 0.0).astype(BF16)).astype(BF16) for q in units]
    t_comp = [(t_comp[q] + _nn(x[q], t_full[q].astype(BF16))).astype(BF16) for q in units]

    zt = [zt_ref[0, p] for p in pairs]
    for c in range(n_chunks):
        qs = [c * npair + p for p in pairs]
        y = [_nt(ar[q], zt[p].astype(BF16)) + _nn(ak_rk[q], vs[q]) for p, q in zip(pairs, qs)]
        u_b = [stack(_nn(t_comp[q], y[p][:n].astype(BF16))).astype(BF16) for p, q in zip(pairs, qs)]
        for p, q in zip(pairs, qs):
            o_st = y[p][n:] + _nn(a_rb[q], u_b[p])
            o_ref[0, sls[q][0], sls[q][1]] = o_st[:L] + o_st[L:]
            zt[p] = zt[p] * p_last[q] + _tn(jnp.concatenate([u_b[p], vs[q]], axis=0), bk_h[q])
    for p in pairs:
        zt_ref[0, p] = zt[p]


def _wkv_chunked(r, lw, k, v, a, b):
    bsz, t, d = r.shape
    npair = d // LANES
    tb = _pick_tile(t, tuple(WKV_CHUNK * c for c in WKV_CHUNKS_PER_STEP))
    spec = pl.BlockSpec((1, tb, d), lambda bi, c: (bi, c, 0))
    return pl.pallas_call(
        _wkv_chunk_kernel,
        grid=(bsz, t // tb),
        in_specs=[spec] * 6,
        out_specs=[spec, pl.BlockSpec((1, npair, LANES, LANES), lambda bi, c: (bi, 0, 0, 0))],
        out_shape=[jax.ShapeDtypeStruct((bsz, t, d), F32),
                   jax.ShapeDtypeStruct((bsz, npair, LANES, LANES), F32)],
        compiler_params=pltpu.CompilerParams(
            dimension_semantics=("parallel", "arbitrary"), vmem_limit_bytes=VMEM_LIMIT),
        name="wkv_chunked",
    )(r, lw, k, v, a, b)


def _wkv_step_kernel(s_ref, r_ref, lw_ref, k_ref, v_ref, a_ref, b_ref, so_ref, o_ref):
    _, bb, nh, e, _ = s_ref.shape
    half = LANES // 2
    ii = lax.broadcasted_iota(jnp.int32, (e, LANES), 0)
    jj = lax.broadcasted_iota(jnp.int32, (e, LANES), 1)
    eye_lo = ii == jj
    eye_hi = ii + half == jj

    for i in range(bb):
        units = []
        for h in range(nh):
            p, par = divmod(h, 2)
            sl = slice(p * LANES, (p + 1) * LANES)
            r, lw, k, v, a, b = (ref[i:i + 1, sl] for ref in (r_ref, lw_ref, k_ref, v_ref, a_ref, b_ref))
            w = jnp.exp(lw)
            if par == 0:
                rh, wh, kh, ah, bh = (z[:, :e] for z in (r, w, k, a, b))
            else:
                rh, wh, kh, ah, bh = (pltpu.roll(z, half, 1)[:, :e] for z in (r, w, k, a, b))
            s = s_ref[0, i, h]
            v_col = jnp.sum(jnp.where(eye_hi if par else eye_lo, v, 0.0), axis=1, keepdims=True)
            sa = jnp.sum(s * ah, axis=1, keepdims=True)
            units.append((s, rh, wh, kh, bh, v_col, sa))
        o_cols = []
        for h, (s, rh, wh, kh, bh, v_col, sa) in enumerate(units):
            s_new = s * wh + sa * bh + v_col * kh
            so_ref[0, i, h] = s_new
            o_cols.append(jnp.sum(s_new * rh, axis=1, keepdims=True))
        for p in range(nh // 2):
            o_ref[i:i + 1, p * LANES:(p + 1) * LANES] = (
                jnp.sum(jnp.where(eye_lo, o_cols[2 * p], 0.0), axis=0, keepdims=True)
                + jnp.sum(jnp.where(eye_hi, o_cols[2 * p + 1], 0.0), axis=0, keepdims=True))


def _wkv_step(states, layer, r, lw, k, v, a, b):
    _, n, h, e, _ = states.shape
    d = h * e
    bb = _pick_tile(n, (8,))
    s_spec = pl.BlockSpec((1, bb, h, e, e), lambda i: (layer, i, 0, 0, 0))
    row_spec = pl.BlockSpec((bb, d), lambda i: (i, 0))
    states, o = pl.pallas_call(
        _wkv_step_kernel,
        grid=(n // bb,),
        in_specs=[s_spec] + [row_spec] * 6,
        out_specs=[s_spec, row_spec],
        out_shape=[jax.ShapeDtypeStruct(states.shape, F32), jax.ShapeDtypeStruct((n, d), F32)],
        input_output_aliases={0: 0},
        compiler_params=pltpu.CompilerParams(
            dimension_semantics=("parallel",), vmem_limit_bytes=VMEM_LIMIT),
        name="wkv_step",
    )(states, r, lw, k, v, a, b)
    return o, states


def _fox_flash_kernel(*refs, tk):
    _flash_body(pl.program_id(1), pl.program_id(2), *refs, tk=tk)


def _flash_body(p, qi, q_ref, k_ref, v_ref, cq_ref, ck_ref, o_ref, s_ref, m_ref, l_ref, acc_ref, *, tk,
                side_work=()):
    tq = q_ref.shape[1]
    half = LANES // 2
    n_piece_rows = cq_ref.shape[2]
    nh = n_piece_rows // 3
    heads = range(2)

    q_head0 = lax.broadcasted_iota(jnp.int32, (tq, LANES), 1) < half
    k_head0 = lax.broadcasted_iota(jnp.int32, (tk, LANES), 1) < half

    prow = lax.broadcasted_iota(jnp.int32, (n_piece_rows, LANES), 0)
    plane = lax.broadcasted_iota(jnp.int32, (n_piece_rows, LANES), 1)
    piece, head = prow >> (nh.bit_length() - 1), prow & (nh - 1)
    base = jnp.where(head == 2 * p, half, jnp.where(head == 2 * p + 1, 0, -LANES))
    place_q = jnp.where(plane == base + piece, 1.0, 0.0).astype(BF16)
    place_k = jnp.where(plane == base + piece + 3, -1.0, 0.0).astype(BF16)
    slot = lax.broadcasted_iota(jnp.int32, (1, LANES), 1) & (half - 1)
    ones_q = jnp.where((slot >= 3) & (slot < 6), 1.0, 0.0)
    ones_k = jnp.where(slot < 3, 1.0, 0.0)

    q = q_ref[0]
    q_aug = (_nn(cq_ref[0], place_q) + ones_q).astype(BF16)
    q_ops = (jnp.where(q_head0, q, q_aug), jnp.where(q_head0, q_aug, q))

    m_ref[...] = jnp.full_like(m_ref, NEG_BIG)
    l_ref[...] = jnp.zeros_like(l_ref)
    acc_ref[...] = jnp.zeros_like(acc_ref)

    def rows(kj):
        return slice(kj * tk, (kj + 1) * tk)

    def scores(kj, slot_idx):
        k = k_ref[0, rows(kj), :]
        k_aug = (_nn(ck_ref[0, rows(kj), :], place_k) + ones_k).astype(BF16)
        k_ops = (jnp.where(k_head0, k, k_aug), jnp.where(k_head0, k_aug, k))
        for h in heads:
            s_ref[slot_idx, h] = _nt(q_ops[h], k_ops[h])

    def consume(kj, slot_idx, on_diagonal):
        v = v_ref[0, rows(kj), :]
        zero = jnp.zeros_like(v)
        v_own = (jnp.where(k_head0, v, zero), jnp.where(k_head0, zero, v))
        if on_diagonal:
            keep = (lax.broadcasted_iota(jnp.int32, (tq, tk), 0)
                    >= lax.broadcasted_iota(jnp.int32, (tq, tk), 1))
        alpha, pv = [], []
        for h in heads:
            s = s_ref[slot_idx, h]
            if on_diagonal:
                s = jnp.where(keep, s, NEG_BIG)
            m_prev = m_ref[h]
            m_new = jnp.maximum(m_prev, jnp.max(s, axis=1, keepdims=True))
            a = jnp.exp2(m_prev - m_new)
            pr = jnp.exp2(s - jnp.concatenate([m_new] * (tk // LANES), axis=1))
            l_ref[h] = a * l_ref[h] + jnp.sum(pr, axis=1, keepdims=True)
            m_ref[h] = m_new
            alpha.append(a)
            pv.append(_nn(pr.astype(BF16), v_own[h]))
        acc_ref[...] = jnp.where(q_head0, alpha[0], alpha[1]) * acc_ref[...] + (pv[0] + pv[1])

    side = list(side_work) + [lambda: None] * (3 - len(side_work))

    def run(n_before):
        scores(0, 0)
        side[0]()
        for kj in range(n_before):
            scores(kj + 1, (kj + 1) % 2)
            consume(kj, kj % 2, False)
        side[1]()
        consume(n_before, n_before % 2, True)
        side[2]()
        o_ref[0] = acc_ref[...] / jnp.where(q_head0, l_ref[0], l_ref[1])

    for n_before in range(k_ref.shape[1] // tk):
        pl.when(qi == n_before)(functools.partial(run, n_before))


def _fox_flash(q_b, k_b, v_b, pieces, tile=512):
    bsz, t, d = q_b.shape
    nh = pieces.shape[-1] // 3
    assert nh & (nh - 1) == 0, "head count must be a power of two"
    npair = d // LANES
    tq = tk = _pick_tile(t, (tile, 256, 128))
    q_map = lambda bi, p, qi: (bi, qi, p)
    seq_map = lambda bi, p, qi: (bi, 0, p)
    return pl.pallas_call(
        functools.partial(_fox_flash_kernel, tk=tk),
        grid=(bsz, npair, t // tq),
        in_specs=[pl.BlockSpec((1, tq, LANES), q_map),
                  pl.BlockSpec((1, t, LANES), seq_map),
                  pl.BlockSpec((1, t, LANES), seq_map),
                  pl.BlockSpec((1, tq, 3 * nh), lambda bi, p, qi: (bi, qi, 0)),
                  pl.BlockSpec((1, t, 3 * nh), lambda bi, p, qi: (bi, 0, 0))],
        out_specs=pl.BlockSpec((1, tq, LANES), q_map),
        out_shape=jax.ShapeDtypeStruct((bsz, t, d), F32),
        scratch_shapes=[pltpu.VMEM((2, 2, tq, tk), F32),
                        pltpu.VMEM((2, tq, LANES), F32), pltpu.VMEM((2, tq, LANES), F32),
                        pltpu.VMEM((tq, LANES), F32)],
        compiler_params=pltpu.CompilerParams(
            dimension_semantics=("parallel", "parallel", "parallel"), vmem_limit_bytes=VMEM_LIMIT),
        name="fox_flash",
    )(q_b, k_b, v_b, pieces, pieces)


def _fox_decode_kernel(pt_ref, q_ref, *refs, npg):
    k_refs, v_refs, f_refs = refs[:npg], refs[npg:2 * npg], refs[2 * npg:3 * npg]
    for phase in _decode_phases(pl.program_id(1), q_ref, k_refs, v_refs, f_refs, *refs[3 * npg:]):
        phase()


def _decode_phases(pg, q_ref, k_refs, v_refs, f_refs, tail_ref, kn_ref, vn_ref, o_ref, m_ref, l_ref, acc_ref):
    npg = len(k_refs)

    @pl.when(pg == 0)
    def _():
        m_ref[...] = jnp.full_like(m_ref, NEG_BIG)
        l_ref[...] = jnp.zeros_like(l_ref)
        acc_ref[...] = jnp.zeros_like(acc_ref)

    nh, e, page = k_refs[0].shape[2:]
    d = nh * e
    row = lax.broadcasted_iota(jnp.int32, (nh, d), 0)
    lane = lax.broadcasted_iota(jnp.int32, (nh, d), 1)
    own = (lane >= row * e) & (lane < (row + 1) * e)
    q_bd = jnp.where(own, q_ref[0], 0.0)
    q_b = q_bd.astype(BF16)
    vals = {}

    pos = lax.broadcasted_iota(jnp.int32, (nh, page), 1)
    eye_h = lax.broadcasted_iota(jnp.int32, (nh, nh), 0) == lax.broadcasted_iota(jnp.int32, (nh, nh), 1)

    def bias(t):
        lf = f_refs[t][0, 0]
        suffix = lf
        shift = 1
        while shift < page:
            suffix = suffix + jnp.where(pos < page - shift, pltpu.roll(suffix, page - shift, 1), 0.0)
            shift *= 2
        tail_col = jnp.sum(jnp.where(eye_h, tail_ref[0, t:t + 1, :], 0.0), axis=1, keepdims=True)
        return (suffix - lf) + tail_col

    def scores():
        vals["s"] = [_nn(q_b, k_refs[t][0, 0].reshape(d, page).astype(BF16)) + bias(t)
                     for t in range(npg)]

    def combine():
        s = vals["s"]
        m_prev = m_ref[...]
        m_new = m_prev
        for t in range(npg):
            m_new = jnp.maximum(m_new, jnp.max(s[t], axis=1, keepdims=True))
        alpha = jnp.exp(m_prev - m_new)
        l_new = alpha * l_ref[...]
        acc = alpha * acc_ref[...]
        for t in range(npg):
            pr = jnp.exp(s[t] - m_new)
            l_new = l_new + jnp.sum(pr, axis=1, keepdims=True)
            acc = acc + _nt(pr.astype(BF16), v_refs[t][0, 0].reshape(d, page).astype(BF16))
        l_ref[...] = l_new
        acc_ref[...] = acc
        m_ref[...] = m_new
        vals.update(m=m_new, l=l_new, acc=acc)

    def finish():
        m_new, l_new, acc = vals["m"], vals["l"], vals["acc"]
        s_new = jnp.sum(q_bd * kn_ref[0], axis=1, keepdims=True)
        m_fin = jnp.maximum(m_new, s_new)
        a_old = jnp.exp(m_new - m_fin)
        p_new = jnp.exp(s_new - m_fin)
        l_fin = l_new * a_old + p_new
        full = (acc * a_old + p_new * vn_ref[0]) / l_fin
        o_ref[0] = jnp.sum(jnp.where(own, full, 0.0), axis=0, keepdims=True)

    return scores, combine, finish


def _fox_decode(layer, q, k_new, v_new, cache_kt, cache_vt, cache_ft, tail, page_table):
    n, d = q.shape
    n_pages = page_table.shape[1]
    nh, e, page = cache_kt.shape[2:]
    npg = _pick_tile(n_pages, (8, 4, 2, 1))

    def kv_spec(t):
        return pl.BlockSpec((1, 1, nh, e, page), lambda b, pg, pt: (layer, pt[b, pg * npg + t], 0, 0, 0))

    def f_spec(t):
        return pl.BlockSpec((1, 1, nh, page), lambda b, pg, pt: (layer, pt[b, pg * npg + t], 0, 0))

    row_spec = pl.BlockSpec((1, 1, d), lambda b, pg, pt: (b, 0, 0))
    grid_spec = pltpu.PrefetchScalarGridSpec(
        num_scalar_prefetch=1,
        grid=(n, n_pages // npg),
        in_specs=[row_spec] + [kv_spec(t) for t in range(npg)] * 2 + [f_spec(t) for t in range(npg)]
                 + [pl.BlockSpec((1, npg, nh), lambda b, pg, pt: (b, pg, 0)), row_spec, row_spec],
        out_specs=row_spec,
        scratch_shapes=[pltpu.VMEM((nh, 1), F32), pltpu.VMEM((nh, 1), F32), pltpu.VMEM((nh, d), F32)],
    )
    o = pl.pallas_call(
        functools.partial(_fox_decode_kernel, npg=npg),
        grid_spec=grid_spec,
        out_shape=jax.ShapeDtypeStruct((n, 1, d), F32),
        compiler_params=pltpu.CompilerParams(
            dimension_semantics=("parallel", "arbitrary"), vmem_limit_bytes=VMEM_LIMIT),
        name="fox_decode",
    )(page_table, q.reshape(n, 1, d), *([cache_kt] * npg), *([cache_vt] * npg), *([cache_ft] * npg),
      tail, k_new.reshape(n, 1, d), v_new.reshape(n, 1, d))
    return o.reshape(n, d)


def _fox_attn_fused_kernel(pt_ref, fq_ref, fk_ref, fv_ref, fcq_ref, fck_ref, dq_ref, *refs, npg, tk, n_groups):
    k_refs, v_refs, f_refs = refs[:npg], refs[npg:2 * npg], refs[2 * npg:3 * npg]
    (tail_ref, kn_ref, vn_ref, o_ref, do_ref,
     s_ref, m_ref, l_ref, acc_ref, dm_ref, dl_ref, dacc_ref) = refs[3 * npg:]
    p, qi = pl.program_id(1), pl.program_id(2)
    step = (pl.program_id(0) * pl.num_programs(1) + p) * pl.num_programs(2) + qi
    decode = _decode_phases(step % n_groups, dq_ref, k_refs, v_refs, f_refs, tail_ref, kn_ref, vn_ref,
                            do_ref, dm_ref, dl_ref, dacc_ref)
    _flash_body(p, qi, fq_ref, fk_ref, fv_ref, fcq_ref, fck_ref, o_ref, s_ref, m_ref, l_ref, acc_ref, tk=tk,
                side_work=decode)


def _fox_attention(layer, q_b, k_b, v_b, pieces, dq, dk_new, dv_new, cache_kt, cache_vt, cache_ft, tail,
                   page_table, tile=512):
    bsz, t, d = q_b.shape
    n = dq.shape[0]
    nh, e, page = cache_kt.shape[2:]
    n_pages = page_table.shape[1]
    npair = d // LANES
    tq = tk = _pick_tile(t, (tile, 256, 128))
    nq = t // tq
    npg = _pick_tile(n_pages, (8, 4, 2, 1))
    n_groups = n_pages // npg
    if bsz * npair * nq != n * n_groups:
        return (_fox_flash(q_b, k_b, v_b, pieces, tile),
                _fox_decode(layer, dq, dk_new, dv_new, cache_kt, cache_vt, cache_ft, tail, page_table))

    def seq_of(bi, p, qi):
        return ((bi * npair + p) * nq + qi) // n_groups

    def grp_of(bi, p, qi):
        return ((bi * npair + p) * nq + qi) % n_groups

    def kv_spec(j):
        return pl.BlockSpec((1, 1, nh, e, page),
                            lambda bi, p, qi, pt: (layer, pt[seq_of(bi, p, qi), grp_of(bi, p, qi) * npg + j], 0, 0, 0))

    def f_spec(j):
        return pl.BlockSpec((1, 1, nh, page),
                            lambda bi, p, qi, pt: (layer, pt[seq_of(bi, p, qi), grp_of(bi, p, qi) * npg + j], 0, 0))

    q_map = lambda bi, p, qi, pt: (bi, qi, p)
    seq_map = lambda bi, p, qi, pt: (bi, 0, p)
    row_spec = pl.BlockSpec((1, 1, d), lambda bi, p, qi, pt: (seq_of(bi, p, qi), 0, 0))
    grid_spec = pltpu.PrefetchScalarGridSpec(
        num_scalar_prefetch=1,
        grid=(bsz, npair, nq),
        in_specs=[pl.BlockSpec((1, tq, LANES), q_map),
                  pl.BlockSpec((1, t, LANES), seq_map),
                  pl.BlockSpec((1, t, LANES), seq_map),
                  pl.BlockSpec((1, tq, 3 * nh), lambda bi, p, qi, pt: (bi, qi, 0)),
                  pl.BlockSpec((1, t, 3 * nh), lambda bi, p, qi, pt: (bi, 0, 0)),
                  row_spec]
                 + [kv_spec(j) for j in range(npg)] * 2 + [f_spec(j) for j in range(npg)]
                 + [pl.BlockSpec((1, npg, nh), lambda bi, p, qi, pt: (seq_of(bi, p, qi), grp_of(bi, p, qi), 0)),
                    row_spec, row_spec],
        out_specs=[pl.BlockSpec((1, tq, LANES), q_map), row_spec],
        scratch_shapes=[pltpu.VMEM((2, 2, tq, tk), F32),
                        pltpu.VMEM((2, tq, LANES), F32), pltpu.VMEM((2, tq, LANES), F32),
                        pltpu.VMEM((tq, LANES), F32),
                        pltpu.VMEM((nh, 1), F32), pltpu.VMEM((nh, 1), F32), pltpu.VMEM((nh, d), F32)],
    )
    o, do = pl.pallas_call(
        functools.partial(_fox_attn_fused_kernel, npg=npg, tk=tk, n_groups=n_groups),
        grid_spec=grid_spec,
        out_shape=[jax.ShapeDtypeStruct((bsz, t, d), F32), jax.ShapeDtypeStruct((n, 1, d), F32)],
        compiler_params=pltpu.CompilerParams(
            dimension_semantics=("arbitrary", "arbitrary", "arbitrary"), vmem_limit_bytes=VMEM_LIMIT),
        name="fox_attention",
    )(page_table, q_b, k_b, v_b, pieces, pieces, dq.reshape(n, 1, d),
      *([cache_kt] * npg), *([cache_vt] * npg), *([cache_ft] * npg), tail,
      dk_new.reshape(n, 1, d), dv_new.reshape(n, 1, d))
    return o, do.reshape(n, d)


def _rmsnorm(x, g):
    xf = x.astype(F32)
    y = xf * lax.rsqrt(jnp.mean(xf * xf, axis=-1, keepdims=True) + RMS_EPS)
    return (y * g.astype(F32)).astype(x.dtype)


def _rwkv_proj_kernel(x_ref, hp_ref, g_ref, mu_ref, w_ref, w0_ref, w1_ref, w2_ref, a0_ref, a1_ref, a2_ref,
                      kk_ref, ka_ref, seg_ref, segt_ref,
                      r_ref, lw_ref, kf_ref, v_ref, nkk_ref, kka_ref, gate_ref, carry_ref,
                      *, tiles_per_seq):
    x = x_ref[...]
    tm = x.shape[0]
    g = g_ref[...]
    h = x * lax.rsqrt(jnp.mean(x * x, axis=-1, keepdims=True) + RMS_EPS) * g
    if tiles_per_seq == 0:
        h_prev = hp_ref[...]
    else:
        first = pl.program_id(0) % tiles_per_seq == 0
        before = jnp.where(first, hp_ref[0], carry_ref[...])
        row = lax.broadcasted_iota(jnp.int32, x.shape, 0)
        h_prev = jnp.where(row == 0, before, pltpu.roll(h, 1, 0))
        carry_ref[...] = h[tm - 1:tm, :]
    xx = h_prev - h
    mix = lambda s: (h + xx * mu_ref[s:s + 1, :]).astype(BF16)
    k = _nn(mix(1), w_ref[1])
    w_mid = _nn(mix(4), w1_ref[...])
    a_mid = _nn(mix(5), a1_ref[...])
    r_ref[...] = _nn(mix(0), w_ref[0])
    kk = k * kk_ref[...]
    kk_sq = _seg_sum(kk * kk, seg_ref[...])
    z = w0_ref[...] + _nn(jnp.tanh(w_mid).astype(BF16), w2_ref[...])
    a_pre = _nn(a_mid.astype(BF16), a2_ref[...])
    v_ref[...] = _nn(mix(2), w_ref[2])
    kk = kk * _seg_bcast(1.0 / jnp.maximum(jnp.sqrt(kk_sq), 1e-12), segt_ref[...])
    gate_ref[...] = _nn(mix(3), w_ref[3])
    lw_ref[...] = -jnp.exp(jnp.minimum(z, 0.0) - jnp.log(1.0 + jnp.exp(-jnp.abs(z))) - 0.5)
    a = jax.nn.sigmoid(a0_ref[...] + a_pre)
    kf_ref[...] = k * (1.0 + (a - 1.0) * ka_ref[...])
    nkk_ref[...] = -kk
    kka_ref[...] = kk * a


def _rwkv_out_kernel(o_ref, r_ref, kf_ref, v_ref, gate_ref, x_ref, lw_ref, lb_ref, rk_ref, w_ref,
                     seg_ref, segt_ref, y_ref):
    seg = seg_ref[...]
    seg_t = segt_ref[...]
    tm, d = o_ref.shape
    inv_e = 1.0 / (d // seg.shape[1])
    n_groups = 2 if tm % 16 == 0 else 1
    rows = [slice(i * (tm // n_groups), (i + 1) * (tm // n_groups)) for i in range(n_groups)]
    o = [o_ref[rs, :] for rs in rows]
    o_sum = [_seg_sum(x, seg) for x in o]
    rk_sum = [_seg_sum(r_ref[rs, :] * kf_ref[rs, :] * rk_ref[...], seg) for rs in rows]
    cen = [x - _seg_bcast(s * inv_e, seg_t) for x, s in zip(o, o_sum)]
    var = [_seg_sum(c * c, seg) * inv_e for c in cen]
    bonus = [_seg_bcast(s, seg_t) * v_ref[rs, :] for s, rs in zip(rk_sum, rows)]
    gn = [c * _seg_bcast(lax.rsqrt(vr + LNX_EPS), seg_t) * lw_ref[...] + lb_ref[...] for c, vr in zip(cen, var)]
    for rs, gn_i, bonus_i in zip(rows, gn, bonus):
        g = gate_ref[rs, :]
        y = ((gn_i + bonus_i) * (g * jax.nn.sigmoid(g))).astype(BF16)
        y_ref[rs, :] = x_ref[rs, :] + _nn(y, w_ref[...])


def _rwkv_layer(x, h_prev0, states, layer, prm, nh):
    (norm, mu, w_in, w0, w1, w2, a0, a1, a2, k_k, k_a, r_k, lnx_w, lnx_b, w_out) = prm
    bsz, t, d = x.shape
    e = d // nh
    m = bsz * t
    seg, seg_t = _seg_mats(d, nh)
    row = lambda z: z.reshape(1, -1).astype(F32)
    full = lambda arr: pl.BlockSpec(arr.shape, lambda i: (0,) * arr.ndim, pipeline_mode=pl.Buffered(1))
    x2 = x.reshape(m, d)
    if t == 1:
        tm = _pick_tile(m, (256, 128))
        tiles_per_seq = 0
        hp = h_prev0.astype(F32)
        hp_spec = pl.BlockSpec((tm, d), lambda i: (i, 0))
    else:
        tm = _pick_tile(t, (256, 128))
        tiles_per_seq = t // tm
        hp = h_prev0.astype(F32).reshape(bsz, 1, d)
        hp_spec = pl.BlockSpec((1, 1, d), lambda i: (i // tiles_per_seq, 0, 0))
    tile = pl.BlockSpec((tm, d), lambda i: (i, 0))
    consts = [row(norm), mu.astype(F32), w_in.astype(BF16), row(w0), w1.astype(BF16), w2.astype(BF16),
              row(a0), a1.astype(BF16), a2.astype(BF16), row(k_k), row(k_a), seg, seg_t]
    r, lw, kf, v, neg_kk, kk_a, gate = pl.pallas_call(
        functools.partial(_rwkv_proj_kernel, tiles_per_seq=tiles_per_seq),
        grid=(m // tm,),
        in_specs=[tile, hp_spec] + [full(c) for c in consts],
        out_specs=[tile] * 7,
        out_shape=[jax.ShapeDtypeStruct((m, d), F32)] * 7,
        scratch_shapes=[pltpu.VMEM((1, d), F32)],
        compiler_params=pltpu.CompilerParams(
            dimension_semantics=("arbitrary",), vmem_limit_bytes=VMEM_LIMIT),
        name="rwkv_proj",
    )(x2, hp, *consts)

    if states is None:
        seq = lambda z: z.reshape(bsz, t, d)
        o, zt = _wkv_chunked(seq(r), seq(lw), seq(kf), seq(v), seq(neg_kk), seq(kk_a))
        zt = zt.reshape(bsz, d // LANES, 2, e, 2, e)
        s_fin = jnp.stack([zt[:, :, 0, :, 0, :], zt[:, :, 1, :, 1, :]], axis=2).reshape(bsz, nh, e, e)
    else:
        o, s_fin = _wkv_step(states, layer, r, lw, kf, v, neg_kk, kk_a)

    tmo = _pick_tile(m, (512, 256, 128))
    tile_o = pl.BlockSpec((tmo, d), lambda i: (i, 0))
    consts_o = [row(lnx_w), row(lnx_b), row(r_k), w_out.astype(BF16), seg, seg_t]
    x_new = pl.pallas_call(
        _rwkv_out_kernel,
        grid=(m // tmo,),
        in_specs=[tile_o] * 6 + [full(c) for c in consts_o],
        out_specs=tile_o,
        out_shape=jax.ShapeDtypeStruct((m, d), F32),
        compiler_params=pltpu.CompilerParams(
            dimension_semantics=("parallel",), vmem_limit_bytes=VMEM_LIMIT),
        name="rwkv_out",
    )(o.reshape(m, d), r, kf, v, gate, x2, *consts_o)
    h_last = _rmsnorm(x[:, -1, :], norm)
    return x_new.reshape(bsz, t, d), s_fin, h_last


def _split2(x):
    hi = x.astype(BF16)
    return hi, (x - hi.astype(F32)).astype(BF16)


def _seg_sum(x, seg):
    hi, lo = _split2(x)
    return _nn(hi, seg) + _nn(lo, seg)


def _seg_bcast(y, seg_t):
    hi, lo = _split2(y)
    return _nn(hi, seg_t) + _nn(lo, seg_t)


def _seg_mats(d, nh):
    lane_head = jnp.arange(d, dtype=jnp.int32) // (d // nh)
    seg = (lane_head[:, None] == jnp.arange(nh, dtype=jnp.int32)[None, :]).astype(BF16)
    return seg, seg.T


def _fox_proj_kernel(x_ref, g_ref, w_ref, wf_ref, bf_ref, qn_ref, kn_ref, seg_ref, segt_ref,
                     q_ref, k_ref, v_ref, gate_ref, lf_ref, *attn_refs, q_scale, tiles_per_seq):
    x = x_ref[...]
    tm = x.shape[0]
    nh = lf_ref.shape[1]
    d = x.shape[1]
    hb = (x * lax.rsqrt(jnp.mean(x * x, axis=-1, keepdims=True) + RMS_EPS) * g_ref[...]).astype(BF16)
    seg = seg_ref[...]
    seg_t = segt_ref[...]
    inv_e = 1.0 / (d // seg.shape[1])

    q = _nn(hb, w_ref[:, 0:d])
    k = _nn(hb, w_ref[:, d:2 * d])
    q_ms = _seg_sum(q * q, seg)
    v = _nn(hb, w_ref[:, 2 * d:3 * d])
    k_ms = _seg_sum(k * k, seg)
    q = q * _seg_bcast(lax.rsqrt(q_ms * inv_e + RMS_EPS), seg_t) * qn_ref[...]
    gate_ref[...] = _nn(hb, w_ref[:, 3 * d:4 * d])
    k = k * _seg_bcast(lax.rsqrt(k_ms * inv_e + RMS_EPS), seg_t) * kn_ref[...]
    k_ref[...] = k
    v_ref[...] = v
    f = _nn(hb, wf_ref[...]) + bf_ref[...]
    logf = jnp.minimum(f, 0.0) - jnp.log(1.0 + jnp.exp(-jnp.abs(f)))
    lf_ref[...] = logf[:, :nh]
    if not attn_refs:
        q_ref[...] = q
        return
    kb_ref, vb_ref, pieces_ref, carry_ref = attn_refs
    q_ref[...] = (q * q_scale).astype(BF16)
    kb_ref[...] = k.astype(BF16)
    vb_ref[...] = v.astype(BF16)
    tri = (lax.broadcasted_iota(jnp.int32, (tm, tm), 0)
           >= lax.broadcasted_iota(jnp.int32, (tm, tm), 1)).astype(BF16)
    f1 = logf.astype(BF16)
    rem = logf - f1.astype(F32)
    f2 = rem.astype(BF16)
    f3 = (rem - f2.astype(F32)).astype(BF16)
    first = pl.program_id(0) % tiles_per_seq == 0
    c = _nn(tri, f1) + (_nn(tri, f2) + _nn(tri, f3)) + jnp.where(first, 0.0, carry_ref[...])
    carry_ref[...] = c[tm - 1:tm, :]
    top16 = lambda z: lax.bitcast_convert_type(
        lax.bitcast_convert_type(z, jnp.uint32) & jnp.uint32(0xFFFF0000), F32)
    c2 = c * LOG2E
    hi = top16(c2)
    mid = top16(c2 - hi)
    lane = lax.broadcasted_iota(jnp.int32, c2.shape, 1)
    pieces_ref[...] = jnp.where(lane < nh, hi, jnp.where(lane < 2 * nh, mid, top16(c2 - hi - mid))).astype(BF16)


def _fox_project(x, norm, w_in, b_f, qn_g, kn_g, nh, attn_q_scale=None):
    bsz, t, d = x.shape
    m = bsz * t
    for_attn = attn_q_scale is not None
    tm = _pick_tile(t if for_attn else m, (256, 128))
    n_f = 3 * nh if for_attn else nh
    seg, seg_t = _seg_mats(d, nh)
    row = lambda z: z.reshape(1, -1).astype(F32)
    full = lambda a: pl.BlockSpec(a.shape, lambda i: (0,) * a.ndim, pipeline_mode=pl.Buffered(1))
    w_f = w_in[:, 4 * d:].astype(BF16)
    consts = [row(norm), w_in[:, :4 * d].astype(BF16), jnp.tile(w_f, (1, n_f // nh)), row(jnp.tile(b_f, n_f // nh)),
              row(jnp.tile(qn_g, nh)), row(jnp.tile(kn_g, nh)), seg, seg_t]
    tile = pl.BlockSpec((tm, d), lambda i: (i, 0))
    narrow = lambda w: pl.BlockSpec((tm, w), lambda i: (i, 0))
    attn_specs = [tile, tile, narrow(n_f)] if for_attn else []
    attn_shapes = ([jax.ShapeDtypeStruct((m, d), BF16)] * 2 + [jax.ShapeDtypeStruct((m, n_f), BF16)]) if for_attn else []
    outs = pl.pallas_call(
        functools.partial(_fox_proj_kernel, q_scale=attn_q_scale, tiles_per_seq=t // tm),
        grid=(m // tm,),
        in_specs=[tile] + [full(a) for a in consts],
        out_specs=[tile] * 4 + [narrow(nh)] + attn_specs,
        out_shape=([jax.ShapeDtypeStruct((m, d), BF16 if for_attn else F32)]
                   + [jax.ShapeDtypeStruct((m, d), F32)] * 3 + [jax.ShapeDtypeStruct((m, nh), F32)] + attn_shapes),
        scratch_shapes=[pltpu.VMEM((1, n_f), F32)] if for_attn else [],
        compiler_params=pltpu.CompilerParams(
            dimension_semantics=("arbitrary" if for_attn else "parallel",), vmem_limit_bytes=VMEM_LIMIT),
        name="fox_proj",
    )(x.reshape(m, d), *consts)
    q, k, v, gate, logf = outs[:5]
    seq = lambda z: z.reshape(bsz, t, z.shape[-1])
    return seq(q), seq(k), seq(v), gate, logf.reshape(bsz, t, nh), tuple(seq(z) for z in outs[5:])


def _gate_out_kernel(o_ref, gate_ref, x_ref, w_ref, y_ref):
    g = gate_ref[...]
    y = (o_ref[...] * (g * jax.nn.sigmoid(g))).astype(BF16)
    y_ref[...] = x_ref[...] + _nn(y, w_ref[...])


def _fox_finish(x, o, gate, w_out):
    bsz, t, d = x.shape
    m = bsz * t
    tm = _pick_tile(m, (512, 256, 128))
    tile = pl.BlockSpec((tm, d), lambda i: (i, 0))
    y = pl.pallas_call(
        _gate_out_kernel,
        grid=(m // tm,),
        in_specs=[tile, tile, tile, pl.BlockSpec((d, d), lambda i: (0, 0))],
        out_specs=tile,
        out_shape=jax.ShapeDtypeStruct((m, d), F32),
        compiler_params=pltpu.CompilerParams(
            dimension_semantics=("parallel",), vmem_limit_bytes=VMEM_LIMIT),
        name="gate_out",
    )(o.reshape(m, d), gate, x.reshape(m, d), w_out.astype(BF16))
    return y.reshape(bsz, t, d)


def kernel(x_prompt, x_sample, state_wkv, state_shift, cache_k, cache_v, cache_logf, page_table,
           norm_a, mu_a, w_in_a, w0_a, w1_a, w2_a, a0_a, a1_a, a2_a, kk_a, ka_a, rk_a, lnx_w_a, lnx_b_a, w_out_a,
           norm_b, w_in_b, bf_b, qn_b, kn_b, w_out_b):
    bsz, t, d = x_prompt.shape
    nb, ts, _ = x_sample.shape
    assert ts == 1, "the sample group carries one new token per sequence"
    nh, e = rk_a.shape[1], rk_a.shape[2]
    assert d == nh * e and 2 * e == LANES and t % WKV_CHUNK == 0
    depth = norm_a.shape[0] + norm_b.shape[0]
    scale = e ** -0.5
    ckt = jnp.transpose(cache_k, (0, 1, 3, 4, 2))
    cvt = jnp.transpose(cache_v, (0, 1, 3, 4, 2))
    cft = jnp.transpose(cache_logf, (0, 1, 3, 2)).astype(F32)
    page_mass = jnp.sum(cft, axis=-1)

    xp, xs = x_prompt, x_sample
    kp_l, vp_l, fp_l, sp_l, hp_l = [], [], [], [], []
    ks_l, vs_l, fs_l, hs_l = [], [], [], []
    states_s = state_wkv.astype(F32)
    for i in range(depth):
        j = i // 2
        if i % 2 == 0:
            prm = (norm_a[j], mu_a[j], w_in_a[j], w0_a[j], w1_a[j], w2_a[j], a0_a[j], a1_a[j], a2_a[j],
                   kk_a[j], ka_a[j], rk_a[j], lnx_w_a[j], lnx_b_a[j], w_out_a[j])
            xp, s_p, l_p = _rwkv_layer(xp, jnp.zeros((bsz, d), xp.dtype), None, j, prm, nh)
            xs, states_s, l_s = _rwkv_layer(xs, state_shift[j], states_s, j, prm, nh)
            sp_l.append(s_p); hp_l.append(l_p); hs_l.append(l_s)
        else:
            qp_b, kp, vp, gp, lfp, attn_ops = _fox_project(xp, norm_b[j], w_in_b[j], bf_b[j], qn_b[j], kn_b[j], nh,
                                                        attn_q_scale=scale * LOG2E)
            qs, ksn, vsn, gs, lfs, _ = _fox_project(xs, norm_b[j], w_in_b[j], bf_b[j], qn_b[j], kn_b[j], nh)
            seq_mass = page_mass[j][page_table]
            later = jnp.cumsum(seq_mass[:, ::-1, :], axis=1)[:, ::-1, :] - seq_mass
            tail = later + lfs.reshape(nb, 1, nh)
            op, osm = _fox_attention(j, qp_b, *attn_ops, (qs * scale).reshape(nb, d), ksn.reshape(nb, d),
                                     vsn.reshape(nb, d), ckt, cvt, cft, tail, page_table)
            xp = _fox_finish(xp, op, gp, w_out_b[j])
            xs = _fox_finish(xs, osm.reshape(nb, 1, d), gs, w_out_b[j])
            hd4 = lambda z, n_, t_: z.reshape(n_, t_, nh, e)
            kp_l.append(hd4(kp, bsz, t)); vp_l.append(hd4(vp, bsz, t)); fp_l.append(lfp)
            ks_l.append(hd4(ksn, nb, ts)); vs_l.append(hd4(vsn, nb, ts)); fs_l.append(lfs)
    return (xp, xs,
            jnp.stack(kp_l), jnp.stack(vp_l), jnp.stack(fp_l), jnp.stack(sp_l), jnp.stack(hp_l),
            jnp.stack(ks_l), jnp.stack(vs_l), jnp.stack(fs_l), states_s, jnp.stack(hs_l))
```
